```python
import math
import jax
import jax.numpy as jnp
from jax import lax
import numpy as np

D_MODEL = 2048
BATCH = 2
SEQ = 4096
DEPTH = 2
DEC_BATCH = 32
DEC_SEQ = 4
PAST_LEN = 8192
PAGE_SIZE = 128

HEAD_DIM = 64
ATTN_GROUPS = ((128, 1), (512, 4), (2048, 16))
N_GROUPS = 3
ATTN_HPG = 4
ATTN_HEADS = N_GROUPS * ATTN_HPG
ATTN_WIDTH = ATTN_HEADS * HEAD_DIM
ATTN_SCALE = HEAD_DIM ** -0.5
ROPE_THETA = 10000.0
QBLOCK = 128

MLSTM_HEADS = 10
MLSTM_WIDTH = MLSTM_HEADS * HEAD_DIM
MLSTM_CONV = 4
MLSTM_CHUNK = 128
MLSTM_F_BIAS = 3.0

GLA_HEADS = 10
GLA_DK = 32
GLA_DV = 64
GLA_K_WIDTH = GLA_HEADS * GLA_DK
GLA_V_WIDTH = GLA_HEADS * GLA_DV
GLA_RANK = 16
GLA_TAU = 16.0
GLA_CHUNK = 64

MIX_WIDTH = ATTN_WIDTH + MLSTM_WIDTH + GLA_V_WIDTH
D_FF = 5632
FFN_CONV = 3
EPS = 1e-6

IN_SPLITS = (ATTN_WIDTH, ATTN_WIDTH, ATTN_WIDTH, 2 * MLSTM_WIDTH, MLSTM_WIDTH, MLSTM_HEADS, MLSTM_HEADS,
             MLSTM_WIDTH, GLA_K_WIDTH, GLA_K_WIDTH, GLA_V_WIDTH, GLA_V_WIDTH, GLA_RANK)
N_IN = 6820
MLSTM_F_OFFSET = 3 * ATTN_WIDTH + 3 * MLSTM_WIDTH + MLSTM_HEADS

kernel_name = 'hybrid_dilattn_mlstm_gla_convffn_step'


def rms_norm(x, g):
    xf = x.astype(jnp.float32)
    y = xf * lax.rsqrt(jnp.mean(xf * xf, axis=-1, keepdims=True) + EPS)
    return (y * g.astype(jnp.float32)).astype(x.dtype)


def rope(x, pos):
    half = HEAD_DIM // 2
    inv_freq = jnp.power(ROPE_THETA, -jnp.arange(half, dtype=jnp.float32) / half)
    ang = pos.astype(jnp.float32)[:, None] * inv_freq[None, :]
    cos = jnp.cos(ang)[None, :, None, :]
    sin = jnp.sin(ang)[None, :, None, :]
    xf = x.astype(jnp.float32)
    x1, x2 = xf[..., :half], xf[..., half:]
    return jnp.concatenate([x1 * cos - x2 * sin, x2 * cos + x1 * sin], axis=-1).astype(x.dtype)


def split_cols(x, sizes):
    out, off = [], 0
    for s in sizes:
        out.append(x[..., off:off + s])
        off += s
    return out


def causal_dwconv(x, buf, w, b):
    K = w.shape[0]
    T = x.shape[1]
    xp = jnp.concatenate([buf.astype(x.dtype), x], axis=1)
    y = b
    for j in range(K):
        y = y + xp[:, j:j + T] * w[j]
    return y, xp[:, T:]


def softmax_lse(s):
    mx = jnp.max(s, axis=-1, keepdims=True)
    e = jnp.exp(s - mx)
    den = jnp.sum(e, axis=-1, keepdims=True)
    return e / den, (mx + jnp.log(den))[..., 0]


def dilated_attn_prompt(q, kv, window, dil):
    B, S, H, Dh = q.shape
    L = S // dil
    nw = window // dil
    bq = math.gcd(L, QBLOCK)
    nb = L // bq
    qs = q.reshape(B, L, dil, H, Dh).transpose(0, 2, 1, 3, 4).reshape(B, dil, nb, bq, H, Dh)
    kvs = kv.reshape(B, L, dil, 2, H, Dh).transpose(0, 2, 1, 3, 4, 5)
    kvp = jnp.pad(kvs, ((0, 0), (0, 0), (nw, 0), (0, 0), (0, 0), (0, 0)))
    kidx = jnp.arange(nb)[:, None] * bq + jnp.arange(bq + nw)[None, :]
    kvb = kvp[:, :, kidx]
    qi = jnp.arange(nb)[:, None] * bq + jnp.arange(bq)[None, :]
    kj = kidx - nw
    dist = qi[:, :, None] - kj[:, None, :]
    mask = (kj[:, None, :] >= 0) & (dist >= 0) & (dist <= nw)
    s = jnp.einsum('brnqhd,brnkhd->brnhqk', qs, kvb[..., 0, :, :],
                   preferred_element_type=jnp.float32) * ATTN_SCALE
    s = jnp.where(mask[:, None], s, -jnp.inf)
    p, lse = softmax_lse(s)
    o = jnp.einsum('brnhqk,brnkhd->brnqhd', p.astype(kv.dtype), kvb[..., 1, :, :],
                   preferred_element_type=jnp.float32)
    o = o.reshape(B, dil, L, H, Dh).transpose(0, 2, 1, 3, 4).reshape(B, S, H, Dh)
    lse = lse.transpose(0, 1, 2, 4, 3).reshape(B, dil, L, H).transpose(0, 2, 1, 3).reshape(B, S, H)
    return o, lse


def dilated_attn_sample(q, kvc, window, dil):
    B, T, H, Dh = q.shape
    Lb = kvc.shape[1] - T
    nw = window // dil
    idx = Lb + jnp.arange(T)[:, None] - dil * jnp.arange(nw + 1)[None, :]
    valid = idx >= 0
    g = kvc[:, jnp.maximum(idx, 0)]
    s = jnp.einsum('bthd,btkhd->bthk', q, g[..., 0, :, :],
                   preferred_element_type=jnp.float32) * ATTN_SCALE
    s = jnp.where(valid[:, None, :], s, -jnp.inf)
    p, lse = softmax_lse(s)
    o = jnp.einsum('bthk,btkhd->bthd', p.astype(kvc.dtype), g[..., 1, :, :],
                   preferred_element_type=jnp.float32)
    return o, lse


def mlstm_chunkwise(q, k, v, ig, lf, C0, n0, m0, chunk):
    B, S, H, D = q.shape
    nc = S // chunk
    f32 = jnp.float32

    def to_chunks(a):
        return a.reshape((B, nc, chunk) + a.shape[2:]).swapaxes(0, 1)

    causal = jnp.tril(jnp.ones((chunk, chunk), dtype=bool))

    def step(carry, xs):
        C, n, m = carry
        qq, kk, vv, ii, ff = xs
        qq, kk, vv = qq.astype(f32), kk.astype(f32), vv.astype(f32)
        F = jnp.cumsum(ff, axis=1)
        logw = jnp.where(causal[None, :, :, None],
                         F[:, :, None, :] - F[:, None, :, :] + ii[:, None, :, :], -jnp.inf)
        inter = F + m[:, None, :]
        mt = jnp.maximum(jnp.max(logw, axis=2), inter)
        A = jnp.exp(logw - mt[:, :, None, :]) * jnp.einsum('bthd,bshd->btsh', qq, kk)
        g = jnp.exp(inter - mt)
        num = jnp.einsum('btsh,bshd->bthd', A, vv) + g[..., None] * jnp.einsum('bthd,bhde->bthe', qq, C)
        den = jnp.sum(A, axis=2) + g * jnp.einsum('bthd,bhd->bth', qq, n)
        hh = num / jnp.maximum(jnp.abs(den), jnp.exp(-mt))[..., None]
        mL = mt[:, -1]
        wL = jnp.exp(F[:, -1:] - F + ii - mL[:, None])
        gL = jnp.exp(F[:, -1] + m - mL)
        C = gL[..., None, None] * C + jnp.einsum('bsh,bshd,bshe->bhde', wL, kk, vv)
        n = gL[..., None] * n + jnp.einsum('bsh,bshd->bhd', wL, kk)
        return (C, n, mL), hh

    carry0 = (C0.astype(f32), n0.astype(f32), m0.astype(f32))
    (C, n, m), hs = lax.scan(step, carry0, tuple(to_chunks(a) for a in (q, k, v, ig, lf)))
    return hs.swapaxes(0, 1).reshape(B, S, H, D), C, n, m


def gla_chunkwise(q, k, v, la, S0, chunk):
    B, T, H, DK = q.shape
    DV = v.shape[-1]
    nc = T // chunk
    f32 = jnp.float32

    def to_chunks(a):
        return a.reshape((B, nc, chunk) + a.shape[2:]).swapaxes(0, 1)

    causal = jnp.tril(jnp.ones((chunk, chunk), dtype=bool))

    def step(Sst, xs):
        qq, kk, vv, aa = xs
        qq, kk, vv = qq.astype(f32), kk.astype(f32), vv.astype(f32)
        Bc = jnp.cumsum(aa, axis=1)
        o = jnp.einsum('bthk,bhkv->bthv', qq * jnp.exp(Bc), Sst)
        decay = jnp.exp(jnp.where(causal[None, :, :, None, None],
                                  Bc[:, :, None] - Bc[:, None, :], -jnp.inf))
        A = jnp.einsum('bthk,bshk,btshk->btsh', qq, kk, decay)
        o = o + jnp.einsum('btsh,bshv->bthv', A, vv)
        BL = Bc[:, -1]
        Sst = jnp.exp(BL)[..., None] * Sst + jnp.einsum('bshk,bshv->bhkv', kk * jnp.exp(BL[:, None] - Bc), vv)
        return Sst, o

    S, outs = lax.scan(step, S0.astype(f32), tuple(to_chunks(a) for a in (q, k, v, la)))
    return outs.swapaxes(0, 1).reshape(B, T, H, DV), S


def zero_state(B, dtype):
    return {
        'C': jnp.zeros((B, MLSTM_HEADS, HEAD_DIM, HEAD_DIM), dtype),
        'n': jnp.zeros((B, MLSTM_HEADS, HEAD_DIM), dtype),
        'm': jnp.zeros((B, MLSTM_HEADS), dtype),
        'mconv': jnp.zeros((B, MLSTM_CONV - 1, 2 * MLSTM_WIDTH), dtype),
        'S': jnp.zeros((B, GLA_HEADS, GLA_DK, GLA_DV), dtype),
        'fconv': jnp.zeros((B, FFN_CONV - 1, D_FF), dtype),
    }


def trunk_layer(x, c, pos, p, st, prompt):
    B, T, _ = x.shape
    f32 = jnp.float32
    mod = jnp.einsum('bd,de->be', jax.nn.silu(c), p['w_ada']) + p['b_ada']
    sh1, sc1, gt1, sh2, sc2, gt2 = jnp.split(mod[:, None, :], 6, axis=-1)

    h = rms_norm(x, p['g_norm1']) * (1.0 + sc1) + sh1
    proj = jnp.einsum('btd,de->bte', h, p['w_in']) + p['b_in']
    aq, ak, av, mqk, mv, mi, mf, mo, gq, gk, gv, gg, ga = split_cols(proj, IN_SPLITS)
    new = {}

    aq = rope(aq.reshape(B, T, ATTN_HEADS, HEAD_DIM), pos)
    ak = rope(ak.reshape(B, T, ATTN_HEADS, HEAD_DIM), pos)
    av = av.reshape(B, T, ATTN_HEADS, HEAD_DIM)
    outs, lses = [], []
    for g, (win, dil) in enumerate(ATTN_GROUPS):
        hs = slice(g * ATTN_HPG, (g + 1) * ATTN_HPG)
        kv = jnp.stack([ak[:, :, hs], av[:, :, hs]], axis=2)
        if prompt:
            o, lse = dilated_attn_prompt(aq[:, :, hs], kv, win, dil)
            new[f'win{g}'] = kv[:, T - min(win, T):]
        else:
            kvc = jnp.concatenate([st[f'win{g}'].astype(kv.dtype), kv], axis=1)
            o, lse = dilated_attn_sample(aq[:, :, hs], kvc, win, dil)
            new[f'win{g}'] = kvc[:, T:]
        outs.append(o)
        lses.append(lse)
    alpha = jax.nn.softmax(jnp.stack(lses, axis=0), axis=0)
    y_attn = jnp.concatenate([outs[g] * alpha[g][..., None] for g in range(N_GROUPS)], axis=2)
    y_attn = y_attn.reshape(B, T, ATTN_WIDTH).astype(x.dtype)

    mqk, new['mconv'] = causal_dwconv(mqk, st['mconv'], p['w_mconv'], p['b_mconv'])
    mqk = jax.nn.silu(mqk)
    mq = mqk[..., :MLSTM_WIDTH].reshape(B, T, MLSTM_HEADS, HEAD_DIM)
    mk = mqk[..., MLSTM_WIDTH:].reshape(B, T, MLSTM_HEADS, HEAD_DIM) * (HEAD_DIM ** -0.5)
    mv = mv.reshape(B, T, MLSTM_HEADS, HEAD_DIM)
    ig = mi.astype(f32)
    lf = jax.nn.log_sigmoid(mf.astype(f32))
    hm, C, n, m = mlstm_chunkwise(mq, mk, mv, ig, lf, st['C'], st['n'], st['m'], math.gcd(T, MLSTM_CHUNK))
    new['C'] = C.astype(x.dtype)
    new['n'] = n.astype(x.dtype)
    new['m'] = m.astype(x.dtype)
    hm = rms_norm(hm, p['g_mlstm'].reshape(MLSTM_HEADS, HEAD_DIM)).astype(x.dtype)
    y_mlstm = hm.reshape(B, T, MLSTM_WIDTH) * jax.nn.sigmoid(mo)

    gq = gq.reshape(B, T, GLA_HEADS, GLA_DK) * (GLA_DK ** -0.5)
    gk = gk.reshape(B, T, GLA_HEADS, GLA_DK)
    gv = gv.reshape(B, T, GLA_HEADS, GLA_DV)
    la = jax.nn.log_sigmoid((jnp.einsum('btr,rk->btk', ga, p['w_gla_a2']) + p['b_gla_a2']).astype(f32)) / GLA_TAU
    la = la.reshape(B, T, GLA_HEADS, GLA_DK)
    og, S = gla_chunkwise(gq, gk, gv, la, st['S'], math.gcd(T, GLA_CHUNK))
    new['S'] = S.astype(x.dtype)
    og = rms_norm(og, p['g_gla'].reshape(GLA_HEADS, GLA_DV)).astype(x.dtype)
    y_gla = og.reshape(B, T, GLA_V_WIDTH) * jax.nn.silu(gg)

    mix = jnp.concatenate([y_attn, y_mlstm, y_gla], axis=-1)
    x = x + gt1 * jnp.einsum('btm,md->btd', mix, p['w_out'])

    h2 = rms_norm(x, p['g_norm2']) * (1.0 + sc2) + sh2
    u = jnp.einsum('btd,df->btf', h2, p['w_ff_in'])
    gate, new['fconv'] = causal_dwconv(u[..., :D_FF], st['fconv'], p['w_fconv'], p['b_fconv'])
    x = x + gt2 * jnp.einsum('btf,fd->btd', jax.nn.silu(gate) * u[..., D_FF:], p['w_ff_out'])
    return x, new


def setup_inputs(seed: int = 0) -> dict:
    key = jax.random.key(seed)
    keys = iter(jax.random.split(key, 40))

    def nrm(shape, scale=1.0):
        return jax.random.normal(next(keys), shape, jnp.float32) * scale

    def gain(shape):
        return 1.0 + nrm(shape, 0.02)

    d = {}
    d['x_prompt'] = nrm((BATCH, SEQ, D_MODEL))
    d['x_sample'] = nrm((DEC_BATCH, DEC_SEQ, D_MODEL))
    d['c_prompt'] = nrm((BATCH, D_MODEL))
    d['c_sample'] = nrm((DEC_BATCH, D_MODEL))
    for g, (win, _) in enumerate(ATTN_GROUPS):
        d[f'cache_win{g}_kv'] = nrm((DEPTH, DEC_BATCH, min(win, PAST_LEN), 2, ATTN_HPG, HEAD_DIM))
    d['state_mlstm_C'] = nrm((DEPTH, DEC_BATCH, MLSTM_HEADS, HEAD_DIM, HEAD_DIM), 0.5)
    d['state_mlstm_n'] = nrm((DEPTH, DEC_BATCH, MLSTM_HEADS, HEAD_DIM), 0.5)
    d['state_mlstm_m'] = nrm((DEPTH, DEC_BATCH, MLSTM_HEADS))
    d['state_mlstm_conv'] = nrm((DEPTH, DEC_BATCH, MLSTM_CONV - 1, 2 * MLSTM_WIDTH))
    d['state_gla_S'] = nrm((DEPTH, DEC_BATCH, GLA_HEADS, GLA_DK, GLA_DV))
    d['state_ffn_conv'] = nrm((DEPTH, DEC_BATCH, FFN_CONV - 1, D_FF))
    d['w_ada'] = nrm((DEPTH, D_MODEL, 6 * D_MODEL), 0.5 * D_MODEL ** -0.5)
    d['b_ada'] = nrm((DEPTH, 6 * D_MODEL), 0.02)
    d['g_norm1'] = gain((DEPTH, D_MODEL))
    d['g_norm2'] = gain((DEPTH, D_MODEL))
    d['w_in'] = nrm((DEPTH, D_MODEL, N_IN), D_MODEL ** -0.5)
    d['b_in'] = nrm((DEPTH, N_IN), 0.02).at[:, MLSTM_F_OFFSET:MLSTM_F_OFFSET + MLSTM_HEADS].add(MLSTM_F_BIAS)
    d['w_mconv'] = nrm((DEPTH, MLSTM_CONV, 2 * MLSTM_WIDTH), MLSTM_CONV ** -0.5)
    d['b_mconv'] = nrm((DEPTH, 2 * MLSTM_WIDTH), 0.02)
    d['g_mlstm'] = gain((DEPTH, MLSTM_WIDTH))
    d['w_gla_a2'] = nrm((DEPTH, GLA_RANK, GLA_K_WIDTH), GLA_RANK ** -0.5)
    d['b_gla_a2'] = nrm((DEPTH, GLA_K_WIDTH), 0.02)
    d['g_gla'] = gain((DEPTH, GLA_V_WIDTH))
    d['w_out'] = nrm((DEPTH, MIX_WIDTH, D_MODEL), MIX_WIDTH ** -0.5)
    d['w_ff_in'] = nrm((DEPTH, D_MODEL, 2 * D_FF), D_MODEL ** -0.5)
    d['w_fconv'] = nrm((DEPTH, FFN_CONV, D_FF), FFN_CONV ** -0.5)
    d['b_fconv'] = nrm((DEPTH, D_FF), 0.02)
    d['w_ff_out'] = nrm((DEPTH, D_FF, D_MODEL), D_FF ** -0.5)
    d['g_final'] = gain((D_MODEL,))
    return d


def reference(x_prompt, x_sample, c_prompt, c_sample, cache_win0_kv, cache_win1_kv, cache_win2_kv,
              state_mlstm_C, state_mlstm_n, state_mlstm_m, state_mlstm_conv, state_gla_S, state_ffn_conv,
              w_ada, b_ada, g_norm1, g_norm2, w_in, b_in, w_mconv, b_mconv, g_mlstm, w_gla_a2, b_gla_a2,
              g_gla, w_out, w_ff_in, w_fconv, b_fconv, w_ff_out, g_final):
    pos_p = jnp.arange(x_prompt.shape[1])
    pos_s = PAST_LEN + jnp.arange(x_sample.shape[1])
    names = ('win0', 'win1', 'win2', 'C', 'n', 'm', 'mconv', 'S', 'fconv')
    col_p = {k: [] for k in names}
    col_s = {k: [] for k in names}
    xp, xs = x_prompt, x_sample
    for l in range(DEPTH):
        p = {'w_ada': w_ada[l], 'b_ada': b_ada[l], 'g_norm1': g_norm1[l], 'g_norm2': g_norm2[l],
             'w_in': w_in[l], 'b_in': b_in[l], 'w_mconv': w_mconv[l], 'b_mconv': b_mconv[l],
             'g_mlstm': g_mlstm[l], 'w_gla_a2': w_gla_a2[l], 'b_gla_a2': b_gla_a2[l], 'g_gla': g_gla[l],
             'w_out': w_out[l], 'w_ff_in': w_ff_in[l], 'w_fconv': w_fconv[l], 'b_fconv': b_fconv[l],
             'w_ff_out': w_ff_out[l]}
        st_p = zero_state(x_prompt.shape[0], x_prompt.dtype)
        st_s = {'win0': cache_win0_kv[l], 'win1': cache_win1_kv[l], 'win2': cache_win2_kv[l],
                'C': state_mlstm_C[l], 'n': state_mlstm_n[l], 'm': state_mlstm_m[l],
                'mconv': state_mlstm_conv[l], 'S': state_gla_S[l], 'fconv': state_ffn_conv[l]}
        xp, new_p = trunk_layer(xp, c_prompt, pos_p, p, st_p, True)
        xs, new_s = trunk_layer(xs, c_sample, pos_s, p, st_s, False)
        for k in names:
            col_p[k].append(new_p[k])
            col_s[k].append(new_s[k])
    y_prompt = rms_norm(xp, g_final)
    y_sample = rms_norm(xs, g_final)
    sp = {k: jnp.stack(v, axis=0) for k, v in col_p.items()}
    ss = {k: jnp.stack(v, axis=0) for k, v in col_s.items()}
    return (y_prompt, y_sample, sp['win0'], ss['win0'], sp['win1'], ss['win1'], sp['win2'], ss['win2'],
            sp['C'], ss['C'], sp['n'], ss['n'], sp['m'], ss['m'], sp['mconv'], ss['mconv'],
            sp['S'], ss['S'], sp['fconv'], ss['fconv'])
```

```python
import functools
import math

import jax
import jax.numpy as jnp
from jax import lax
from jax.experimental import pallas as pl
from jax.experimental.pallas import tpu as pltpu

F32 = jnp.float32
BF16 = jnp.bfloat16
HI = lax.Precision.HIGHEST

HEAD_DIM = 64
ATTN_GROUPS = ((128, 1), (512, 4), (2048, 16))
ATTN_HPG = 4
ATTN_WIDTH = 3 * ATTN_HPG * HEAD_DIM
GROUP_W = ATTN_HPG * HEAD_DIM
ATTN_SCALE = HEAD_DIM ** -0.5
ROPE_THETA = 10000.0
BAND = 128
MLSTM_HEADS = 10
MLSTM_WIDTH = MLSTM_HEADS * HEAD_DIM
MLSTM_CONV = 4
MLSTM_CHUNK = 128
GLA_HEADS = 10
GLA_DK = 32
GLA_DV = 64
GLA_K_WIDTH = GLA_HEADS * GLA_DK
GLA_V_WIDTH = GLA_HEADS * GLA_DV
GLA_RANK = 16
GLA_TAU = 16.0
GLA_CHUNK = 128
GLA_SAFE_DECAY = 60.0
FFN_CONV = 3
EPS = 1e-6
PAST_LEN = 8192
NEG = -1e30

LANE = 128
VMEM_LIMIT = 56 * 1024 * 1024

IN_SPLITS = (ATTN_WIDTH, ATTN_WIDTH, ATTN_WIDTH, 2 * MLSTM_WIDTH, MLSTM_WIDTH, MLSTM_HEADS, MLSTM_HEADS,
             MLSTM_WIDTH, GLA_K_WIDTH, GLA_K_WIDTH, GLA_V_WIDTH, GLA_V_WIDTH, GLA_RANK)

C_AQ, C_AK, C_AV = 0, 768, 1536
C_MI, C_MF = 2304, 2432
C_MQK = 2560
C_MV, C_MO = 3840, 4480
C_GQK, C_GV, C_GG = 5120, 5760, 6400
C_GA = 7040
N_PACK = 7168
ROPE_COLS = 2 * ATTN_WIDTH


def _cparams(sem):
    return pltpu.CompilerParams(dimension_semantics=sem, vmem_limit_bytes=VMEM_LIMIT)


def _sigmoid(x):
    return 1.0 / (1.0 + jnp.exp(-x))


def _log_sigmoid(x):
    return jnp.minimum(x, 0.0) - jnp.log(1.0 + jnp.exp(-jnp.abs(x)))


def _dot(a, b):
    return jnp.dot(a, b, preferred_element_type=F32)


def _dot_nt(a, b):
    return lax.dot_general(a, b, (((1,), (1,)), ((), ())), preferred_element_type=F32)


def _ada_kernel(c_ref, w_ref, b_ref, o_ref):
    c = c_ref[...]
    s = (c * _sigmoid(c)).astype(BF16)
    o_ref[0] = _dot(s, w_ref[0].astype(BF16)) + b_ref[0]


def _ada_call(c_all, w_ada, b_ada):
    depth, d, n = w_ada.shape
    rows = c_all.shape[0]
    tn = 1024
    return pl.pallas_call(
        _ada_kernel,
        grid=(depth, n // tn),
        in_specs=[pl.BlockSpec((rows, d), lambda l, j: (0, 0)),
                  pl.BlockSpec((1, d, tn), lambda l, j: (l, 0, j)),
                  pl.BlockSpec((1, 1, tn), lambda l, j: (l, 0, j))],
        out_specs=pl.BlockSpec((1, rows, tn), lambda l, j: (l, 0, j)),
        out_shape=jax.ShapeDtypeStruct((depth, rows, n), F32),
        compiler_params=_cparams(("arbitrary", "arbitrary")),
        name="ada",
    )(c_all, w_ada, b_ada.reshape(depth, 1, n))


def _normmod_kernel(x_ref, g_ref, sc_ref, sh_ref, o_ref):
    x = x_ref[0]
    y = x * lax.rsqrt(jnp.mean(x * x, axis=-1, keepdims=True) + EPS) * g_ref[...]
    o_ref[0] = (y * (1.0 + sc_ref[0]) + sh_ref[0]).astype(o_ref.dtype)


def _normmod_call(x, g, sc, sh, out_dtype, tm):
    G, R, D = x.shape
    rr = sc.shape[1]
    mod_spec = (pl.BlockSpec((1, 1, D), lambda b, i: (b, 0, 0)) if rr == 1
                else pl.BlockSpec((1, tm, D), lambda b, i: (b, i, 0)))
    return pl.pallas_call(
        _normmod_kernel,
        grid=(G, R // tm),
        in_specs=[pl.BlockSpec((1, tm, D), lambda b, i: (b, i, 0)),
                  pl.BlockSpec((1, D), lambda b, i: (0, 0)),
                  mod_spec, mod_spec],
        out_specs=pl.BlockSpec((1, tm, D), lambda b, i: (b, i, 0)),
        out_shape=jax.ShapeDtypeStruct((G, R, D), out_dtype),
        compiler_params=_cparams(("arbitrary", "arbitrary")),
        name="normmod",
    )(x, g.reshape(1, D), sc, sh)


def _rope_chunk(x, cos, sin_a, sin_b):
    return x * cos + pltpu.roll(x, 96, 1) * sin_a + pltpu.roll(x, 32, 1) * sin_b


def _inproj_kernel(h_ref, w_ref, b_ref, cos_ref, sa_ref, sb_ref, o_ref, *, tn):
    j = pl.program_id(0)
    o_ref[...] = _dot(h_ref[...], w_ref[...]) + b_ref[...]
    n_chunks = tn // LANE
    rope_tiles = -(-ROPE_COLS // tn)
    for jt in range(rope_tiles):
        chunks = min(n_chunks, (ROPE_COLS - jt * tn) // LANE)

        @pl.when(j == jt)
        def _():
            cos, sa, sb = cos_ref[...], sa_ref[...], sb_ref[...]
            for c in range(chunks):
                sl = slice(c * LANE, (c + 1) * LANE)
                o_ref[:, sl] = _rope_chunk(o_ref[:, sl], cos, sa, sb)


def _inproj_call(h, w, b, cos, sin_a, sin_b, tm, tn):
    M, D = h.shape
    N = w.shape[1]
    tp = cos.shape[0] // tm
    tab = pl.BlockSpec((tm, LANE), lambda j, i: (i % tp, 0))
    return pl.pallas_call(
        functools.partial(_inproj_kernel, tn=tn),
        grid=(N // tn, M // tm),
        in_specs=[pl.BlockSpec((tm, D), lambda j, i: (i, 0)),
                  pl.BlockSpec((D, tn), lambda j, i: (0, j)),
                  pl.BlockSpec((1, tn), lambda j, i: (0, j)),
                  tab, tab, tab],
        out_specs=pl.BlockSpec((tm, tn), lambda j, i: (i, j)),
        out_shape=jax.ShapeDtypeStruct((M, N), F32),
        compiler_params=_cparams(("arbitrary", "arbitrary")),
        name="inproj",
    )(h, w, b, cos, sin_a, sin_b)


def _attn_prompt_kernel(q_ref, kp_ref, kc_ref, vp_ref, vc_ref, o_ref, l_ref):
    i = pl.program_id(2)
    row = lax.broadcasted_iota(jnp.int32, (BAND, 2 * BAND), 0)
    col = lax.broadcasted_iota(jnp.int32, (BAND, 2 * BAND), 1)
    valid = (col >= row) & (col <= row + BAND) & ((col >= BAND) | (i > 0))
    for h in range(ATTN_HPG):
        hs = slice(h * HEAD_DIM, (h + 1) * HEAD_DIM)
        q = q_ref[0, :, hs].astype(BF16)
        k = jnp.concatenate([kp_ref[0, :, hs], kc_ref[0, :, hs]], axis=0).astype(BF16)
        v = jnp.concatenate([vp_ref[0, :, hs], vc_ref[0, :, hs]], axis=0).astype(BF16)
        s = jnp.where(valid, _dot_nt(q, k) * ATTN_SCALE, NEG)
        mx = jnp.max(s, axis=-1, keepdims=True)
        e = jnp.exp(s - mx)
        den = jnp.sum(e, axis=-1, keepdims=True)
        o_ref[0, :, hs] = _dot(e.astype(BF16), v) / den
        l_ref[0, :, hs] = jnp.broadcast_to(mx + jnp.log(den), (BAND, HEAD_DIM))


def _attn_prompt_call(proj, g, dil):
    B, S, N = proj.shape
    L = S // dil
    nb = L // BAND
    pv = proj.reshape(B, L, dil * N)
    nblk = N // GROUP_W
    cq, ck, cv = C_AQ // GROUP_W + g, C_AK // GROUP_W + g, C_AV // GROUP_W + g

    def spec(cblk, prev):
        if prev:
            return pl.BlockSpec((1, BAND, GROUP_W), lambda b, r, i: (b, jnp.maximum(i - 1, 0), r * nblk + cblk))
        return pl.BlockSpec((1, BAND, GROUP_W), lambda b, r, i: (b, i, r * nblk + cblk))

    ospec = pl.BlockSpec((1, BAND, GROUP_W), lambda b, r, i: (b, i, r))
    o, lse = pl.pallas_call(
        _attn_prompt_kernel,
        grid=(B, dil, nb),
        in_specs=[spec(cq, False), spec(ck, True), spec(ck, False), spec(cv, True), spec(cv, False)],
        out_specs=[ospec, ospec],
        out_shape=[jax.ShapeDtypeStruct((B, L, dil * GROUP_W), F32)] * 2,
        compiler_params=_cparams(("arbitrary", "arbitrary", "arbitrary")),
        name=f"attn_prompt_g{g}",
    )(pv, pv, pv, pv, pv)
    return o.reshape(B, S, GROUP_W), lse.reshape(B, S, GROUP_W)


def _attn_mix_kernel(o0, o1, o2, l0, l1, l2, y_ref):
    a0, a1, a2 = l0[...], l1[...], l2[...]
    mx = jnp.maximum(jnp.maximum(a0, a1), a2)
    e0, e1, e2 = jnp.exp(a0 - mx), jnp.exp(a1 - mx), jnp.exp(a2 - mx)
    inv = 1.0 / (e0 + e1 + e2)
    y_ref[:, 0 * GROUP_W:1 * GROUP_W] = (o0[...] * (e0 * inv)).astype(y_ref.dtype)
    y_ref[:, 1 * GROUP_W:2 * GROUP_W] = (o1[...] * (e1 * inv)).astype(y_ref.dtype)
    y_ref[:, 2 * GROUP_W:3 * GROUP_W] = (o2[...] * (e2 * inv)).astype(y_ref.dtype)


def _attn_mix_call(os_, ls_, tm):
    M = os_[0].shape[0]
    spec = pl.BlockSpec((tm, GROUP_W), lambda i: (i, 0))
    return pl.pallas_call(
        _attn_mix_kernel,
        grid=(M // tm,),
        in_specs=[spec] * 6,
        out_specs=pl.BlockSpec((tm, ATTN_WIDTH), lambda i: (i, 0)),
        out_shape=jax.ShapeDtypeStruct((M, ATTN_WIDTH), BF16),
        compiler_params=_cparams(("arbitrary",)),
        name="attn_mix",
    )(*os_, *ls_)


def _attn_sample_kernel(qkv_ref, c0_ref, c1_ref, c2_ref, y_ref, q8, kall, vall, *, T):
    q8[...] = jnp.zeros_like(q8)
    q8[0:T, :] = qkv_ref[0, :, 0:ATTN_WIDTH]
    rows = ATTN_HPG * 8
    outs, lses = [], []
    for g, cref in enumerate((c0_ref, c1_ref, c2_ref)):
        kall[...] = jnp.zeros_like(kall)
        vall[...] = jnp.zeros_like(vall)
        knew = qkv_ref[0, :, C_AK + g * GROUP_W:C_AK + (g + 1) * GROUP_W]
        vnew = qkv_ref[0, :, C_AV + g * GROUP_W:C_AV + (g + 1) * GROUP_W]
        nres = 1 if g == 0 else T
        for r in range(nres):
            kall[r * BAND:(r + 1) * BAND, :] = cref[0, :, r * 2 * GROUP_W:r * 2 * GROUP_W + GROUP_W]
            vall[r * BAND:(r + 1) * BAND, :] = cref[0, :, r * 2 * GROUP_W + GROUP_W:(r + 1) * 2 * GROUP_W]
        nk = nres * BAND
        kall[nk:nk + T, :] = knew
        vall[nk:nk + T, :] = vnew
        nkp = nk + 8
        qg = q8[:, g * GROUP_W:(g + 1) * GROUP_W]
        qrep = jnp.concatenate([qg] * ATTN_HPG, axis=0)
        rr = lax.broadcasted_iota(jnp.int32, (rows, GROUP_W), 0)
        cc = lax.broadcasted_iota(jnp.int32, (rows, GROUP_W), 1)
        headsel = (rr // 8) == (cc // HEAD_DIM)
        qbd = jnp.where(headsel, qrep, 0.0).astype(BF16)
        s = _dot_nt(qbd, kall[0:nkp, :].astype(BF16)) * ATTN_SCALE
        t = lax.broadcasted_iota(jnp.int32, (rows, nkp), 0) % 8
        j = lax.broadcasted_iota(jnp.int32, (rows, nkp), 1)
        if g == 0:
            valid = (j >= t) & (j <= t + BAND) & (j < nk + T)
        else:
            valid = ((j >= t * BAND) & (j < (t + 1) * BAND)) | (j == nk + t)
        s = jnp.where(valid, s, NEG)
        mx = jnp.max(s, axis=-1, keepdims=True)
        e = jnp.exp(s - mx)
        den = jnp.sum(e, axis=-1, keepdims=True)
        pv = _dot(e.astype(BF16), vall[0:nkp, :].astype(BF16)) / den
        lse = jnp.broadcast_to(mx + jnp.log(den), (rows, GROUP_W))
        pv = jnp.where(headsel, pv, 0.0)
        lse = jnp.where(headsel, lse, 0.0)
        o8 = pv[0:8]
        l8 = lse[0:8]
        for h in range(1, ATTN_HPG):
            o8 = o8 + pv[8 * h:8 * h + 8]
            l8 = l8 + lse[8 * h:8 * h + 8]
        outs.append(o8)
        lses.append(l8)
    mx = jnp.maximum(jnp.maximum(lses[0], lses[1]), lses[2])
    es = [jnp.exp(l - mx) for l in lses]
    inv = 1.0 / (es[0] + es[1] + es[2])
    for g in range(3):
        y_ref[0, :, g * GROUP_W:(g + 1) * GROUP_W] = (outs[g] * (es[g] * inv))[0:T].astype(y_ref.dtype)


def _attn_sample_call(qkv, caches):
    B, T, _ = qkv.shape
    views, specs = [], []
    for g, (win, dil) in enumerate(ATTN_GROUPS):
        c = caches[g]
        assert c.shape[1] == win and T <= dil or g == 0, "sample attention assumes a full window cache"
        assert c.shape[1] // dil == BAND
        if g == 0:
            views.append(c.reshape(B, BAND, 2 * GROUP_W))
            specs.append(pl.BlockSpec((1, BAND, 2 * GROUP_W), lambda b: (b, 0, 0)))
        else:
            views.append(c.reshape(B, BAND, dil * 2 * GROUP_W))
            specs.append(pl.BlockSpec((1, BAND, T * 2 * GROUP_W), lambda b: (b, 0, 0)))
    return pl.pallas_call(
        functools.partial(_attn_sample_kernel, T=T),
        grid=(B,),
        in_specs=[pl.BlockSpec((1, T, 3 * ATTN_WIDTH), lambda b: (b, 0, 0))] + specs,
        out_specs=pl.BlockSpec((1, T, ATTN_WIDTH), lambda b: (b, 0, 0)),
        out_shape=jax.ShapeDtypeStruct((B, T, ATTN_WIDTH), BF16),
        scratch_shapes=[pltpu.VMEM((8, ATTN_WIDTH), F32),
                        pltpu.VMEM((T * BAND + 8, GROUP_W), F32),
                        pltpu.VMEM((T * BAND + 8, GROUP_W), F32)],
        compiler_params=_cparams(("arbitrary",)),
        name="attn_sample",
    )(qkv, *views)


def _mlstm_kernel(mqk_ref, mv_ref, mi_ref, mf_ref, mo_ref, cst_ref, wc_ref, bc_ref, c0_ref, n0_ref, m0_ref,
                  g_ref, tri_ref, y_ref, c_ref, n_ref, m_ref, cso_ref,
                  xbuf, vbuf, cs, ns, ms, tpad, kwp, *, T, Lp):
    c = pl.program_id(1)
    last = pl.num_programs(1) - 1
    W = MLSTM_WIDTH

    @pl.when(c == 0)
    def _():
        xbuf[...] = jnp.zeros_like(xbuf)
        vbuf[...] = jnp.zeros_like(vbuf)
        tpad[...] = jnp.zeros_like(tpad)
        kwp[...] = jnp.zeros_like(kwp)
        xbuf[5:8, :] = cst_ref[0]
        cs[...] = c0_ref[0]
        ns[...] = n0_ref[0]
        ms[...] = m0_ref[0]

    xbuf[8:8 + T, :] = mqk_ref[0]
    vbuf[0:T, :] = mv_ref[0]
    w = wc_ref[...]
    y = (bc_ref[...] + xbuf[8:8 + Lp, :] * w[3:4] + xbuf[7:7 + Lp, :] * w[2:3]
         + xbuf[6:6 + Lp, :] * w[1:2] + xbuf[5:5 + Lp, :] * w[0:1])
    tail = xbuf[8 + T - 3:8 + T, :]
    xbuf[5:8, :] = tail

    @pl.when(c == last)
    def _():
        cso_ref[0] = tail

    qk = y * _sigmoid(y)

    rowid = lax.broadcasted_iota(jnp.int32, (Lp, LANE), 0)
    real = rowid < T
    tpad[0:T, :] = mi_ref[0]
    ig = jnp.where(real, tpad[0:Lp, :], NEG)
    tpad[0:T, :] = mf_ref[0]
    lf = jnp.where(real, _log_sigmoid(tpad[0:Lp, :]), 0.0)
    tri = tri_ref[...]
    F = jnp.dot(tri, lf, precision=HI, preferred_element_type=F32)
    inter = F + ms[...]
    tpad[0:Lp, :] = F
    FT = tpad[...].T
    tpad[0:Lp, :] = ig
    IT = tpad[...].T
    causal = tri > 0.5
    v_all = vbuf[0:Lp, :]
    gls = []

    for h in range(MLSTM_HEADS):
        hs = slice(h * HEAD_DIM, (h + 1) * HEAD_DIM)
        q = qk[:, h * HEAD_DIM:(h + 1) * HEAD_DIM]
        k = qk[:, W + h * HEAD_DIM:W + (h + 1) * HEAD_DIM] * (HEAD_DIM ** -0.5)
        v = v_all[:, hs]
        qb, kb, vb = q.astype(BF16), k.astype(BF16), v.astype(BF16)
        fcol = F[:, h:h + 1]
        icol = ig[:, h:h + 1]
        frow = FT[h:h + 1, 0:Lp]
        irow = IT[h:h + 1, 0:Lp]
        logw = jnp.where(causal, fcol - frow + irow, NEG)
        inter_h = inter[:, h:h + 1]
        mt = jnp.maximum(jnp.max(logw, axis=-1, keepdims=True), inter_h)
        A = jnp.exp(logw - mt) * _dot_nt(qb, kb)
        gq = jnp.exp(inter_h - mt)
        C = cs[h]
        nrow = ns[h:h + 1, :]
        num = _dot(A.astype(BF16), vb) + gq * _dot(qb, C.astype(BF16))
        den = jnp.sum(A, axis=-1, keepdims=True) + gq * jnp.sum(q * nrow, axis=-1, keepdims=True)
        hh = num / jnp.maximum(jnp.abs(den), jnp.exp(-mt))
        m_old = ms[0:1, h:h + 1]
        mL = mt[Lp - 1:Lp, :]
        f_last = F[Lp - 1:Lp, h:h + 1]
        wL = jnp.exp(f_last - fcol + icol - mL)
        gL = jnp.exp(f_last + m_old - mL)
        kw = k * wL
        kwp[0:Lp, hs] = kw
        gls.append(gL)
        ns[h:h + 1, :] = gL * nrow + jnp.sum(kw, axis=0, keepdims=True)
        ms[0:1, h:h + 1] = mL
        hn = hh * lax.rsqrt(jnp.mean(hh * hh, axis=-1, keepdims=True) + EPS) * g_ref[0:1, hs]
        yo = hn * _sigmoid(mo_ref[0, :, hs]) if Lp == T else hn[0:T] * _sigmoid(mo_ref[0, :, hs])
        y_ref[0, :, hs] = yo.astype(y_ref.dtype)

    kwt = kwp[...].T
    for h in range(MLSTM_HEADS):
        hs = slice(h * HEAD_DIM, (h + 1) * HEAD_DIM)
        upd = _dot(kwt[h * HEAD_DIM:(h + 1) * HEAD_DIM, 0:Lp].astype(BF16), v_all[:, hs].astype(BF16))
        cs[h] = gls[h] * cs[h] + upd

    @pl.when(c == last)
    def _():
        c_ref[0] = cs[...]
        n_ref[0] = ns[...]
        m_ref[0] = ms[...]


def _mlstm_call(proj, conv_state, w_conv, b_conv, c0, n0, m0, g_mlstm, chunk):
    B, S, N = proj.shape
    T = chunk
    nc = S // T
    Lp = max(16, T)
    H = MLSTM_HEADS
    m0p = jnp.pad(m0, ((0, 0), (0, LANE - H))).reshape(B, 1, LANE)
    tri = jnp.tril(jnp.ones((Lp, Lp), F32))

    def col(width, off):
        blk = off // width
        return pl.BlockSpec((1, T, width), lambda b, c: (b, c, blk))

    def const(shape):
        nd = len(shape)
        return pl.BlockSpec(shape, lambda b, c: (0,) * nd)

    def per_b(shape):
        nd = len(shape)
        return pl.BlockSpec((1,) + shape, lambda b, c: (b,) + (0,) * nd)

    outs = pl.pallas_call(
        functools.partial(_mlstm_kernel, T=T, Lp=Lp),
        grid=(B, nc),
        in_specs=[col(2 * MLSTM_WIDTH, C_MQK), col(MLSTM_WIDTH, C_MV), col(LANE, C_MI), col(LANE, C_MF),
                  col(MLSTM_WIDTH, C_MO), per_b((MLSTM_CONV - 1, 2 * MLSTM_WIDTH)),
                  const((MLSTM_CONV, 2 * MLSTM_WIDTH)), const((1, 2 * MLSTM_WIDTH)),
                  per_b((H, HEAD_DIM, HEAD_DIM)), per_b((H, HEAD_DIM)), per_b((1, LANE)),
                  const((1, MLSTM_WIDTH)), const((Lp, Lp))],
        out_specs=[pl.BlockSpec((1, T, MLSTM_WIDTH), lambda b, c: (b, c, 0)),
                   per_b((H, HEAD_DIM, HEAD_DIM)), per_b((H, HEAD_DIM)), per_b((1, LANE)),
                   per_b((MLSTM_CONV - 1, 2 * MLSTM_WIDTH))],
        out_shape=[jax.ShapeDtypeStruct((B, S, MLSTM_WIDTH), BF16),
                   jax.ShapeDtypeStruct((B, H, HEAD_DIM, HEAD_DIM), F32),
                   jax.ShapeDtypeStruct((B, H, HEAD_DIM), F32),
                   jax.ShapeDtypeStruct((B, 1, LANE), F32),
                   jax.ShapeDtypeStruct((B, MLSTM_CONV - 1, 2 * MLSTM_WIDTH), F32)],
        scratch_shapes=[pltpu.VMEM((8 + Lp, 2 * MLSTM_WIDTH), F32),
                        pltpu.VMEM((Lp, MLSTM_WIDTH), F32),
                        pltpu.VMEM((H, HEAD_DIM, HEAD_DIM), F32),
                        pltpu.VMEM((H, HEAD_DIM), F32),
                        pltpu.VMEM((1, LANE), F32),
                        pltpu.VMEM((LANE, LANE), F32),
                        pltpu.VMEM((LANE, MLSTM_WIDTH), F32)],
        compiler_params=_cparams(("arbitrary", "arbitrary")),
        name="mlstm",
    )(proj, proj, proj, proj, proj, conv_state, w_conv, b_conv.reshape(1, -1), c0, n0, m0p,
      g_mlstm.reshape(1, -1), tri)
    y, C, n, m, cso = outs
    return y, C, n, m[:, 0, :H], cso


def _gla_kernel(gqk_ref, gv_ref, gg_ref, ga_ref, wa_ref, wat_ref, s0_ref, g_ref, tri_ref, bd_ref, ee_ref,
                y_ref, s_ref, qkp, vp, gap, sbd, osc, *, T, Lp):
    c = pl.program_id(1)
    last = pl.num_programs(1) - 1
    KW, VW = GLA_K_WIDTH, GLA_V_WIDTH

    @pl.when(c == 0)
    def _():
        qkp[...] = jnp.zeros_like(qkp)
        vp[...] = jnp.zeros_like(vp)
        gap[...] = jnp.zeros_like(gap)
        sbd[...] = jnp.zeros_like(sbd)
        osc[...] = jnp.zeros_like(osc)
        for h in range(GLA_HEADS):
            sbd[h * GLA_DK:(h + 1) * GLA_DK, h * GLA_DV:(h + 1) * GLA_DV] = s0_ref[0, h]

    qkp[0:T, :] = gqk_ref[0]
    vp[0:T, :] = gv_ref[0]
    lane = lax.broadcasted_iota(jnp.int32, (T, LANE), 1)
    gap[0:T, :] = jnp.where(lane == GLA_RANK, 1.0, ga_ref[0])

    tri = tri_ref[...]
    ga = gap[0:Lp, :]
    rowid = lax.broadcasted_iota(jnp.int32, (Lp, KW), 0)
    la = jnp.dot(ga, wa_ref[...], precision=HI, preferred_element_type=F32)
    la = jnp.where(rowid < T, _log_sigmoid(la) / GLA_TAU, 0.0)
    bc = jnp.dot(tri[0:Lp, 0:Lp], la, precision=HI, preferred_element_type=F32)
    colid = lax.broadcasted_iota(jnp.int32, (KW, LANE), 1)
    lat = jnp.dot(wat_ref[...], gap[...].T, precision=HI, preferred_element_type=F32)
    lat = jnp.where(colid < T, _log_sigmoid(lat) / GLA_TAU, 0.0)
    bct = lax.dot_general(lat, tri, (((1,), (1,)), ((), ())), precision=HI,
                          preferred_element_type=F32)
    blcol = bct[:, LANE - 1:LANE]
    blrow = bc[Lp - 1:Lp, :]
    kt = qkp[...].T[KW:2 * KW, :]
    q = qkp[0:Lp, 0:KW] * (GLA_DK ** -0.5)
    k = qkp[0:Lp, KW:2 * KW]
    v = vp[0:Lp, :]
    vfull = vp[...]
    bd = bd_ref[...]
    safe = jnp.min(blrow) >= -GLA_SAFE_DECAY

    @pl.when(safe)
    def _():
        qb = (q * jnp.exp(bc)).astype(BF16)
        kb = (k * jnp.exp(-bc)).astype(BF16)
        osc[0:Lp, :] = _dot(qb, sbd[...].astype(BF16))
        causal = tri[0:Lp, 0:Lp] > 0.5
        vb = v.astype(BF16)
        for h in range(GLA_HEADS):
            ks = slice(h * GLA_DK, (h + 1) * GLA_DK)
            vs = slice(h * GLA_DV, (h + 1) * GLA_DV)
            A = jnp.where(causal, _dot_nt(qb[:, ks], kb[:, ks]), 0.0)
            osc[0:Lp, vs] = osc[0:Lp, vs] + _dot(A.astype(BF16), vb[:, vs])
        klt = (kt * jnp.exp(blcol - bct)).astype(BF16)
        sbd[...] = bd * (jnp.exp(blcol) * sbd[...] + _dot(klt, vfull.astype(BF16)))

    @pl.when(jnp.logical_not(safe))
    def _():
        srow = lax.broadcasted_iota(jnp.int32, (LANE, LANE), 0)

        def body(t, carry):
            sel = (srow == t).astype(F32)
            lac = jnp.dot(lat, sel, precision=HI, preferred_element_type=F32)
            kc = jnp.dot(kt, sel, precision=HI, preferred_element_type=F32)
            dec = jnp.concatenate([jnp.exp(lac)] * (VW // LANE), axis=1)
            kcw = jnp.concatenate([kc] * (VW // LANE), axis=1)
            vrow = vp[pl.ds(t, 1), :]
            snew = bd * (dec * sbd[...] + kcw * vrow)
            sbd[...] = snew
            qrow = jnp.broadcast_to(qkp[pl.ds(t, 1), 0:KW] * (GLA_DK ** -0.5), (8, KW))
            orow = jnp.dot(qrow, snew, precision=HI, preferred_element_type=F32)
            osc[pl.ds(t, 1), :] = orow[0:1]
            return carry

        lax.fori_loop(0, T, body, 0)

    o = osc[0:Lp, :]
    ms = jnp.dot(o * o, ee_ref[...], precision=HI, preferred_element_type=F32)
    og = o * lax.rsqrt(ms + EPS) * g_ref[...]
    gg = gg_ref[0]
    yo = (og if Lp == T else og[0:T]) * (gg * _sigmoid(gg))
    y_ref[0] = yo.astype(y_ref.dtype)

    @pl.when(c == last)
    def _():
        for h in range(GLA_HEADS):
            s_ref[0, h] = sbd[h * GLA_DK:(h + 1) * GLA_DK, h * GLA_DV:(h + 1) * GLA_DV]


def _gla_call(proj, w_a2, b_a2, s0, g_gla, chunk):
    B, S, N = proj.shape
    T = chunk
    nc = S // T
    Lp = max(16, T)
    H, KW, VW = GLA_HEADS, GLA_K_WIDTH, GLA_V_WIDTH
    wa = jnp.zeros((LANE, KW), F32).at[:GLA_RANK].set(w_a2).at[GLA_RANK].set(b_a2)
    tri = jnp.tril(jnp.ones((LANE, LANE), F32))
    hk = jnp.arange(KW) // GLA_DK
    hv = jnp.arange(VW) // GLA_DV
    bd = (hk[:, None] == hv[None, :]).astype(F32)
    ee = (hv[:, None] == hv[None, :]).astype(F32) / GLA_DV

    def col(width, off):
        blk = off // width
        return pl.BlockSpec((1, T, width), lambda b, c: (b, c, blk))

    def const(shape):
        nd = len(shape)
        return pl.BlockSpec(shape, lambda b, c: (0,) * nd)

    y, s = pl.pallas_call(
        functools.partial(_gla_kernel, T=T, Lp=Lp),
        grid=(B, nc),
        in_specs=[col(2 * KW, C_GQK), col(VW, C_GV), col(VW, C_GG), col(LANE, C_GA),
                  const((LANE, KW)), const((KW, LANE)),
                  pl.BlockSpec((1, H, GLA_DK, GLA_DV), lambda b, c: (b, 0, 0, 0)),
                  const((1, VW)), const((LANE, LANE)), const((KW, VW)), const((VW, VW))],
        out_specs=[pl.BlockSpec((1, T, VW), lambda b, c: (b, c, 0)),
                   pl.BlockSpec((1, H, GLA_DK, GLA_DV), lambda b, c: (b, 0, 0, 0))],
        out_shape=[jax.ShapeDtypeStruct((B, S, VW), BF16),
                   jax.ShapeDtypeStruct((B, H, GLA_DK, GLA_DV), F32)],
        scratch_shapes=[pltpu.VMEM((LANE, 2 * KW), F32),
                        pltpu.VMEM((LANE, VW), F32),
                        pltpu.VMEM((LANE, LANE), F32),
                        pltpu.VMEM((KW, VW), F32),
                        pltpu.VMEM((LANE, VW), F32)],
        compiler_params=_cparams(("arbitrary", "arbitrary")),
        name="gla",
    )(proj, proj, proj, proj, wa, wa.T, s0, g_gla.reshape(1, -1), tri, bd, ee)
    return y, s


def _outproj_kernel(ya_ref, ym_ref, yg_ref, w_ref, x_ref, gt_ref, g2_ref, sc_ref, sh_ref, xo_ref, h2_ref):
    a, b = ATTN_WIDTH, ATTN_WIDTH + MLSTM_WIDTH
    acc = _dot(ya_ref[0], w_ref[0:a, :])
    acc = acc + _dot(ym_ref[0], w_ref[a:b, :])
    acc = acc + _dot(yg_ref[0], w_ref[b:, :])
    x = x_ref[0] + gt_ref[0] * acc
    xo_ref[0] = x
    y = x * lax.rsqrt(jnp.mean(x * x, axis=-1, keepdims=True) + EPS) * g2_ref[...]
    h2_ref[0] = (y * (1.0 + sc_ref[0]) + sh_ref[0]).astype(h2_ref.dtype)


def _outproj_call(ya, ym, yg, w_out, x, gt, g2, sc, sh, tm):
    G, R, D = x.shape
    rr = gt.shape[1]
    mod_spec = (pl.BlockSpec((1, 1, D), lambda b, i: (b, 0, 0)) if rr == 1
                else pl.BlockSpec((1, tm, D), lambda b, i: (b, i, 0)))

    def act(width):
        return pl.BlockSpec((1, tm, width), lambda b, i: (b, i, 0))

    return pl.pallas_call(
        _outproj_kernel,
        grid=(G, R // tm),
        in_specs=[act(ATTN_WIDTH), act(MLSTM_WIDTH), act(GLA_V_WIDTH),
                  pl.BlockSpec(w_out.shape, lambda b, i: (0, 0)),
                  act(D), mod_spec, pl.BlockSpec((1, D), lambda b, i: (0, 0)), mod_spec, mod_spec],
        out_specs=[act(D), act(D)],
        out_shape=[jax.ShapeDtypeStruct((G, R, D), F32), jax.ShapeDtypeStruct((G, R, D), BF16)],
        compiler_params=_cparams(("arbitrary", "arbitrary")),
        name="outproj",
    )(ya, ym, yg, w_out, x, gt, g2.reshape(1, D), sc, sh)


def _ffn_kernel(h_ref, wg_ref, wu_ref, wc_ref, bc_ref, wo_ref, x_ref, gt_ref, init_ref,
                xo_ref, st_ref, ubuf, cbuf, *, tm, u, R, tiles_per_seq):
    m = pl.program_id(1)
    f = pl.program_id(2)
    nf = pl.num_programs(2)
    h = h_ref[0]
    ug = _dot(h, wg_ref[...])
    uu = _dot(h, wu_ref[...])

    @pl.when(m % tiles_per_seq == 0)
    def _():
        ubuf[0:R, :] = init_ref[0]

    @pl.when(m % tiles_per_seq != 0)
    def _():
        ubuf[0:R, :] = cbuf[f]

    ubuf[R:R + tm, :] = ug
    w = wc_ref[...]
    gate = bc_ref[...] + ug * w[2:3] + ubuf[R - u:R - u + tm, :] * w[1:2] + ubuf[R - 2 * u:R - 2 * u + tm, :] * w[0:1]
    tail = ubuf[tm:tm + R, :]
    cbuf[f] = tail
    st_ref[0, 0] = tail
    act = (gate * _sigmoid(gate) * uu).astype(BF16)
    part = _dot(act, wo_ref[...])

    @pl.when(f == 0)
    def _():
        xo_ref[0] = part

    @pl.when(f != 0)
    def _():
        xo_ref[0] = xo_ref[0] + part

    @pl.when(f == nf - 1)
    def _():
        xo_ref[0] = x_ref[0] + gt_ref[0] * xo_ref[0]


def _ffn_call(h2, w_g, w_u, w_conv, b_conv, w_o, x, gt, init, tm, tf, u):
    G, rows, D = x.shape
    F = w_g.shape[1]
    R = max(8, 2 * u)
    nm, nf = rows // tm, F // tf
    rr = gt.shape[1]
    mod_spec = (pl.BlockSpec((1, 1, D), lambda b, i, f: (b, 0, 0)) if rr == 1
                else pl.BlockSpec((1, tm, D), lambda b, i, f: (b, i, 0)))
    xo, st = pl.pallas_call(
        functools.partial(_ffn_kernel, tm=tm, u=u, R=R, tiles_per_seq=nm),
        grid=(G, nm, nf),
        in_specs=[pl.BlockSpec((1, tm, D), lambda b, i, f: (b, i, 0)),
                  pl.BlockSpec((D, tf), lambda b, i, f: (0, f)),
                  pl.BlockSpec((D, tf), lambda b, i, f: (0, f)),
                  pl.BlockSpec((FFN_CONV, tf), lambda b, i, f: (0, f)),
                  pl.BlockSpec((1, tf), lambda b, i, f: (0, f)),
                  pl.BlockSpec((tf, D), lambda b, i, f: (f, 0)),
                  pl.BlockSpec((1, tm, D), lambda b, i, f: (b, i, 0)),
                  mod_spec,
                  pl.BlockSpec((1, R, tf), lambda b, i, f: (b, 0, f))],
        out_specs=[pl.BlockSpec((1, tm, D), lambda b, i, f: (b, i, 0)),
                   pl.BlockSpec((1, 1, R, tf), lambda b, i, f: (b, i, 0, f))],
        out_shape=[jax.ShapeDtypeStruct((G, rows, D), F32),
                   jax.ShapeDtypeStruct((G, nm, R, F), F32)],
        scratch_shapes=[pltpu.VMEM((R + tm, tf), F32), pltpu.VMEM((nf, R, tf), F32)],
        compiler_params=_cparams(("arbitrary", "arbitrary", "arbitrary")),
        name="ffn",
    )(h2, w_g, w_u, w_conv, b_conv.reshape(1, F), w_o, x, gt, init)
    return xo, st[:, nm - 1]


def _pack_in_proj(w_in, b_in):
    def split(a):
        out, off = [], 0
        for s in IN_SPLITS:
            out.append(a[..., off:off + s])
            off += s
        return out

    def pad(a, n):
        return jnp.pad(a, [(0, 0)] * (a.ndim - 1) + [(0, n - a.shape[-1])])

    def pack(a):
        aq, ak, av, mqk, mv, mi, mf, mo, gq, gk, gv, gg, ga = split(a)
        return jnp.concatenate([aq, ak, av, pad(mi, LANE), pad(mf, LANE), mqk, mv, mo, gq, gk, gv, gg,
                                pad(ga, LANE)], axis=-1)

    return pack(w_in).astype(BF16), pack(b_in)


def _rope_tables(pos):
    half = HEAD_DIM // 2
    inv_freq = jnp.power(ROPE_THETA, -jnp.arange(half, dtype=F32) / half)
    ang = pos.astype(F32)[:, None] * inv_freq[None, :]
    cos, sin = jnp.cos(ang), jnp.sin(ang)
    zero = jnp.zeros_like(sin)
    reps = LANE // HEAD_DIM
    return (jnp.tile(jnp.concatenate([cos, cos], -1), (1, reps)),
            jnp.tile(jnp.concatenate([-sin, zero], -1), (1, reps)),
            jnp.tile(jnp.concatenate([zero, sin], -1), (1, reps)))


def _pick_tile(n, pref):
    t = math.gcd(n, pref)
    return t


def kernel(x_prompt, x_sample, c_prompt, c_sample, cache_win0_kv, cache_win1_kv, cache_win2_kv, state_mlstm_C, state_mlstm_n, state_mlstm_m, state_mlstm_conv, state_gla_S, state_ffn_conv, w_ada, b_ada, g_norm1, g_norm2, w_in, b_in, w_mconv, b_mconv, g_mlstm, w_gla_a2, b_gla_a2, g_gla, w_out, w_ff_in, w_fconv, b_fconv, w_ff_out, g_final):
    B, S, D = x_prompt.shape
    Bs, Ts, _ = x_sample.shape
    depth = w_ada.shape[0]
    d_ff = w_fconv.shape[-1]
    caches = (cache_win0_kv, cache_win1_kv, cache_win2_kv)
    Ms = Bs * Ts

    n_c = B + Bs
    rows_c = -(-n_c // 8) * 8
    c_all = jnp.pad(jnp.concatenate([c_prompt, c_sample], axis=0), ((0, rows_c - n_c), (0, 0)))
    mod = _ada_call(c_all, w_ada, b_ada).reshape(depth, rows_c, 6, D)

    w_in_p, b_in_p = _pack_in_proj(w_in, b_in)
    w_out_b = w_out.astype(BF16)
    w_g = w_ff_in[..., :d_ff].astype(BF16)
    w_u = w_ff_in[..., d_ff:].astype(BF16)
    w_o = w_ff_out.astype(BF16)

    rope_p = _rope_tables(jnp.arange(S))
    rope_s = _rope_tables(PAST_LEN + jnp.repeat(jnp.arange(Ts), Bs))

    tm_p = _pick_tile(S, 512)
    tn = 1024
    tf = _pick_tile(d_ff, 512)

    xp = x_prompt
    xs = x_sample.transpose(1, 0, 2).reshape(1, Ms, D)
    zeros_p = {
        'mconv': jnp.zeros((B, MLSTM_CONV - 1, 2 * MLSTM_WIDTH), F32),
        'C': jnp.zeros((B, MLSTM_HEADS, HEAD_DIM, HEAD_DIM), F32),
        'n': jnp.zeros((B, MLSTM_HEADS, HEAD_DIM), F32),
        'm': jnp.zeros((B, MLSTM_HEADS), F32),
        'S': jnp.zeros((B, GLA_HEADS, GLA_DK, GLA_DV), F32),
        'fconv': jnp.zeros((B, 8, d_ff), F32),
    }
    names = ('win0', 'win1', 'win2', 'C', 'n', 'm', 'mconv', 'S', 'fconv')
    col_p = {k: [] for k in names}
    col_s = {k: [] for k in names}

    for l in range(depth):
        mp = mod[l, :B]
        ms_ = jnp.tile(mod[l, B:B + Bs], (Ts, 1, 1))

        def mods_p(i):
            return mp[:, i:i + 1, :]

        def mods_s(i):
            return ms_[None, :, i, :]

        h = _normmod_call(xp, g_norm1[l], mods_p(1), mods_p(0), BF16, tm_p)
        proj = _inproj_call(h.reshape(B * S, D), w_in_p[l], b_in_p[l][None], *rope_p, tm_p, tn)
        proj = proj.reshape(B, S, N_PACK)
        os_, ls_ = [], []
        for g, (win, dil) in enumerate(ATTN_GROUPS):
            o, lse = _attn_prompt_call(proj, g, dil)
            os_.append(o.reshape(B * S, GROUP_W))
            ls_.append(lse.reshape(B * S, GROUP_W))
            keep = min(win, S)
            kk = proj[:, S - keep:, C_AK + g * GROUP_W:C_AK + (g + 1) * GROUP_W]
            vv = proj[:, S - keep:, C_AV + g * GROUP_W:C_AV + (g + 1) * GROUP_W]
            col_p[f'win{g}'].append(jnp.stack([kk, vv], axis=2).reshape(B, keep, 2, ATTN_HPG, HEAD_DIM))
        ya = _attn_mix_call(os_, ls_, tm_p).reshape(B, S, ATTN_WIDTH)
        ym, C, n, m, cso = _mlstm_call(proj, zeros_p['mconv'], w_mconv[l], b_mconv[l], zeros_p['C'],
                                        zeros_p['n'], zeros_p['m'], g_mlstm[l], math.gcd(S, MLSTM_CHUNK))
        yg, Sg = _gla_call(proj, w_gla_a2[l], b_gla_a2[l], zeros_p['S'], g_gla[l], math.gcd(S, GLA_CHUNK))
        xp, h2 = _outproj_call(ya, ym, yg, w_out_b[l], xp, mods_p(2), g_norm2[l], mods_p(4), mods_p(3), tm_p)
        xp, fst = _ffn_call(h2, w_g[l], w_u[l], w_fconv[l], b_fconv[l], w_o[l], xp, mods_p(5),
                            zeros_p['fconv'], tm_p, tf, 1)
        for k_, v_ in (('C', C), ('n', n), ('m', m), ('mconv', cso), ('S', Sg), ('fconv', fst[:, 6:8])):
            col_p[k_].append(v_)

        h = _normmod_call(xs, g_norm1[l], mods_s(1), mods_s(0), BF16, Ms)
        proj = _inproj_call(h.reshape(Ms, D), w_in_p[l], b_in_p[l][None], *rope_s, Ms, tn)
        proj_b = proj.reshape(Ts, Bs, N_PACK).transpose(1, 0, 2)
        ya = _attn_sample_call(proj_b[:, :, :3 * ATTN_WIDTH], [c[l] for c in caches])
        for g in range(3):
            kk = proj_b[:, :, C_AK + g * GROUP_W:C_AK + (g + 1) * GROUP_W]
            vv = proj_b[:, :, C_AV + g * GROUP_W:C_AV + (g + 1) * GROUP_W]
            new = jnp.stack([kk, vv], axis=2).reshape(Bs, Ts, 2, ATTN_HPG, HEAD_DIM)
            col_s[f'win{g}'].append(jnp.concatenate([caches[g][l][:, Ts:], new], axis=1))
        ym, C, n, m, cso = _mlstm_call(proj_b, state_mlstm_conv[l], w_mconv[l], b_mconv[l], state_mlstm_C[l],
                                        state_mlstm_n[l], state_mlstm_m[l], g_mlstm[l], Ts)
        yg, Sg = _gla_call(proj_b, w_gla_a2[l], b_gla_a2[l], state_gla_S[l], g_gla[l], Ts)

        def tmaj(a):
            return a.transpose(1, 0, 2).reshape(1, Ms, a.shape[-1])

        xs, h2 = _outproj_call(tmaj(ya), tmaj(ym), tmaj(yg), w_out_b[l], xs, mods_s(2), g_norm2[l],
                               mods_s(4), mods_s(3), Ms)
        n_st = (FFN_CONV - 1) * Bs
        r_st = max(8, n_st)
        init = state_ffn_conv[l].transpose(1, 0, 2).reshape(1, n_st, d_ff)
        init = jnp.pad(init, ((0, 0), (r_st - n_st, 0), (0, 0)))
        xs, fst = _ffn_call(h2, w_g[l], w_u[l], w_fconv[l], b_fconv[l], w_o[l], xs, mods_s(5), init, Ms, tf, Bs)
        fst = fst[:, r_st - n_st:].reshape(FFN_CONV - 1, Bs, d_ff).transpose(1, 0, 2)
        for k_, v_ in (('C', C), ('n', n), ('m', m), ('mconv', cso), ('S', Sg), ('fconv', fst)):
            col_s[k_].append(v_)

    y_prompt = _normmod_call(xp, g_final, jnp.zeros((B, 1, D), F32), jnp.zeros((B, 1, D), F32), F32, tm_p)
    y_sample = _normmod_call(xs, g_final, jnp.zeros((1, 1, D), F32), jnp.zeros((1, 1, D), F32), F32, Ms)
    y_sample = y_sample.reshape(Ts, Bs, D).transpose(1, 0, 2)
    sp = {k: jnp.stack(v, axis=0) for k, v in col_p.items()}
    ss = {k: jnp.stack(v, axis=0) for k, v in col_s.items()}
    return (y_prompt, y_sample, sp['win0'], ss['win0'], sp['win1'], ss['win1'], sp['win2'], ss['win2'],
            sp['C'], ss['C'], sp['n'], ss['n'], sp['m'], ss['m'], sp['mconv'], ss['mconv'],
            sp['S'], ss['S'], sp['fconv'], ss['fconv'])
```

```python
import functools
import math

import jax
import jax.numpy as jnp
from jax import lax
from jax.experimental import pallas as pl
from jax.experimental.pallas import tpu as pltpu

F32 = jnp.float32
BF16 = jnp.bfloat16
HI = lax.Precision.HIGHEST

HEAD_DIM = 64
ATTN_GROUPS = ((128, 1), (512, 4), (2048, 16))
ATTN_HPG = 4
ATTN_WIDTH = 3 * ATTN_HPG * HEAD_DIM
GROUP_W = ATTN_HPG * HEAD_DIM
ATTN_SCALE = HEAD_DIM ** -0.5
ROPE_THETA = 10000.0
BAND = 128
MLSTM_HEADS = 10
MLSTM_WIDTH = MLSTM_HEADS * HEAD_DIM
MLSTM_CONV = 4
MLSTM_CHUNK = 128
GLA_HEADS = 10
GLA_DK = 32
GLA_DV = 64
GLA_K_WIDTH = GLA_HEADS * GLA_DK
GLA_V_WIDTH = GLA_HEADS * GLA_DV
GLA_RANK = 16
GLA_TAU = 16.0
GLA_CHUNK = 128
GLA_SAFE_DECAY = 60.0
FFN_CONV = 3
EPS = 1e-6
PAST_LEN = 8192
NEG = -1e30

LANE = 128
VMEM_LIMIT = 56 * 1024 * 1024

IN_SPLITS = (ATTN_WIDTH, ATTN_WIDTH, ATTN_WIDTH, 2 * MLSTM_WIDTH, MLSTM_WIDTH, MLSTM_HEADS, MLSTM_HEADS,
             MLSTM_WIDTH, GLA_K_WIDTH, GLA_K_WIDTH, GLA_V_WIDTH, GLA_V_WIDTH, GLA_RANK)

C_AQ, C_AK, C_AV = 0, 768, 1536
C_MI, C_MF = 2304, 2432
C_MQK = 2560
C_MV, C_MO = 3840, 4480
C_GQK, C_GV, C_GG = 5120, 5760, 6400
C_GA = 7040
N_PACK = 7168
ROPE_COLS = 2 * ATTN_WIDTH


def _cparams(sem):
    return pltpu.CompilerParams(dimension_semantics=sem, vmem_limit_bytes=VMEM_LIMIT)


def _sigmoid(x):
    return 1.0 / (1.0 + jnp.exp(-x))


def _log_sigmoid(x):
    return jnp.minimum(x, 0.0) - jnp.log(1.0 + jnp.exp(-jnp.abs(x)))


def _dot(a, b):
    return jnp.dot(a, b, preferred_element_type=F32)


def _dot_nt(a, b):
    return lax.dot_general(a, b, (((1,), (1,)), ((), ())), preferred_element_type=F32)


def _ada_kernel(c_ref, w_ref, b_ref, o_ref):
    c = c_ref[...]
    s = (c * _sigmoid(c)).astype(BF16)
    o_ref[0] = _dot(s, w_ref[0].astype(BF16)) + b_ref[0]


def _ada_call(c_all, w_ada, b_ada):
    depth, d, n = w_ada.shape
    rows = c_all.shape[0]
    tn = 1024
    return pl.pallas_call(
        _ada_kernel,
        grid=(depth, n // tn),
        in_specs=[pl.BlockSpec((rows, d), lambda l, j: (0, 0)),
                  pl.BlockSpec((1, d, tn), lambda l, j: (l, 0, j)),
                  pl.BlockSpec((1, 1, tn), lambda l, j: (l, 0, j))],
        out_specs=pl.BlockSpec((1, rows, tn), lambda l, j: (l, 0, j)),
        out_shape=jax.ShapeDtypeStruct((depth, rows, n), F32),
        compiler_params=_cparams(("arbitrary", "arbitrary")),
        name="ada",
    )(c_all, w_ada, b_ada.reshape(depth, 1, n))


def _normmod_kernel(x_ref, g_ref, sc_ref, sh_ref, o_ref):
    x = x_ref[0]
    y = x * lax.rsqrt(jnp.mean(x * x, axis=-1, keepdims=True) + EPS) * g_ref[...]
    o_ref[0] = (y * (1.0 + sc_ref[0]) + sh_ref[0]).astype(o_ref.dtype)


def _normmod_call(x, g, sc, sh, out_dtype, tm):
    G, R, D = x.shape
    rr = sc.shape[1]
    mod_spec = (pl.BlockSpec((1, 1, D), lambda b, i: (b, 0, 0)) if rr == 1
                else pl.BlockSpec((1, tm, D), lambda b, i: (b, i, 0)))
    return pl.pallas_call(
        _normmod_kernel,
        grid=(G, R // tm),
        in_specs=[pl.BlockSpec((1, tm, D), lambda b, i: (b, i, 0)),
                  pl.BlockSpec((1, D), lambda b, i: (0, 0)),
                  mod_spec, mod_spec],
        out_specs=pl.BlockSpec((1, tm, D), lambda b, i: (b, i, 0)),
        out_shape=jax.ShapeDtypeStruct((G, R, D), out_dtype),
        compiler_params=_cparams(("arbitrary", "arbitrary")),
        name="normmod",
    )(x, g.reshape(1, D), sc, sh)


def _rope_chunk(x, cos, sin_a, sin_b):
    return x * cos + pltpu.roll(x, 96, 1) * sin_a + pltpu.roll(x, 32, 1) * sin_b


def _inproj_kernel(h_ref, w_ref, b_ref, cos_ref, sa_ref, sb_ref, o_ref, *, tn):
    j = pl.program_id(0)
    o_ref[...] = _dot(h_ref[...], w_ref[...]) + b_ref[...]
    n_chunks = tn // LANE
    rope_tiles = -(-ROPE_COLS // tn)
    for jt in range(rope_tiles):
        chunks = min(n_chunks, (ROPE_COLS - jt * tn) // LANE)

        @pl.when(j == jt)
        def _():
            cos, sa, sb = cos_ref[...], sa_ref[...], sb_ref[...]
            for c in range(chunks):
                sl = slice(c * LANE, (c + 1) * LANE)
                o_ref[:, sl] = _rope_chunk(o_ref[:, sl], cos, sa, sb)


def _inproj_call(h, w, b, cos, sin_a, sin_b, tm, tn):
    M, D = h.shape
    N = w.shape[1]
    tp = cos.shape[0] // tm
    tab = pl.BlockSpec((tm, LANE), lambda j, i: (i % tp, 0))
    return pl.pallas_call(
        functools.partial(_inproj_kernel, tn=tn),
        grid=(N // tn, M // tm),
        in_specs=[pl.BlockSpec((tm, D), lambda j, i: (i, 0)),
                  pl.BlockSpec((D, tn), lambda j, i: (0, j)),
                  pl.BlockSpec((1, tn), lambda j, i: (0, j)),
                  tab, tab, tab],
        out_specs=pl.BlockSpec((tm, tn), lambda j, i: (i, j)),
        out_shape=jax.ShapeDtypeStruct((M, N), F32),
        compiler_params=_cparams(("arbitrary", "arbitrary")),
        name="inproj",
    )(h, w, b, cos, sin_a, sin_b)


def _attn_prompt_kernel(q_ref, kp_ref, kc_ref, vp_ref, vc_ref, o_ref, l_ref, obuf, lbuf, *, dil):
    i = pl.program_id(2)
    row = lax.broadcasted_iota(jnp.int32, (BAND, 2 * BAND), 0)
    col = lax.broadcasted_iota(jnp.int32, (BAND, 2 * BAND), 1)
    valid = (col >= row) & (col <= row + BAND) & ((col >= BAND) | (i > 0))

    def residue(r):
        rows = slice(None) if dil == 1 else pl.ds(r, BAND, stride=dil)
        q_all = q_ref[0, rows, :]
        k_all = jnp.concatenate([kp_ref[0, rows, :], kc_ref[0, rows, :]], axis=0)
        v_all = jnp.concatenate([vp_ref[0, rows, :], vc_ref[0, rows, :]], axis=0)
        for h in range(LANE // HEAD_DIM):
            hs = slice(h * HEAD_DIM, (h + 1) * HEAD_DIM)
            s = jnp.where(valid, _dot_nt(q_all[:, hs].astype(BF16), k_all[:, hs].astype(BF16)) * ATTN_SCALE, NEG)
            mx = jnp.max(s, axis=-1, keepdims=True)
            e = jnp.exp(s - mx)
            den = jnp.sum(e, axis=-1, keepdims=True)
            obuf[:, hs] = _dot(e.astype(BF16), v_all[:, hs].astype(BF16)) / den
            lbuf[:, hs] = jnp.broadcast_to(mx + jnp.log(den), (BAND, HEAD_DIM))
        o_ref[0, rows, :] = obuf[...]
        l_ref[0, rows, :] = lbuf[...]

    if dil == 1:
        residue(0)
    else:
        def body(r, carry):
            residue(r)
            return carry

        lax.fori_loop(0, dil, body, 0)


def _attn_prompt_call(proj, g, dil):
    B, S, N = proj.shape
    tb = BAND * dil
    nb = S // tb
    hp = GROUP_W // LANE
    cq, ck, cv = (C_AQ + g * GROUP_W) // LANE, (C_AK + g * GROUP_W) // LANE, (C_AV + g * GROUP_W) // LANE

    def spec(cblk, prev):
        if prev:
            return pl.BlockSpec((1, tb, LANE), lambda b, p, i: (b, jnp.maximum(i - 1, 0), cblk + p))
        return pl.BlockSpec((1, tb, LANE), lambda b, p, i: (b, i, cblk + p))

    ospec = pl.BlockSpec((1, tb, LANE), lambda b, p, i: (b, i, p))
    o, lse = pl.pallas_call(
        functools.partial(_attn_prompt_kernel, dil=dil),
        grid=(B, hp, nb),
        in_specs=[spec(cq, False), spec(ck, True), spec(ck, False), spec(cv, True), spec(cv, False)],
        out_specs=[ospec, ospec],
        out_shape=[jax.ShapeDtypeStruct((B, S, GROUP_W), F32)] * 2,
        scratch_shapes=[pltpu.VMEM((BAND, LANE), F32), pltpu.VMEM((BAND, LANE), F32)],
        compiler_params=_cparams(("arbitrary", "arbitrary", "arbitrary")),
        name=f"attn_prompt_g{g}",
    )(proj, proj, proj, proj, proj)
    return o, lse


def _win_extract_kernel(x_ref, o_ref):
    o_ref[0, 0] = x_ref[0].T


def _win_extract_call(proj, g, keep):
    B, S, N = proj.shape
    chunk = min(keep, 512)
    first = (S - keep) // chunk
    cblk = C_AK // GROUP_W + g
    step = (C_AV - C_AK) // GROUP_W
    out = pl.pallas_call(
        _win_extract_kernel,
        grid=(B, 2, keep // chunk),
        in_specs=[pl.BlockSpec((1, chunk, GROUP_W), lambda b, kv, j: (b, first + j, cblk + step * kv))],
        out_specs=pl.BlockSpec((1, 1, GROUP_W, chunk), lambda b, kv, j: (b, kv, 0, j)),
        out_shape=jax.ShapeDtypeStruct((B, 2, GROUP_W, keep), F32),
        compiler_params=_cparams(("arbitrary", "arbitrary", "arbitrary")),
        name=f"win_extract_g{g}",
    )(proj)
    return out.reshape(B, 2, ATTN_HPG, HEAD_DIM, keep).transpose(0, 4, 1, 2, 3)


def _attn_mix_kernel(o0, o1, o2, l0, l1, l2, y_ref):
    a0, a1, a2 = l0[...], l1[...], l2[...]
    mx = jnp.maximum(jnp.maximum(a0, a1), a2)
    e0, e1, e2 = jnp.exp(a0 - mx), jnp.exp(a1 - mx), jnp.exp(a2 - mx)
    inv = 1.0 / (e0 + e1 + e2)
    y_ref[:, 0 * GROUP_W:1 * GROUP_W] = (o0[...] * (e0 * inv)).astype(y_ref.dtype)
    y_ref[:, 1 * GROUP_W:2 * GROUP_W] = (o1[...] * (e1 * inv)).astype(y_ref.dtype)
    y_ref[:, 2 * GROUP_W:3 * GROUP_W] = (o2[...] * (e2 * inv)).astype(y_ref.dtype)


def _attn_mix_call(os_, ls_, tm):
    M = os_[0].shape[0]
    spec = pl.BlockSpec((tm, GROUP_W), lambda i: (i, 0))
    return pl.pallas_call(
        _attn_mix_kernel,
        grid=(M // tm,),
        in_specs=[spec] * 6,
        out_specs=pl.BlockSpec((tm, ATTN_WIDTH), lambda i: (i, 0)),
        out_shape=jax.ShapeDtypeStruct((M, ATTN_WIDTH), BF16),
        compiler_params=_cparams(("arbitrary",)),
        name="attn_mix",
    )(*os_, *ls_)


QROWS = 16


def _attn_sample_kernel(qkv_ref, c0_ref, c1_ref, c2_ref, *rest, T, n_prev):
    y_ref, n0_ref, n1_ref, n2_ref, q16, npad, ysc = rest[n_prev:]
    q16[...] = jnp.zeros_like(q16)
    q16[0:T, :] = qkv_ref[0, :, 0:ATTN_WIDTH]
    outs, lses = [], []
    for g, (cref, nref) in enumerate(((c0_ref, n0_ref), (c1_ref, n1_ref), (c2_ref, n2_ref))):
        Lb = cref.shape[-1]
        dil = ATTN_GROUPS[g][1]
        npad[...] = jnp.zeros_like(npad)
        npad[0:T, 0:GROUP_W] = qkv_ref[0, :, C_AK + g * GROUP_W:C_AK + (g + 1) * GROUP_W]
        npad[0:T, GROUP_W:2 * GROUP_W] = qkv_ref[0, :, C_AV + g * GROUP_W:C_AV + (g + 1) * GROUP_W]
        new_t = npad[...].T
        ncols = Lb + LANE
        t = lax.broadcasted_iota(jnp.int32, (QROWS, ncols), 0)
        c = lax.broadcasted_iota(jnp.int32, (QROWS, ncols), 1)
        if dil == 1:
            valid = ((c >= t) & (c < Lb)) | ((c >= Lb) & (c - Lb <= t) & (c - Lb < T))
        else:
            valid = ((c < Lb) & ((c & (dil - 1)) == t)) | (c == Lb + t)
        lane = lax.broadcasted_iota(jnp.int32, (HEAD_DIM, LANE), 1)
        for h in range(ATTN_HPG):
            rk = slice(h * HEAD_DIM, (h + 1) * HEAD_DIM)
            rv = slice(GROUP_W + h * HEAD_DIM, GROUP_W + (h + 1) * HEAD_DIM)
            kc, vc = cref[0, 0, 0, h], cref[0, 0, 1, h]
            kt = jnp.concatenate([kc, new_t[rk, :]], axis=1).astype(BF16)
            vt = jnp.concatenate([vc, new_t[rv, :]], axis=1).astype(BF16)
            qh = q16[:, g * GROUP_W + h * HEAD_DIM:g * GROUP_W + (h + 1) * HEAD_DIM].astype(BF16)
            s = jnp.where(valid, _dot(qh, kt) * ATTN_SCALE, NEG)
            mx = jnp.max(s, axis=-1, keepdims=True)
            e = jnp.exp(s - mx)
            den = jnp.sum(e, axis=-1, keepdims=True)
            outs.append(_dot_nt(e.astype(BF16), vt) / den)
            lses.append(mx + jnp.log(den))
            for kv, blk, rn in ((0, kc, rk), (1, vc, rv)):
                sh = pltpu.roll(blk, Lb - T, 1)
                newc = pltpu.roll(new_t[rn, :], LANE - T, 1)
                if Lb > LANE:
                    nref[0, 0, kv, h, :, 0:Lb - LANE] = sh[:, 0:Lb - LANE]
                nref[0, 0, kv, h, :, Lb - LANE:Lb] = jnp.where(lane >= LANE - T, newc, sh[:, Lb - LANE:Lb])
    for h in range(ATTN_HPG):
        l0, l1, l2 = lses[h], lses[ATTN_HPG + h], lses[2 * ATTN_HPG + h]
        mx = jnp.maximum(jnp.maximum(l0, l1), l2)
        es = [jnp.exp(l0 - mx), jnp.exp(l1 - mx), jnp.exp(l2 - mx)]
        inv = 1.0 / (es[0] + es[1] + es[2])
        for g in range(3):
            c0 = g * GROUP_W + h * HEAD_DIM
            ysc[:, c0:c0 + HEAD_DIM] = outs[g * ATTN_HPG + h] * (es[g] * inv)
    y_ref[0] = ysc[0:T, :].astype(y_ref.dtype)


def _attn_sample_call(qkv, caches_t, layer, prev):
    B, T, _ = qkv.shape
    depth = caches_t[0].shape[0]
    for g, (win, dil) in enumerate(ATTN_GROUPS):
        assert caches_t[g].shape[-1] == win and win // dil == BAND and (g == 0 or T <= dil), \
            "sample attention assumes full window caches"
    cspecs = [pl.BlockSpec((1, 1) + c.shape[2:], lambda b: (layer, b, 0, 0, 0, 0)) for c in caches_t]
    n_prev = 0 if prev is None else 3
    prev_args = [] if prev is None else list(prev)
    prev_specs = [pl.BlockSpec(memory_space=pl.ANY)] * n_prev
    aliases = {} if prev is None else {4 + g: 1 + g for g in range(3)}
    outs = pl.pallas_call(
        functools.partial(_attn_sample_kernel, T=T, n_prev=n_prev),
        grid=(B,),
        in_specs=[pl.BlockSpec((1, T, 3 * ATTN_WIDTH), lambda b: (b, 0, 0))] + cspecs + prev_specs,
        out_specs=[pl.BlockSpec((1, T, ATTN_WIDTH), lambda b: (b, 0, 0))] + cspecs,
        out_shape=[jax.ShapeDtypeStruct((B, T, ATTN_WIDTH), BF16)]
        + [jax.ShapeDtypeStruct(c.shape, F32) for c in caches_t],
        scratch_shapes=[pltpu.VMEM((QROWS, ATTN_WIDTH), F32),
                        pltpu.VMEM((LANE, 2 * GROUP_W), F32),
                        pltpu.VMEM((QROWS, ATTN_WIDTH), F32)],
        input_output_aliases=aliases,
        compiler_params=_cparams(("arbitrary",)),
        name="attn_sample",
    )(qkv, *caches_t, *prev_args)
    return outs[0], outs[1:]


def _mlstm_kernel(mqk_ref, mv_ref, mi_ref, mf_ref, mo_ref, cst_ref, wc_ref, bc_ref, c0_ref, n0_ref, m0_ref,
                  g_ref, tri_ref, y_ref, c_ref, n_ref, m_ref, cso_ref,
                  xbuf, vbuf, cs, ns, ms, tpad, kwp, *, T, Lp):
    c = pl.program_id(1)
    last = pl.num_programs(1) - 1
    W = MLSTM_WIDTH

    @pl.when(c == 0)
    def _():
        xbuf[...] = jnp.zeros_like(xbuf)
        vbuf[...] = jnp.zeros_like(vbuf)
        tpad[...] = jnp.zeros_like(tpad)
        kwp[...] = jnp.zeros_like(kwp)
        xbuf[5:8, :] = cst_ref[0]
        cs[...] = c0_ref[0]
        ns[...] = n0_ref[0]
        ms[...] = m0_ref[0]

    xbuf[8:8 + T, :] = mqk_ref[0]
    vbuf[0:T, :] = mv_ref[0]
    w = wc_ref[...]
    y = (bc_ref[...] + xbuf[8:8 + Lp, :] * w[3:4] + xbuf[7:7 + Lp, :] * w[2:3]
         + xbuf[6:6 + Lp, :] * w[1:2] + xbuf[5:5 + Lp, :] * w[0:1])
    tail = xbuf[8 + T - 3:8 + T, :]
    xbuf[5:8, :] = tail

    @pl.when(c == last)
    def _():
        cso_ref[0] = tail

    qk = y * _sigmoid(y)

    rowid = lax.broadcasted_iota(jnp.int32, (Lp, LANE), 0)
    real = rowid < T
    tpad[0:T, :] = mi_ref[0]
    ig = jnp.where(real, tpad[0:Lp, :], NEG)
    tpad[0:T, :] = mf_ref[0]
    lf = jnp.where(real, _log_sigmoid(tpad[0:Lp, :]), 0.0)
    tri = tri_ref[...]
    F = jnp.dot(tri, lf, precision=HI, preferred_element_type=F32)
    inter = F + ms[...]
    tpad[0:Lp, :] = F
    FT = tpad[...].T
    tpad[0:Lp, :] = ig
    IT = tpad[...].T
    causal = tri > 0.5
    v_all = vbuf[0:Lp, :]
    gls = []

    for h in range(MLSTM_HEADS):
        hs = slice(h * HEAD_DIM, (h + 1) * HEAD_DIM)
        q = qk[:, h * HEAD_DIM:(h + 1) * HEAD_DIM]
        k = qk[:, W + h * HEAD_DIM:W + (h + 1) * HEAD_DIM] * (HEAD_DIM ** -0.5)
        v = v_all[:, hs]
        qb, kb, vb = q.astype(BF16), k.astype(BF16), v.astype(BF16)
        fcol = F[:, h:h + 1]
        icol = ig[:, h:h + 1]
        frow = FT[h:h + 1, 0:Lp]
        irow = IT[h:h + 1, 0:Lp]
        logw = jnp.where(causal, fcol - frow + irow, NEG)
        inter_h = inter[:, h:h + 1]
        mt = jnp.maximum(jnp.max(logw, axis=-1, keepdims=True), inter_h)
        A = jnp.exp(logw - mt) * _dot_nt(qb, kb)
        gq = jnp.exp(inter_h - mt)
        C = cs[h]
        nrow = ns[h:h + 1, :]
        num = _dot(A.astype(BF16), vb) + gq * _dot(qb, C.astype(BF16))
        den = jnp.sum(A, axis=-1, keepdims=True) + gq * jnp.sum(q * nrow, axis=-1, keepdims=True)
        hh = num / jnp.maximum(jnp.abs(den), jnp.exp(-mt))
        m_old = ms[0:1, h:h + 1]
        mL = mt[Lp - 1:Lp, :]
        f_last = F[Lp - 1:Lp, h:h + 1]
        wL = jnp.exp(f_last - fcol + icol - mL)
        gL = jnp.exp(f_last + m_old - mL)
        kw = k * wL
        kwp[0:Lp, hs] = kw
        gls.append(gL)
        ns[h:h + 1, :] = gL * nrow + jnp.sum(kw, axis=0, keepdims=True)
        ms[0:1, h:h + 1] = mL
        hn = hh * lax.rsqrt(jnp.mean(hh * hh, axis=-1, keepdims=True) + EPS) * g_ref[0:1, hs]
        yo = hn * _sigmoid(mo_ref[0, :, hs]) if Lp == T else hn[0:T] * _sigmoid(mo_ref[0, :, hs])
        y_ref[0, :, hs] = yo.astype(y_ref.dtype)

    kwt = kwp[...].T
    for h in range(MLSTM_HEADS):
        hs = slice(h * HEAD_DIM, (h + 1) * HEAD_DIM)
        upd = _dot(kwt[h * HEAD_DIM:(h + 1) * HEAD_DIM, 0:Lp].astype(BF16), v_all[:, hs].astype(BF16))
        cs[h] = gls[h] * cs[h] + upd

    @pl.when(c == last)
    def _():
        c_ref[0] = cs[...]
        n_ref[0] = ns[...]
        m_ref[0] = ms[...]


def _mlstm_call(proj, conv_state, w_conv, b_conv, c0, n0, m0, g_mlstm, chunk):
    B, S, N = proj.shape
    T = chunk
    nc = S // T
    Lp = max(16, T)
    H = MLSTM_HEADS
    m0p = jnp.pad(m0, ((0, 0), (0, LANE - H))).reshape(B, 1, LANE)
    tri = jnp.tril(jnp.ones((Lp, Lp), F32))

    def col(width, off):
        blk = off // width
        return pl.BlockSpec((1, T, width), lambda b, c: (b, c, blk))

    def const(shape):
        nd = len(shape)
        return pl.BlockSpec(shape, lambda b, c: (0,) * nd)

    def per_b(shape):
        nd = len(shape)
        return pl.BlockSpec((1,) + shape, lambda b, c: (b,) + (0,) * nd)

    outs = pl.pallas_call(
        functools.partial(_mlstm_kernel, T=T, Lp=Lp),
        grid=(B, nc),
        in_specs=[col(2 * MLSTM_WIDTH, C_MQK), col(MLSTM_WIDTH, C_MV), col(LANE, C_MI), col(LANE, C_MF),
                  col(MLSTM_WIDTH, C_MO), per_b((MLSTM_CONV - 1, 2 * MLSTM_WIDTH)),
                  const((MLSTM_CONV, 2 * MLSTM_WIDTH)), const((1, 2 * MLSTM_WIDTH)),
                  per_b((H, HEAD_DIM, HEAD_DIM)), per_b((H, HEAD_DIM)), per_b((1, LANE)),
                  const((1, MLSTM_WIDTH)), const((Lp, Lp))],
        out_specs=[pl.BlockSpec((1, T, MLSTM_WIDTH), lambda b, c: (b, c, 0)),
                   per_b((H, HEAD_DIM, HEAD_DIM)), per_b((H, HEAD_DIM)), per_b((1, LANE)),
                   per_b((MLSTM_CONV - 1, 2 * MLSTM_WIDTH))],
        out_shape=[jax.ShapeDtypeStruct((B, S, MLSTM_WIDTH), BF16),
                   jax.ShapeDtypeStruct((B, H, HEAD_DIM, HEAD_DIM), F32),
                   jax.ShapeDtypeStruct((B, H, HEAD_DIM), F32),
                   jax.ShapeDtypeStruct((B, 1, LANE), F32),
                   jax.ShapeDtypeStruct((B, MLSTM_CONV - 1, 2 * MLSTM_WIDTH), F32)],
        scratch_shapes=[pltpu.VMEM((8 + Lp, 2 * MLSTM_WIDTH), F32),
                        pltpu.VMEM((Lp, MLSTM_WIDTH), F32),
                        pltpu.VMEM((H, HEAD_DIM, HEAD_DIM), F32),
                        pltpu.VMEM((H, HEAD_DIM), F32),
                        pltpu.VMEM((1, LANE), F32),
                        pltpu.VMEM((LANE, LANE), F32),
                        pltpu.VMEM((LANE, MLSTM_WIDTH), F32)],
        compiler_params=_cparams(("arbitrary", "arbitrary")),
        name="mlstm",
    )(proj, proj, proj, proj, proj, conv_state, w_conv, b_conv.reshape(1, -1), c0, n0, m0p,
      g_mlstm.reshape(1, -1), tri)
    y, C, n, m, cso = outs
    return y, C, n, m[:, 0, :H], cso


def _gla_kernel(gqk_ref, gv_ref, gg_ref, ga_ref, wa_ref, wat_ref, s0_ref, g_ref, tri_ref, bd_ref, ee_ref,
                y_ref, s_ref, qkp, vp, gap, sbd, osc, *, T, Lp):
    c = pl.program_id(1)
    last = pl.num_programs(1) - 1
    KW, VW = GLA_K_WIDTH, GLA_V_WIDTH

    @pl.when(c == 0)
    def _():
        qkp[...] = jnp.zeros_like(qkp)
        vp[...] = jnp.zeros_like(vp)
        gap[...] = jnp.zeros_like(gap)
        sbd[...] = jnp.zeros_like(sbd)
        osc[...] = jnp.zeros_like(osc)
        for h in range(GLA_HEADS):
            sbd[h * GLA_DK:(h + 1) * GLA_DK, h * GLA_DV:(h + 1) * GLA_DV] = s0_ref[0, h]

    qkp[0:T, :] = gqk_ref[0]
    vp[0:T, :] = gv_ref[0]
    lane = lax.broadcasted_iota(jnp.int32, (T, LANE), 1)
    gap[0:T, :] = jnp.where(lane == GLA_RANK, 1.0, ga_ref[0])

    tri = tri_ref[...]
    ga = gap[0:Lp, :]
    rowid = lax.broadcasted_iota(jnp.int32, (Lp, KW), 0)
    la = jnp.dot(ga, wa_ref[...], precision=HI, preferred_element_type=F32)
    la = jnp.where(rowid < T, _log_sigmoid(la) / GLA_TAU, 0.0)
    bc = jnp.dot(tri[0:Lp, 0:Lp], la, precision=HI, preferred_element_type=F32)
    colid = lax.broadcasted_iota(jnp.int32, (KW, LANE), 1)
    lat = jnp.dot(wat_ref[...], gap[...].T, precision=HI, preferred_element_type=F32)
    lat = jnp.where(colid < T, _log_sigmoid(lat) / GLA_TAU, 0.0)
    bct = lax.dot_general(lat, tri, (((1,), (1,)), ((), ())), precision=HI,
                          preferred_element_type=F32)
    blcol = bct[:, LANE - 1:LANE]
    blrow = bc[Lp - 1:Lp, :]
    kt = qkp[...].T[KW:2 * KW, :]
    q = qkp[0:Lp, 0:KW] * (GLA_DK ** -0.5)
    k = qkp[0:Lp, KW:2 * KW]
    v = vp[0:Lp, :]
    vfull = vp[...]
    bd = bd_ref[...]
    safe = jnp.min(blrow) >= -GLA_SAFE_DECAY

    @pl.when(safe)
    def _():
        qb = (q * jnp.exp(bc)).astype(BF16)
        kb = (k * jnp.exp(-bc)).astype(BF16)
        osc[0:Lp, :] = _dot(qb, sbd[...].astype(BF16))
        causal = tri[0:Lp, 0:Lp] > 0.5
        vb = v.astype(BF16)
        for h in range(GLA_HEADS):
            ks = slice(h * GLA_DK, (h + 1) * GLA_DK)
            vs = slice(h * GLA_DV, (h + 1) * GLA_DV)
            A = jnp.where(causal, _dot_nt(qb[:, ks], kb[:, ks]), 0.0)
            osc[0:Lp, vs] = osc[0:Lp, vs] + _dot(A.astype(BF16), vb[:, vs])
        klt = (kt * jnp.exp(blcol - bct)).astype(BF16)
        sbd[...] = bd * (jnp.exp(blcol) * sbd[...] + _dot(klt, vfull.astype(BF16)))

    @pl.when(jnp.logical_not(safe))
    def _():
        srow = lax.broadcasted_iota(jnp.int32, (LANE, LANE), 0)

        def body(t, carry):
            sel = (srow == t).astype(F32)
            lac = jnp.dot(lat, sel, precision=HI, preferred_element_type=F32)
            kc = jnp.dot(kt, sel, precision=HI, preferred_element_type=F32)
            dec = jnp.concatenate([jnp.exp(lac)] * (VW // LANE), axis=1)
            kcw = jnp.concatenate([kc] * (VW // LANE), axis=1)
            vrow = vp[pl.ds(t, 1), :]
            snew = bd * (dec * sbd[...] + kcw * vrow)
            sbd[...] = snew
            qrow = jnp.broadcast_to(qkp[pl.ds(t, 1), 0:KW] * (GLA_DK ** -0.5), (8, KW))
            orow = jnp.dot(qrow, snew, precision=HI, preferred_element_type=F32)
            osc[pl.ds(t, 1), :] = orow[0:1]
            return carry

        lax.fori_loop(0, T, body, 0)

    o = osc[0:Lp, :]
    ms = jnp.dot(o * o, ee_ref[...], precision=HI, preferred_element_type=F32)
    og = o * lax.rsqrt(ms + EPS) * g_ref[...]
    gg = gg_ref[0]
    yo = (og if Lp == T else og[0:T]) * (gg * _sigmoid(gg))
    y_ref[0] = yo.astype(y_ref.dtype)

    @pl.when(c == last)
    def _():
        for h in range(GLA_HEADS):
            s_ref[0, h] = sbd[h * GLA_DK:(h + 1) * GLA_DK, h * GLA_DV:(h + 1) * GLA_DV]


def _gla_call(proj, w_a2, b_a2, s0, g_gla, chunk):
    B, S, N = proj.shape
    T = chunk
    nc = S // T
    Lp = max(16, T)
    H, KW, VW = GLA_HEADS, GLA_K_WIDTH, GLA_V_WIDTH
    wa = jnp.zeros((LANE, KW), F32).at[:GLA_RANK].set(w_a2).at[GLA_RANK].set(b_a2)
    tri = jnp.tril(jnp.ones((LANE, LANE), F32))
    hk = jnp.arange(KW) // GLA_DK
    hv = jnp.arange(VW) // GLA_DV
    bd = (hk[:, None] == hv[None, :]).astype(F32)
    ee = (hv[:, None] == hv[None, :]).astype(F32) / GLA_DV

    def col(width, off):
        blk = off // width
        return pl.BlockSpec((1, T, width), lambda b, c: (b, c, blk))

    def const(shape):
        nd = len(shape)
        return pl.BlockSpec(shape, lambda b, c: (0,) * nd)

    y, s = pl.pallas_call(
        functools.partial(_gla_kernel, T=T, Lp=Lp),
        grid=(B, nc),
        in_specs=[col(2 * KW, C_GQK), col(VW, C_GV), col(VW, C_GG), col(LANE, C_GA),
                  const((LANE, KW)), const((KW, LANE)),
                  pl.BlockSpec((1, H, GLA_DK, GLA_DV), lambda b, c: (b, 0, 0, 0)),
                  const((1, VW)), const((LANE, LANE)), const((KW, VW)), const((VW, VW))],
        out_specs=[pl.BlockSpec((1, T, VW), lambda b, c: (b, c, 0)),
                   pl.BlockSpec((1, H, GLA_DK, GLA_DV), lambda b, c: (b, 0, 0, 0))],
        out_shape=[jax.ShapeDtypeStruct((B, S, VW), BF16),
                   jax.ShapeDtypeStruct((B, H, GLA_DK, GLA_DV), F32)],
        scratch_shapes=[pltpu.VMEM((LANE, 2 * KW), F32),
                        pltpu.VMEM((LANE, VW), F32),
                        pltpu.VMEM((LANE, LANE), F32),
                        pltpu.VMEM((KW, VW), F32),
                        pltpu.VMEM((LANE, VW), F32)],
        compiler_params=_cparams(("arbitrary", "arbitrary")),
        name="gla",
    )(proj, proj, proj, proj, wa, wa.T, s0, g_gla.reshape(1, -1), tri, bd, ee)
    return y, s


def _outproj_kernel(ya_ref, ym_ref, yg_ref, w_ref, x_ref, gt_ref, g2_ref, sc_ref, sh_ref, xo_ref, h2_ref):
    a, b = ATTN_WIDTH, ATTN_WIDTH + MLSTM_WIDTH
    acc = _dot(ya_ref[0], w_ref[0:a, :])
    acc = acc + _dot(ym_ref[0], w_ref[a:b, :])
    acc = acc + _dot(yg_ref[0], w_ref[b:, :])
    x = x_ref[0] + gt_ref[0] * acc
    xo_ref[0] = x
    y = x * lax.rsqrt(jnp.mean(x * x, axis=-1, keepdims=True) + EPS) * g2_ref[...]
    h2_ref[0] = (y * (1.0 + sc_ref[0]) + sh_ref[0]).astype(h2_ref.dtype)


def _outproj_call(ya, ym, yg, w_out, x, gt, g2, sc, sh, tm):
    G, R, D = x.shape
    rr = gt.shape[1]
    mod_spec = (pl.BlockSpec((1, 1, D), lambda b, i: (b, 0, 0)) if rr == 1
                else pl.BlockSpec((1, tm, D), lambda b, i: (b, i, 0)))

    def act(width):
        return pl.BlockSpec((1, tm, width), lambda b, i: (b, i, 0))

    return pl.pallas_call(
        _outproj_kernel,
        grid=(G, R // tm),
        in_specs=[act(ATTN_WIDTH), act(MLSTM_WIDTH), act(GLA_V_WIDTH),
                  pl.BlockSpec(w_out.shape, lambda b, i: (0, 0)),
                  act(D), mod_spec, pl.BlockSpec((1, D), lambda b, i: (0, 0)), mod_spec, mod_spec],
        out_specs=[act(D), act(D)],
        out_shape=[jax.ShapeDtypeStruct((G, R, D), F32), jax.ShapeDtypeStruct((G, R, D), BF16)],
        compiler_params=_cparams(("arbitrary", "arbitrary")),
        name="outproj",
    )(ya, ym, yg, w_out, x, gt, g2.reshape(1, D), sc, sh)


def _ffn_kernel(h_ref, wg_ref, wu_ref, wc_ref, bc_ref, wo_ref, x_ref, gt_ref, init_ref,
                xo_ref, st_ref, ubuf, cbuf, *, tm, u, R, tiles_per_seq):
    m = pl.program_id(1)
    f = pl.program_id(2)
    nf = pl.num_programs(2)
    h = h_ref[0]
    ug = _dot(h, wg_ref[...])
    uu = _dot(h, wu_ref[...])

    @pl.when(m % tiles_per_seq == 0)
    def _():
        ubuf[0:R, :] = init_ref[0]

    @pl.when(m % tiles_per_seq != 0)
    def _():
        ubuf[0:R, :] = cbuf[f]

    ubuf[R:R + tm, :] = ug
    w = wc_ref[...]
    gate = bc_ref[...] + ug * w[2:3] + ubuf[R - u:R - u + tm, :] * w[1:2] + ubuf[R - 2 * u:R - 2 * u + tm, :] * w[0:1]
    tail = ubuf[tm:tm + R, :]
    cbuf[f] = tail
    st_ref[0, 0] = tail
    act = (gate * _sigmoid(gate) * uu).astype(BF16)
    part = _dot(act, wo_ref[...])

    @pl.when(f == 0)
    def _():
        xo_ref[0] = part

    @pl.when(f != 0)
    def _():
        xo_ref[0] = xo_ref[0] + part

    @pl.when(f == nf - 1)
    def _():
        xo_ref[0] = x_ref[0] + gt_ref[0] * xo_ref[0]


def _ffn_call(h2, w_g, w_u, w_conv, b_conv, w_o, x, gt, init, tm, tf, u):
    G, rows, D = x.shape
    F = w_g.shape[1]
    R = max(8, 2 * u)
    nm, nf = rows // tm, F // tf
    rr = gt.shape[1]
    mod_spec = (pl.BlockSpec((1, 1, D), lambda b, i, f: (b, 0, 0)) if rr == 1
                else pl.BlockSpec((1, tm, D), lambda b, i, f: (b, i, 0)))
    xo, st = pl.pallas_call(
        functools.partial(_ffn_kernel, tm=tm, u=u, R=R, tiles_per_seq=nm),
        grid=(G, nm, nf),
        in_specs=[pl.BlockSpec((1, tm, D), lambda b, i, f: (b, i, 0)),
                  pl.BlockSpec((D, tf), lambda b, i, f: (0, f)),
                  pl.BlockSpec((D, tf), lambda b, i, f: (0, f)),
                  pl.BlockSpec((FFN_CONV, tf), lambda b, i, f: (0, f)),
                  pl.BlockSpec((1, tf), lambda b, i, f: (0, f)),
                  pl.BlockSpec((tf, D), lambda b, i, f: (f, 0)),
                  pl.BlockSpec((1, tm, D), lambda b, i, f: (b, i, 0)),
                  mod_spec,
                  pl.BlockSpec((1, R, tf), lambda b, i, f: (b, 0, f))],
        out_specs=[pl.BlockSpec((1, tm, D), lambda b, i, f: (b, i, 0)),
                   pl.BlockSpec((1, 1, R, tf), lambda b, i, f: (b, i, 0, f))],
        out_shape=[jax.ShapeDtypeStruct((G, rows, D), F32),
                   jax.ShapeDtypeStruct((G, nm, R, F), F32)],
        scratch_shapes=[pltpu.VMEM((R + tm, tf), F32), pltpu.VMEM((nf, R, tf), F32)],
        compiler_params=_cparams(("arbitrary", "arbitrary", "arbitrary")),
        name="ffn",
    )(h2, w_g, w_u, w_conv, b_conv.reshape(1, F), w_o, x, gt, init)
    return xo, st[:, nm - 1]


def _pack_w_kernel(w_ref, o_ref):
    x = w_ref[0]
    rows = x.shape[0]
    n_in = x.shape[1]
    attn = 3 * ATTN_WIDTH
    gates = attn + 3 * MLSTM_WIDTH
    tail = gates + 2 * MLSTM_HEADS
    o_ref[0, :, 0:attn] = x[:, 0:attn].astype(BF16)
    lane = lax.broadcasted_iota(jnp.int32, (rows, LANE), 1)
    gwin = x[:, gates:gates + LANE]
    o_ref[0, :, C_MI:C_MI + LANE] = jnp.where(lane < MLSTM_HEADS, gwin, 0.0).astype(BF16)
    o_ref[0, :, C_MF:C_MF + LANE] = jnp.where(lane < MLSTM_HEADS, pltpu.roll(gwin, LANE - MLSTM_HEADS, 1),
                                              0.0).astype(BF16)
    o_ref[0, :, C_MQK:C_MQK + 3 * MLSTM_WIDTH] = x[:, attn:gates].astype(BF16)
    o_ref[0, :, C_GA:N_PACK] = jnp.zeros((rows, N_PACK - C_GA), BF16)
    o_ref[0, :, C_MO:C_MO + n_in - tail] = x[:, tail:n_in].astype(BF16)


def _pack_w_call(w_in):
    depth, d, n_in = w_in.shape
    tk = 256
    return pl.pallas_call(
        _pack_w_kernel,
        grid=(depth, d // tk),
        in_specs=[pl.BlockSpec((1, tk, n_in), lambda l, i: (l, i, 0))],
        out_specs=pl.BlockSpec((1, tk, N_PACK), lambda l, i: (l, i, 0)),
        out_shape=jax.ShapeDtypeStruct((depth, d, N_PACK), BF16),
        compiler_params=_cparams(("arbitrary", "arbitrary")),
        name="pack_w_in",
    )(w_in)


def _pack_in_proj(w_in, b_in):
    def split(a):
        out, off = [], 0
        for s in IN_SPLITS:
            out.append(a[..., off:off + s])
            off += s
        return out

    def pad(a, n):
        return jnp.pad(a, [(0, 0)] * (a.ndim - 1) + [(0, n - a.shape[-1])])

    def pack(a):
        aq, ak, av, mqk, mv, mi, mf, mo, gq, gk, gv, gg, ga = split(a)
        return jnp.concatenate([aq, ak, av, pad(mi, LANE), pad(mf, LANE), mqk, mv, mo, gq, gk, gv, gg,
                                pad(ga, LANE)], axis=-1)

    return _pack_w_call(w_in), pack(b_in)


def _rope_tables(pos):
    half = HEAD_DIM // 2
    inv_freq = jnp.power(ROPE_THETA, -jnp.arange(half, dtype=F32) / half)
    ang = pos.astype(F32)[:, None] * inv_freq[None, :]
    cos, sin = jnp.cos(ang), jnp.sin(ang)
    zero = jnp.zeros_like(sin)
    reps = LANE // HEAD_DIM
    return (jnp.tile(jnp.concatenate([cos, cos], -1), (1, reps)),
            jnp.tile(jnp.concatenate([-sin, zero], -1), (1, reps)),
            jnp.tile(jnp.concatenate([zero, sin], -1), (1, reps)))


def _pick_tile(n, pref):
    t = math.gcd(n, pref)
    return t


def kernel(x_prompt, x_sample, c_prompt, c_sample, cache_win0_kv, cache_win1_kv, cache_win2_kv, state_mlstm_C, state_mlstm_n, state_mlstm_m, state_mlstm_conv, state_gla_S, state_ffn_conv, w_ada, b_ada, g_norm1, g_norm2, w_in, b_in, w_mconv, b_mconv, g_mlstm, w_gla_a2, b_gla_a2, g_gla, w_out, w_ff_in, w_fconv, b_fconv, w_ff_out, g_final):
    B, S, D = x_prompt.shape
    Bs, Ts, _ = x_sample.shape
    depth = w_ada.shape[0]
    d_ff = w_fconv.shape[-1]
    caches = (cache_win0_kv, cache_win1_kv, cache_win2_kv)
    Ms = Bs * Ts

    n_c = B + Bs
    rows_c = -(-n_c // 8) * 8
    c_all = jnp.pad(jnp.concatenate([c_prompt, c_sample], axis=0), ((0, rows_c - n_c), (0, 0)))
    mod = _ada_call(c_all, w_ada, b_ada).reshape(depth, rows_c, 6, D)

    w_in_p, b_in_p = _pack_in_proj(w_in, b_in)
    w_out_b = w_out.astype(BF16)
    w_g = w_ff_in[..., :d_ff].astype(BF16)
    w_u = w_ff_in[..., d_ff:].astype(BF16)
    w_o = w_ff_out.astype(BF16)

    rope_p = _rope_tables(jnp.arange(S))
    rope_s = _rope_tables(PAST_LEN + jnp.repeat(jnp.arange(Ts), Bs))

    tm_p = _pick_tile(S, 512)
    tn = 1024
    tf = _pick_tile(d_ff, 512)

    xp = x_prompt
    xs = x_sample.transpose(1, 0, 2).reshape(1, Ms, D)
    zeros_p = {
        'mconv': jnp.zeros((B, MLSTM_CONV - 1, 2 * MLSTM_WIDTH), F32),
        'C': jnp.zeros((B, MLSTM_HEADS, HEAD_DIM, HEAD_DIM), F32),
        'n': jnp.zeros((B, MLSTM_HEADS, HEAD_DIM), F32),
        'm': jnp.zeros((B, MLSTM_HEADS), F32),
        'S': jnp.zeros((B, GLA_HEADS, GLA_DK, GLA_DV), F32),
        'fconv': jnp.zeros((B, 8, d_ff), F32),
    }
    names = ('win0', 'win1', 'win2', 'C', 'n', 'm', 'mconv', 'S', 'fconv')
    col_p = {k: [] for k in names}
    col_s = {k: [] for k in names}
    caches_t = [c.transpose(0, 1, 3, 4, 5, 2) for c in caches]
    new_caches = None

    for l in range(depth):
        mp = mod[l, :B]
        ms_ = jnp.tile(mod[l, B:B + Bs], (Ts, 1, 1))

        def mods_p(i):
            return mp[:, i:i + 1, :]

        def mods_s(i):
            return ms_[None, :, i, :]

        h = _normmod_call(xp, g_norm1[l], mods_p(1), mods_p(0), BF16, tm_p)
        proj = _inproj_call(h.reshape(B * S, D), w_in_p[l], b_in_p[l][None], *rope_p, tm_p, tn)
        proj = proj.reshape(B, S, N_PACK)
        os_, ls_ = [], []
        for g, (win, dil) in enumerate(ATTN_GROUPS):
            o, lse = _attn_prompt_call(proj, g, dil)
            os_.append(o.reshape(B * S, GROUP_W))
            ls_.append(lse.reshape(B * S, GROUP_W))
            col_p[f'win{g}'].append(_win_extract_call(proj, g, min(win, S)))
        ya = _attn_mix_call(os_, ls_, tm_p).reshape(B, S, ATTN_WIDTH)
        ym, C, n, m, cso = _mlstm_call(proj, zeros_p['mconv'], w_mconv[l], b_mconv[l], zeros_p['C'],
                                        zeros_p['n'], zeros_p['m'], g_mlstm[l], math.gcd(S, MLSTM_CHUNK))
        yg, Sg = _gla_call(proj, w_gla_a2[l], b_gla_a2[l], zeros_p['S'], g_gla[l], math.gcd(S, GLA_CHUNK))
        xp, h2 = _outproj_call(ya, ym, yg, w_out_b[l], xp, mods_p(2), g_norm2[l], mods_p(4), mods_p(3), tm_p)
        xp, fst = _ffn_call(h2, w_g[l], w_u[l], w_fconv[l], b_fconv[l], w_o[l], xp, mods_p(5),
                            zeros_p['fconv'], tm_p, tf, 1)
        for k_, v_ in (('C', C), ('n', n), ('m', m), ('mconv', cso), ('S', Sg), ('fconv', fst[:, 6:8])):
            col_p[k_].append(v_)

        h = _normmod_call(xs, g_norm1[l], mods_s(1), mods_s(0), BF16, Ms)
        proj = _inproj_call(h.reshape(Ms, D), w_in_p[l], b_in_p[l][None], *rope_s, Ms, tn)
        proj_b = proj.reshape(Ts, Bs, N_PACK).transpose(1, 0, 2)
        ya, new_caches = _attn_sample_call(proj_b[:, :, :3 * ATTN_WIDTH], caches_t, l, new_caches)
        ym, C, n, m, cso = _mlstm_call(proj_b, state_mlstm_conv[l], w_mconv[l], b_mconv[l], state_mlstm_C[l],
                                        state_mlstm_n[l], state_mlstm_m[l], g_mlstm[l], Ts)
        yg, Sg = _gla_call(proj_b, w_gla_a2[l], b_gla_a2[l], state_gla_S[l], g_gla[l], Ts)

        def tmaj(a):
            return a.transpose(1, 0, 2).reshape(1, Ms, a.shape[-1])

        xs, h2 = _outproj_call(tmaj(ya), tmaj(ym), tmaj(yg), w_out_b[l], xs, mods_s(2), g_norm2[l],
                               mods_s(4), mods_s(3), Ms)
        n_st = (FFN_CONV - 1) * Bs
        r_st = max(8, n_st)
        init = state_ffn_conv[l].transpose(1, 0, 2).reshape(1, n_st, d_ff)
        init = jnp.pad(init, ((0, 0), (r_st - n_st, 0), (0, 0)))
        xs, fst = _ffn_call(h2, w_g[l], w_u[l], w_fconv[l], b_fconv[l], w_o[l], xs, mods_s(5), init, Ms, tf, Bs)
        fst = fst[:, r_st - n_st:].reshape(FFN_CONV - 1, Bs, d_ff).transpose(1, 0, 2)
        for k_, v_ in (('C', C), ('n', n), ('m', m), ('mconv', cso), ('S', Sg), ('fconv', fst)):
            col_s[k_].append(v_)

    y_prompt = _normmod_call(xp, g_final, jnp.zeros((B, 1, D), F32), jnp.zeros((B, 1, D), F32), F32, tm_p)
    y_sample = _normmod_call(xs, g_final, jnp.zeros((1, 1, D), F32), jnp.zeros((1, 1, D), F32), F32, Ms)
    y_sample = y_sample.reshape(Ts, Bs, D).transpose(1, 0, 2)
    sp = {k: jnp.stack(v, axis=0) for k, v in col_p.items()}
    ss = {k: jnp.stack(v, axis=0) for k, v in col_s.items() if v}
    for g in range(3):
        ss[f'win{g}'] = new_caches[g].transpose(0, 1, 5, 2, 3, 4)
    return (y_prompt, y_sample, sp['win0'], ss['win0'], sp['win1'], ss['win1'], sp['win2'], ss['win2'],
            sp['C'], ss['C'], sp['n'], ss['n'], sp['m'], ss['m'], sp['mconv'], ss['mconv'],
            sp['S'], ss['S'], sp['fconv'], ss['fconv'])
```

```python
import functools
import math

import jax
import jax.numpy as jnp
from jax import lax
from jax.experimental import pallas as pl
from jax.experimental.pallas import tpu as pltpu

F32 = jnp.float32
BF16 = jnp.bfloat16
HI = lax.Precision.HIGHEST

HEAD_DIM = 64
ATTN_GROUPS = ((128, 1), (512, 4), (2048, 16))
ATTN_HPG = 4
ATTN_WIDTH = 3 * ATTN_HPG * HEAD_DIM
GROUP_W = ATTN_HPG * HEAD_DIM
ATTN_SCALE = HEAD_DIM ** -0.5
ROPE_THETA = 10000.0
BAND = 128
RES_UNROLL = 4
MLSTM_HEADS = 10
MLSTM_WIDTH = MLSTM_HEADS * HEAD_DIM
MLSTM_CONV = 4
MLSTM_CHUNK = 128
GLA_HEADS = 10
GLA_DK = 32
GLA_DV = 64
GLA_K_WIDTH = GLA_HEADS * GLA_DK
GLA_V_WIDTH = GLA_HEADS * GLA_DV
GLA_RANK = 16
GLA_TAU = 16.0
GLA_CHUNK = 128
GLA_SAFE_DECAY = 60.0
FFN_CONV = 3
FFN_SUB = 256
EPS = 1e-6
PAST_LEN = 8192
NEG = -1e30

LANE = 128
VMEM_LIMIT = 56 * 1024 * 1024

IN_SPLITS = (ATTN_WIDTH, ATTN_WIDTH, ATTN_WIDTH, 2 * MLSTM_WIDTH, MLSTM_WIDTH, MLSTM_HEADS, MLSTM_HEADS,
             MLSTM_WIDTH, GLA_K_WIDTH, GLA_K_WIDTH, GLA_V_WIDTH, GLA_V_WIDTH, GLA_RANK)

C_AQ, C_AK, C_AV = 0, 768, 1536
C_MI, C_MF = 2304, 2432
C_MQK = 2560
C_MV, C_MO = 3840, 4480
C_GQK, C_GV, C_GG = 5120, 5760, 6400
C_GA = 7040
N_PACK = 7168
ROPE_COLS = 2 * ATTN_WIDTH


def _cparams(sem):
    return pltpu.CompilerParams(dimension_semantics=sem, vmem_limit_bytes=VMEM_LIMIT)


def _sigmoid(x):
    return 1.0 / (1.0 + jnp.exp(-x))


def _log_sigmoid(x):
    return jnp.minimum(x, 0.0) - jnp.log(1.0 + jnp.exp(-jnp.abs(x)))


def _dot(a, b):
    return jnp.dot(a, b, preferred_element_type=F32)


def _dot_nt(a, b):
    return lax.dot_general(a, b, (((1,), (1,)), ((), ())), preferred_element_type=F32)


def _split3(x):
    hi = x.astype(BF16)
    r1 = x - hi.astype(F32)
    mid = r1.astype(BF16)
    lo = (r1 - mid.astype(F32)).astype(BF16)
    return hi, mid, lo


def _dot_sel(sel, x, parts=3):
    sb = sel.astype(BF16)
    return sum(_dot(sb, p) for p in _split3(x)[:parts])


def _dot_sel_rhs(x, sel, parts=2):
    sb = sel.astype(BF16)
    return sum(_dot(p, sb) for p in _split3(x)[:parts])


def _dot_sel_nt(x, sel):
    sb = sel.astype(BF16)
    return sum(_dot_nt(p, sb) for p in _split3(x))


def _ada_kernel(c_ref, w_ref, b_ref, o_ref):
    c = c_ref[...]
    s = (c * _sigmoid(c)).astype(BF16)
    o_ref[0] = _dot(s, w_ref[0].astype(BF16)) + b_ref[0]


def _ada_call(c_all, w_ada, b_ada):
    depth, d, n = w_ada.shape
    rows = c_all.shape[0]
    tn = 1024
    return pl.pallas_call(
        _ada_kernel,
        grid=(depth, n // tn),
        in_specs=[pl.BlockSpec((rows, d), lambda l, j: (0, 0)),
                  pl.BlockSpec((1, d, tn), lambda l, j: (l, 0, j)),
                  pl.BlockSpec((1, 1, tn), lambda l, j: (l, 0, j))],
        out_specs=pl.BlockSpec((1, rows, tn), lambda l, j: (l, 0, j)),
        out_shape=jax.ShapeDtypeStruct((depth, rows, n), F32),
        compiler_params=_cparams(("arbitrary", "arbitrary")),
        name="ada",
    )(c_all, w_ada, b_ada.reshape(depth, 1, n))


def _normmod_kernel(x_ref, g_ref, sc_ref, sh_ref, o_ref):
    x = x_ref[0]
    y = x * lax.rsqrt(jnp.mean(x * x, axis=-1, keepdims=True) + EPS) * g_ref[...]
    o_ref[0] = (y * (1.0 + sc_ref[0]) + sh_ref[0]).astype(o_ref.dtype)


def _normmod_call(x, g, sc, sh, out_dtype, tm):
    G, R, D = x.shape
    rr = sc.shape[1]
    mod_spec = (pl.BlockSpec((1, 1, D), lambda b, i: (b, 0, 0)) if rr == 1
                else pl.BlockSpec((1, tm, D), lambda b, i: (b, i, 0)))
    return pl.pallas_call(
        _normmod_kernel,
        grid=(G, R // tm),
        in_specs=[pl.BlockSpec((1, tm, D), lambda b, i: (b, i, 0)),
                  pl.BlockSpec((1, D), lambda b, i: (0, 0)),
                  mod_spec, mod_spec],
        out_specs=pl.BlockSpec((1, tm, D), lambda b, i: (b, i, 0)),
        out_shape=jax.ShapeDtypeStruct((G, R, D), out_dtype),
        compiler_params=_cparams(("arbitrary", "arbitrary")),
        name="normmod",
    )(x, g.reshape(1, D), sc, sh)


def _rope_chunk(x, cos, sin_a, sin_b):
    return x * cos + pltpu.roll(x, 96, 1) * sin_a + pltpu.roll(x, 32, 1) * sin_b


def _inproj_kernel(h_ref, w_ref, b_ref, cos_ref, sa_ref, sb_ref, o_ref, *, tn):
    j = pl.program_id(0)
    o_ref[...] = _dot(h_ref[...], w_ref[...]) + b_ref[...]
    n_chunks = tn // LANE
    rope_tiles = -(-ROPE_COLS // tn)
    for jt in range(rope_tiles):
        chunks = min(n_chunks, (ROPE_COLS - jt * tn) // LANE)

        @pl.when(j == jt)
        def _():
            cos, sa, sb = cos_ref[...], sa_ref[...], sb_ref[...]
            for c in range(chunks):
                sl = slice(c * LANE, (c + 1) * LANE)
                o_ref[:, sl] = _rope_chunk(o_ref[:, sl], cos, sa, sb)


def _inproj_call(h, w, b, cos, sin_a, sin_b, tm, tn):
    M, D = h.shape
    N = w.shape[1]
    tp = cos.shape[0] // tm
    tab = pl.BlockSpec((tm, LANE), lambda j, i: (i % tp, 0))
    return pl.pallas_call(
        functools.partial(_inproj_kernel, tn=tn),
        grid=(N // tn, M // tm),
        in_specs=[pl.BlockSpec((tm, D), lambda j, i: (i, 0)),
                  pl.BlockSpec((D, tn), lambda j, i: (0, j)),
                  pl.BlockSpec((1, tn), lambda j, i: (0, j)),
                  tab, tab, tab],
        out_specs=pl.BlockSpec((tm, tn), lambda j, i: (i, j)),
        out_shape=jax.ShapeDtypeStruct((M, N), F32),
        compiler_params=_cparams(("arbitrary", "arbitrary")),
        name="inproj",
    )(h, w, b, cos, sin_a, sin_b)


def _attn_prompt_kernel(q_ref, kp_ref, kc_ref, vp_ref, vc_ref, o_ref, l_ref, obuf, lbuf, *, dil, nq):
    i = pl.program_id(2)
    row = lax.broadcasted_iota(jnp.int32, (BAND, 2 * BAND), 0)
    col = lax.broadcasted_iota(jnp.int32, (BAND, 2 * BAND), 1)
    band = (col >= row) & (col <= row + BAND)
    band_first = band & ((col >= BAND) | (i > 0))
    nr = nq * BAND

    def residues(rs):
        rows_of, items = [], []
        for u, r in enumerate(rs):
            if dil == 1:
                rows, prev_rows = slice(None), slice(nr - BAND, nr)
            else:
                rows, prev_rows = pl.ds(r, nr, stride=dil), pl.ds((nr - BAND) * dil + r, BAND, stride=dil)
            rows_of.append(rows)
            q_all = q_ref[0, rows, :].astype(BF16)
            k_all = jnp.concatenate([kp_ref[0, prev_rows, :], kc_ref[0, rows, :]], axis=0).astype(BF16)
            v_all = jnp.concatenate([vp_ref[0, prev_rows, :], vc_ref[0, rows, :]], axis=0).astype(BF16)
            for sb in range(nq):
                qs = slice(sb * BAND, (sb + 1) * BAND)
                ks = slice(sb * BAND, (sb + 2) * BAND)
                for h in range(LANE // HEAD_DIM):
                    hs = slice(h * HEAD_DIM, (h + 1) * HEAD_DIM)
                    items.append((u, qs, hs, q_all[qs, hs], k_all[ks, hs], v_all[ks, hs],
                                  band_first if sb == 0 else band))
        ss = [jnp.where(it[6], _dot_nt(it[3], it[4]) * ATTN_SCALE, NEG) for it in items]
        mxs = [jnp.max(s, axis=-1, keepdims=True) for s in ss]
        es = [jnp.exp(s - mx) for s, mx in zip(ss, mxs)]
        dens = [jnp.sum(e, axis=-1, keepdims=True) for e in es]
        pvs = [_dot(e.astype(BF16), it[5]) for e, it in zip(es, items)]
        outs = [pv / den for pv, den in zip(pvs, dens)]
        lses = [jnp.broadcast_to(mx + jnp.log(den), (BAND, HEAD_DIM)) for mx, den in zip(mxs, dens)]
        for it, o, l in zip(items, outs, lses):
            obuf[it[0], it[1], it[2]] = o
            lbuf[it[0], it[1], it[2]] = l
        for u, rows in enumerate(rows_of):
            o_ref[0, rows, :] = obuf[u]
            l_ref[0, rows, :] = lbuf[u]

    if dil == 1:
        residues([0])
    elif dil <= RES_UNROLL:
        residues(list(range(dil)))
    else:
        def body(j, carry):
            residues([j * RES_UNROLL + u for u in range(RES_UNROLL)])
            return carry

        lax.fori_loop(0, dil // RES_UNROLL, body, 0)


def _attn_prompt_call(proj, g, dil):
    B, S, N = proj.shape
    nq = max(1, math.gcd(S, 512) // (BAND * dil))
    tb = nq * BAND * dil
    nb = S // tb
    hp = GROUP_W // LANE
    cq, ck, cv = (C_AQ + g * GROUP_W) // LANE, (C_AK + g * GROUP_W) // LANE, (C_AV + g * GROUP_W) // LANE

    def spec(cblk, prev):
        if prev:
            return pl.BlockSpec((1, tb, LANE), lambda b, p, i: (b, jnp.maximum(i - 1, 0), cblk + p))
        return pl.BlockSpec((1, tb, LANE), lambda b, p, i: (b, i, cblk + p))

    ospec = pl.BlockSpec((1, tb, LANE), lambda b, p, i: (b, i, p))
    o, lse = pl.pallas_call(
        functools.partial(_attn_prompt_kernel, dil=dil, nq=nq),
        grid=(B, hp, nb),
        in_specs=[spec(cq, False), spec(ck, True), spec(ck, False), spec(cv, True), spec(cv, False)],
        out_specs=[ospec, ospec],
        out_shape=[jax.ShapeDtypeStruct((B, S, GROUP_W), F32)] * 2,
        scratch_shapes=[pltpu.VMEM((min(dil, RES_UNROLL), nq * BAND, LANE), F32)] * 2,
        compiler_params=_cparams(("arbitrary", "arbitrary", "arbitrary")),
        name=f"attn_prompt_g{g}",
    )(proj, proj, proj, proj, proj)
    return o, lse


def _win_extract_kernel(x_ref, o_ref):
    o_ref[0, 0] = x_ref[0].T


def _win_extract_call(proj, g, keep):
    B, S, N = proj.shape
    chunk = min(keep, 512)
    first = (S - keep) // chunk
    cblk = C_AK // GROUP_W + g
    step = (C_AV - C_AK) // GROUP_W
    out = pl.pallas_call(
        _win_extract_kernel,
        grid=(B, 2, keep // chunk),
        in_specs=[pl.BlockSpec((1, chunk, GROUP_W), lambda b, kv, j: (b, first + j, cblk + step * kv))],
        out_specs=pl.BlockSpec((1, 1, GROUP_W, chunk), lambda b, kv, j: (b, kv, 0, j)),
        out_shape=jax.ShapeDtypeStruct((B, 2, GROUP_W, keep), F32),
        compiler_params=_cparams(("arbitrary", "arbitrary", "arbitrary")),
        name=f"win_extract_g{g}",
    )(proj)
    return out.reshape(B, 2, ATTN_HPG, HEAD_DIM, keep).transpose(0, 4, 1, 2, 3)


def _attn_mix_kernel(o0, o1, o2, l0, l1, l2, y_ref):
    a0, a1, a2 = l0[...], l1[...], l2[...]
    mx = jnp.maximum(jnp.maximum(a0, a1), a2)
    e0, e1, e2 = jnp.exp(a0 - mx), jnp.exp(a1 - mx), jnp.exp(a2 - mx)
    inv = 1.0 / (e0 + e1 + e2)
    y_ref[:, 0 * GROUP_W:1 * GROUP_W] = (o0[...] * (e0 * inv)).astype(y_ref.dtype)
    y_ref[:, 1 * GROUP_W:2 * GROUP_W] = (o1[...] * (e1 * inv)).astype(y_ref.dtype)
    y_ref[:, 2 * GROUP_W:3 * GROUP_W] = (o2[...] * (e2 * inv)).astype(y_ref.dtype)


def _attn_mix_call(os_, ls_, tm):
    M = os_[0].shape[0]
    spec = pl.BlockSpec((tm, GROUP_W), lambda i: (i, 0))
    return pl.pallas_call(
        _attn_mix_kernel,
        grid=(M // tm,),
        in_specs=[spec] * 6,
        out_specs=pl.BlockSpec((tm, ATTN_WIDTH), lambda i: (i, 0)),
        out_shape=jax.ShapeDtypeStruct((M, ATTN_WIDTH), BF16),
        compiler_params=_cparams(("arbitrary",)),
        name="attn_mix",
    )(*os_, *ls_)


QROWS = 16


def _attn_sample_kernel(qkv_ref, c0_ref, c1_ref, c2_ref, *rest, T, n_prev):
    y_ref, n0_ref, n1_ref, n2_ref, q16, npad, ysc = rest[n_prev:]
    q16[...] = jnp.zeros_like(q16)
    npad[...] = jnp.zeros_like(npad)
    q16[0:T, :] = qkv_ref[0, :, 0:ATTN_WIDTH]
    lane = lax.broadcasted_iota(jnp.int32, (HEAD_DIM, LANE), 1)
    items = []
    for g, (cref, nref) in enumerate(((c0_ref, n0_ref), (c1_ref, n1_ref), (c2_ref, n2_ref))):
        Lb = cref.shape[-1]
        dil = ATTN_GROUPS[g][1]
        npad[g, 0:T, 0:GROUP_W] = qkv_ref[0, :, C_AK + g * GROUP_W:C_AK + (g + 1) * GROUP_W]
        npad[g, 0:T, GROUP_W:2 * GROUP_W] = qkv_ref[0, :, C_AV + g * GROUP_W:C_AV + (g + 1) * GROUP_W]
        new_t = npad[g].T
        ncols = Lb + LANE
        t = lax.broadcasted_iota(jnp.int32, (QROWS, ncols), 0)
        c = lax.broadcasted_iota(jnp.int32, (QROWS, ncols), 1)
        if dil == 1:
            valid = ((c >= t) & (c < Lb)) | ((c >= Lb) & (c - Lb <= t) & (c - Lb < T))
        else:
            valid = ((c < Lb) & ((c & (dil - 1)) == t)) | (c == Lb + t)
        for h in range(ATTN_HPG):
            rk = slice(h * HEAD_DIM, (h + 1) * HEAD_DIM)
            rv = slice(GROUP_W + h * HEAD_DIM, GROUP_W + (h + 1) * HEAD_DIM)
            kc, vc = cref[0, 0, 0, h], cref[0, 0, 1, h]
            kt = jnp.concatenate([kc, new_t[rk, :]], axis=1).astype(BF16)
            vt = jnp.concatenate([vc, new_t[rv, :]], axis=1).astype(BF16)
            qh = q16[:, g * GROUP_W + h * HEAD_DIM:g * GROUP_W + (h + 1) * HEAD_DIM].astype(BF16)
            items.append((qh, kt, vt, valid))
            for kv, blk, rn in ((0, kc, rk), (1, vc, rv)):
                sh = pltpu.roll(blk, Lb - T, 1)
                newc = pltpu.roll(new_t[rn, :], LANE - T, 1)
                if Lb > LANE:
                    nref[0, 0, kv, h, :, 0:Lb - LANE] = sh[:, 0:Lb - LANE]
                nref[0, 0, kv, h, :, Lb - LANE:Lb] = jnp.where(lane >= LANE - T, newc, sh[:, Lb - LANE:Lb])
    ss = [jnp.where(valid, _dot(qh, kt) * ATTN_SCALE, NEG) for qh, kt, vt, valid in items]
    mxs = [jnp.max(s, axis=-1, keepdims=True) for s in ss]
    es = [jnp.exp(s - mx) for s, mx in zip(ss, mxs)]
    dens = [jnp.sum(e, axis=-1, keepdims=True) for e in es]
    pvs = [_dot_nt(e.astype(BF16), it[2]) for e, it in zip(es, items)]
    outs = [pv / den for pv, den in zip(pvs, dens)]
    lses = [mx + jnp.log(den) for mx, den in zip(mxs, dens)]
    for h in range(ATTN_HPG):
        l0, l1, l2 = lses[h], lses[ATTN_HPG + h], lses[2 * ATTN_HPG + h]
        mx = jnp.maximum(jnp.maximum(l0, l1), l2)
        es = [jnp.exp(l0 - mx), jnp.exp(l1 - mx), jnp.exp(l2 - mx)]
        inv = 1.0 / (es[0] + es[1] + es[2])
        for g in range(3):
            c0 = g * GROUP_W + h * HEAD_DIM
            ysc[:, c0:c0 + HEAD_DIM] = outs[g * ATTN_HPG + h] * (es[g] * inv)
    y_ref[0] = ysc[0:T, :].astype(y_ref.dtype)


def _attn_sample_call(qkv, caches_t, layer, prev):
    B, T, _ = qkv.shape
    depth = caches_t[0].shape[0]
    for g, (win, dil) in enumerate(ATTN_GROUPS):
        assert caches_t[g].shape[-1] == win and win // dil == BAND and (g == 0 or T <= dil), \
            "sample attention assumes full window caches"
    cspecs = [pl.BlockSpec((1, 1) + c.shape[2:], lambda b: (layer, b, 0, 0, 0, 0)) for c in caches_t]
    n_prev = 0 if prev is None else 3
    prev_args = [] if prev is None else list(prev)
    prev_specs = [pl.BlockSpec(memory_space=pl.ANY)] * n_prev
    aliases = {} if prev is None else {4 + g: 1 + g for g in range(3)}
    outs = pl.pallas_call(
        functools.partial(_attn_sample_kernel, T=T, n_prev=n_prev),
        grid=(B,),
        in_specs=[pl.BlockSpec((1, T, 3 * ATTN_WIDTH), lambda b: (b, 0, 0))] + cspecs + prev_specs,
        out_specs=[pl.BlockSpec((1, T, ATTN_WIDTH), lambda b: (b, 0, 0))] + cspecs,
        out_shape=[jax.ShapeDtypeStruct((B, T, ATTN_WIDTH), BF16)]
        + [jax.ShapeDtypeStruct(c.shape, F32) for c in caches_t],
        scratch_shapes=[pltpu.VMEM((QROWS, ATTN_WIDTH), F32),
                        pltpu.VMEM((3, LANE, 2 * GROUP_W), F32),
                        pltpu.VMEM((QROWS, ATTN_WIDTH), F32)],
        input_output_aliases=aliases,
        compiler_params=_cparams(("arbitrary",)),
        name="attn_sample",
    )(qkv, *caches_t, *prev_args)
    return outs[0], outs[1:]


def _mlstm_kernel(mqk_ref, mv_ref, mi_ref, mf_ref, mo_ref, cst_ref, wc_ref, bc_ref, c0_ref, n0_ref, m0_ref,
                  g_ref, tri_ref, y_ref, c_ref, n_ref, m_ref, cso_ref,
                  xbuf, vbuf, cs, ns, ms, tpad, kwp, *, T, Lp):
    c = pl.program_id(1)
    last = pl.num_programs(1) - 1
    W = MLSTM_WIDTH

    @pl.when(c == 0)
    def _():
        xbuf[...] = jnp.zeros_like(xbuf)
        vbuf[...] = jnp.zeros_like(vbuf)
        tpad[...] = jnp.zeros_like(tpad)
        kwp[...] = jnp.zeros_like(kwp)
        xbuf[5:8, :] = cst_ref[0]
        cs[...] = c0_ref[0]
        ns[...] = n0_ref[0]
        ms[...] = m0_ref[0]

    xbuf[8:8 + T, :] = mqk_ref[0]
    vbuf[0:T, :] = mv_ref[0]
    w = wc_ref[...]
    y = (bc_ref[...] + xbuf[8:8 + Lp, :] * w[3:4] + xbuf[7:7 + Lp, :] * w[2:3]
         + xbuf[6:6 + Lp, :] * w[1:2] + xbuf[5:5 + Lp, :] * w[0:1])
    tail = xbuf[8 + T - 3:8 + T, :]
    xbuf[5:8, :] = tail

    @pl.when(c == last)
    def _():
        cso_ref[0] = tail

    qk = y * _sigmoid(y)

    rowid = lax.broadcasted_iota(jnp.int32, (Lp, LANE), 0)
    real = rowid < T
    tpad[0:T, :] = mi_ref[0]
    ig = jnp.where(real, tpad[0:Lp, :], NEG)
    tpad[0:T, :] = mf_ref[0]
    lf = jnp.where(real, _log_sigmoid(tpad[0:Lp, :]), 0.0)
    tri = tri_ref[...]
    F = _dot_sel(tri, lf)
    inter = F + ms[...]
    tpad[0:Lp, :] = F
    FT = tpad[...].T
    tpad[0:Lp, :] = ig
    IT = tpad[...].T
    causal = tri > 0.5
    v_all = vbuf[0:Lp, :]
    heads = range(MLSTM_HEADS)
    hsl = [slice(h * HEAD_DIM, (h + 1) * HEAD_DIM) for h in heads]
    n_all = ns[...]
    m_row = ms[...]
    lane_row = lax.broadcasted_iota(jnp.int32, (1, LANE), 1)

    qs = [qk[:, hsl[h]] for h in heads]
    ks = [qk[:, W + h * HEAD_DIM:W + (h + 1) * HEAD_DIM] * (HEAD_DIM ** -0.5) for h in heads]
    qbs = [q.astype(BF16) for q in qs]
    vbs = [v_all[:, hsl[h]].astype(BF16) for h in heads]
    sqk = [_dot_nt(qbs[h], ks[h].astype(BF16)) for h in heads]
    qcs = [_dot(qbs[h], cs[h].astype(BF16)) for h in heads]

    fcols = [F[:, h:h + 1] for h in heads]
    icols = [ig[:, h:h + 1] for h in heads]
    inters = [inter[:, h:h + 1] for h in heads]
    logws = [jnp.where(causal, fcols[h] - FT[h:h + 1, 0:Lp] + IT[h:h + 1, 0:Lp], NEG) for h in heads]
    rmax = [jnp.max(logws[h], axis=-1, keepdims=True) for h in heads]
    mts = [jnp.maximum(rmax[h], inters[h]) for h in heads]
    As = [jnp.exp(logws[h] - mts[h]) * sqk[h] for h in heads]
    gqs = [jnp.exp(inters[h] - mts[h]) for h in heads]
    nrows = [n_all[h:h + 1, :] for h in heads]
    asum = [jnp.sum(As[h], axis=-1, keepdims=True) for h in heads]
    qn = [jnp.sum(qs[h] * nrows[h], axis=-1, keepdims=True) for h in heads]
    dens = [asum[h] + gqs[h] * qn[h] for h in heads]
    a_bf = [As[h].astype(BF16) for h in heads]
    avs = [_dot(a_bf[h], vbs[h]) for h in heads]
    mLs = [mts[h][Lp - 1:Lp, :] for h in heads]
    flast = [F[Lp - 1:Lp, h:h + 1] for h in heads]
    wLs = [jnp.exp(flast[h] - fcols[h] + icols[h] - mLs[h]) for h in heads]
    gls = [jnp.exp(flast[h] + m_row[0:1, h:h + 1] - mLs[h]) for h in heads]
    kws = [ks[h] * wLs[h] for h in heads]
    for h in heads:
        kwp[0:Lp, hsl[h]] = kws[h]
    ksum = [jnp.sum(kws[h], axis=0, keepdims=True) for h in heads]
    for h in heads:
        ns[h:h + 1, :] = gls[h] * nrows[h] + ksum[h]
    m_new = m_row
    for h in heads:
        m_new = jnp.where(lane_row == h, mLs[h], m_new)
    ms[...] = m_new

    kwt = kwp[...].T
    upds = [_dot(kwt[hsl[h], 0:Lp].astype(BF16), vbs[h]) for h in heads]

    nums = [avs[h] + gqs[h] * qcs[h] for h in heads]
    hhs = [nums[h] / jnp.maximum(jnp.abs(dens[h]), jnp.exp(-mts[h])) for h in heads]
    msq = [jnp.mean(hhs[h] * hhs[h], axis=-1, keepdims=True) for h in heads]
    hns = [hhs[h] * lax.rsqrt(msq[h] + EPS) * g_ref[0:1, hsl[h]] for h in heads]
    ogs = [_sigmoid(mo_ref[0, :, hsl[h]]) for h in heads]
    for h in heads:
        yo = hns[h] * ogs[h] if Lp == T else hns[h][0:T] * ogs[h]
        y_ref[0, :, hsl[h]] = yo.astype(y_ref.dtype)
    for h in heads:
        cs[h] = gls[h] * cs[h] + upds[h]

    @pl.when(c == last)
    def _():
        c_ref[0] = cs[...]
        n_ref[0] = ns[...]
        m_ref[0] = ms[...]


def _mlstm_call(proj, conv_state, w_conv, b_conv, c0, n0, m0, g_mlstm, chunk):
    B, S, N = proj.shape
    T = chunk
    nc = S // T
    Lp = max(16, T)
    H = MLSTM_HEADS
    m0p = jnp.pad(m0, ((0, 0), (0, LANE - H))).reshape(B, 1, LANE)
    tri = jnp.tril(jnp.ones((Lp, Lp), F32))

    def col(width, off):
        blk = off // width
        return pl.BlockSpec((1, T, width), lambda b, c: (b, c, blk))

    def const(shape):
        nd = len(shape)
        return pl.BlockSpec(shape, lambda b, c: (0,) * nd)

    def per_b(shape):
        nd = len(shape)
        return pl.BlockSpec((1,) + shape, lambda b, c: (b,) + (0,) * nd)

    outs = pl.pallas_call(
        functools.partial(_mlstm_kernel, T=T, Lp=Lp),
        grid=(B, nc),
        in_specs=[col(2 * MLSTM_WIDTH, C_MQK), col(MLSTM_WIDTH, C_MV), col(LANE, C_MI), col(LANE, C_MF),
                  col(MLSTM_WIDTH, C_MO), per_b((MLSTM_CONV - 1, 2 * MLSTM_WIDTH)),
                  const((MLSTM_CONV, 2 * MLSTM_WIDTH)), const((1, 2 * MLSTM_WIDTH)),
                  per_b((H, HEAD_DIM, HEAD_DIM)), per_b((H, HEAD_DIM)), per_b((1, LANE)),
                  const((1, MLSTM_WIDTH)), const((Lp, Lp))],
        out_specs=[pl.BlockSpec((1, T, MLSTM_WIDTH), lambda b, c: (b, c, 0)),
                   per_b((H, HEAD_DIM, HEAD_DIM)), per_b((H, HEAD_DIM)), per_b((1, LANE)),
                   per_b((MLSTM_CONV - 1, 2 * MLSTM_WIDTH))],
        out_shape=[jax.ShapeDtypeStruct((B, S, MLSTM_WIDTH), BF16),
                   jax.ShapeDtypeStruct((B, H, HEAD_DIM, HEAD_DIM), F32),
                   jax.ShapeDtypeStruct((B, H, HEAD_DIM), F32),
                   jax.ShapeDtypeStruct((B, 1, LANE), F32),
                   jax.ShapeDtypeStruct((B, MLSTM_CONV - 1, 2 * MLSTM_WIDTH), F32)],
        scratch_shapes=[pltpu.VMEM((8 + Lp, 2 * MLSTM_WIDTH), F32),
                        pltpu.VMEM((Lp, MLSTM_WIDTH), F32),
                        pltpu.VMEM((H, HEAD_DIM, HEAD_DIM), F32),
                        pltpu.VMEM((H, HEAD_DIM), F32),
                        pltpu.VMEM((1, LANE), F32),
                        pltpu.VMEM((LANE, LANE), F32),
                        pltpu.VMEM((LANE, MLSTM_WIDTH), F32)],
        compiler_params=_cparams(("arbitrary", "arbitrary")),
        name="mlstm",
    )(proj, proj, proj, proj, proj, conv_state, w_conv, b_conv.reshape(1, -1), c0, n0, m0p,
      g_mlstm.reshape(1, -1), tri)
    y, C, n, m, cso = outs
    return y, C, n, m[:, 0, :H], cso


def _gla_kernel(gqk_ref, gv_ref, gg_ref, ga_ref, wa_ref, wat_ref, s0_ref, g_ref, tri_ref, bd_ref, ee_ref,
                y_ref, s_ref, qkp, vp, gap, sbd, osc, *, T, Lp):
    c = pl.program_id(1)
    last = pl.num_programs(1) - 1
    KW, VW = GLA_K_WIDTH, GLA_V_WIDTH

    @pl.when(c == 0)
    def _():
        qkp[...] = jnp.zeros_like(qkp)
        vp[...] = jnp.zeros_like(vp)
        gap[...] = jnp.zeros_like(gap)
        sbd[...] = jnp.zeros_like(sbd)
        osc[...] = jnp.zeros_like(osc)
        for h in range(GLA_HEADS):
            sbd[h * GLA_DK:(h + 1) * GLA_DK, h * GLA_DV:(h + 1) * GLA_DV] = s0_ref[0, h]

    qkp[0:T, :] = gqk_ref[0]
    vp[0:T, :] = gv_ref[0]
    lane = lax.broadcasted_iota(jnp.int32, (T, LANE), 1)
    gap[0:T, :] = jnp.where(lane == GLA_RANK, 1.0, ga_ref[0])

    tri = tri_ref[...]
    ga = gap[0:Lp, :]
    rowid = lax.broadcasted_iota(jnp.int32, (Lp, KW), 0)
    wa_hi, wa_lo, _ = _split3(wa_ref[...])
    wat_hi, wat_lo, _ = _split3(wat_ref[...])
    ga_b = ga.astype(BF16)
    gat_b = gap[...].T.astype(BF16)
    la = _dot(ga_b, wa_hi) + _dot(ga_b, wa_lo)
    la = jnp.where(rowid < T, _log_sigmoid(la) / GLA_TAU, 0.0)
    bc = _dot_sel(tri[0:Lp, 0:Lp], la)
    colid = lax.broadcasted_iota(jnp.int32, (KW, LANE), 1)
    lat = _dot(wat_hi, gat_b) + _dot(wat_lo, gat_b)
    lat = jnp.where(colid < T, _log_sigmoid(lat) / GLA_TAU, 0.0)
    bct = _dot_sel_nt(lat, tri)
    blcol = bct[:, LANE - 1:LANE]
    blrow = bc[Lp - 1:Lp, :]
    kt = qkp[...].T[KW:2 * KW, :]
    q = qkp[0:Lp, 0:KW] * (GLA_DK ** -0.5)
    k = qkp[0:Lp, KW:2 * KW]
    v = vp[0:Lp, :]
    vfull = vp[...]
    bd = bd_ref[...]
    safe = jnp.min(blrow) >= -GLA_SAFE_DECAY

    @pl.when(safe)
    def _():
        qb = (q * jnp.exp(bc)).astype(BF16)
        kb = (k * jnp.exp(-bc)).astype(BF16)
        o_inter = _dot(qb, sbd[...].astype(BF16))
        causal = tri[0:Lp, 0:Lp] > 0.5
        vb = v.astype(BF16)
        heads = range(GLA_HEADS)
        scores = [_dot_nt(qb[:, h * GLA_DK:(h + 1) * GLA_DK], kb[:, h * GLA_DK:(h + 1) * GLA_DK]) for h in heads]
        a_bf = [jnp.where(causal, s, 0.0).astype(BF16) for s in scores]
        ovs = [_dot(a_bf[h], vb[:, h * GLA_DV:(h + 1) * GLA_DV]) for h in heads]
        osc[0:Lp, :] = o_inter + jnp.concatenate(ovs, axis=1)
        klt = (kt * jnp.exp(blcol - bct)).astype(BF16)
        sbd[...] = bd * (jnp.exp(blcol) * sbd[...] + _dot(klt, vfull.astype(BF16)))

    @pl.when(jnp.logical_not(safe))
    def _():
        srow = lax.broadcasted_iota(jnp.int32, (LANE, LANE), 0)

        def body(t, carry):
            sel = (srow == t).astype(F32)
            lac = jnp.dot(lat, sel, precision=HI, preferred_element_type=F32)
            kc = jnp.dot(kt, sel, precision=HI, preferred_element_type=F32)
            dec = jnp.concatenate([jnp.exp(lac)] * (VW // LANE), axis=1)
            kcw = jnp.concatenate([kc] * (VW // LANE), axis=1)
            vrow = vp[pl.ds(t, 1), :]
            snew = bd * (dec * sbd[...] + kcw * vrow)
            sbd[...] = snew
            qrow = jnp.broadcast_to(qkp[pl.ds(t, 1), 0:KW] * (GLA_DK ** -0.5), (8, KW))
            orow = jnp.dot(qrow, snew, precision=HI, preferred_element_type=F32)
            osc[pl.ds(t, 1), :] = orow[0:1]
            return carry

        lax.fori_loop(0, T, body, 0)

    o = osc[0:Lp, :]
    ms = _dot_sel_rhs(o * o, ee_ref[...])
    og = o * lax.rsqrt(ms + EPS) * g_ref[...]
    gg = gg_ref[0]
    yo = (og if Lp == T else og[0:T]) * (gg * _sigmoid(gg))
    y_ref[0] = yo.astype(y_ref.dtype)

    @pl.when(c == last)
    def _():
        for h in range(GLA_HEADS):
            s_ref[0, h] = sbd[h * GLA_DK:(h + 1) * GLA_DK, h * GLA_DV:(h + 1) * GLA_DV]


def _gla_call(proj, w_a2, b_a2, s0, g_gla, chunk):
    B, S, N = proj.shape
    T = chunk
    nc = S // T
    Lp = max(16, T)
    H, KW, VW = GLA_HEADS, GLA_K_WIDTH, GLA_V_WIDTH
    wa = jnp.zeros((LANE, KW), F32).at[:GLA_RANK].set(w_a2).at[GLA_RANK].set(b_a2)
    tri = jnp.tril(jnp.ones((LANE, LANE), F32))
    hk = jnp.arange(KW) // GLA_DK
    hv = jnp.arange(VW) // GLA_DV
    bd = (hk[:, None] == hv[None, :]).astype(F32)
    ee = (hv[:, None] == hv[None, :]).astype(F32) / GLA_DV

    def col(width, off):
        blk = off // width
        return pl.BlockSpec((1, T, width), lambda b, c: (b, c, blk))

    def const(shape):
        nd = len(shape)
        return pl.BlockSpec(shape, lambda b, c: (0,) * nd)

    y, s = pl.pallas_call(
        functools.partial(_gla_kernel, T=T, Lp=Lp),
        grid=(B, nc),
        in_specs=[col(2 * KW, C_GQK), col(VW, C_GV), col(VW, C_GG), col(LANE, C_GA),
                  const((LANE, KW)), const((KW, LANE)),
                  pl.BlockSpec((1, H, GLA_DK, GLA_DV), lambda b, c: (b, 0, 0, 0)),
                  const((1, VW)), const((LANE, LANE)), const((KW, VW)), const((VW, VW))],
        out_specs=[pl.BlockSpec((1, T, VW), lambda b, c: (b, c, 0)),
                   pl.BlockSpec((1, H, GLA_DK, GLA_DV), lambda b, c: (b, 0, 0, 0))],
        out_shape=[jax.ShapeDtypeStruct((B, S, VW), BF16),
                   jax.ShapeDtypeStruct((B, H, GLA_DK, GLA_DV), F32)],
        scratch_shapes=[pltpu.VMEM((LANE, 2 * KW), F32),
                        pltpu.VMEM((LANE, VW), F32),
                        pltpu.VMEM((LANE, LANE), F32),
                        pltpu.VMEM((KW, VW), F32),
                        pltpu.VMEM((LANE, VW), F32)],
        compiler_params=_cparams(("arbitrary", "arbitrary")),
        name="gla",
    )(proj, proj, proj, proj, wa, wa.T, s0, g_gla.reshape(1, -1), tri, bd, ee)
    return y, s


def _outproj_kernel(ya_ref, ym_ref, yg_ref, w_ref, x_ref, gt_ref, g2_ref, sc_ref, sh_ref, xo_ref, h2_ref):
    a, b = ATTN_WIDTH, ATTN_WIDTH + MLSTM_WIDTH
    acc = _dot(ya_ref[0], w_ref[0:a, :])
    acc = acc + _dot(ym_ref[0], w_ref[a:b, :])
    acc = acc + _dot(yg_ref[0], w_ref[b:, :])
    x = x_ref[0] + gt_ref[0] * acc
    xo_ref[0] = x
    y = x * lax.rsqrt(jnp.mean(x * x, axis=-1, keepdims=True) + EPS) * g2_ref[...]
    h2_ref[0] = (y * (1.0 + sc_ref[0]) + sh_ref[0]).astype(h2_ref.dtype)


def _outproj_call(ya, ym, yg, w_out, x, gt, g2, sc, sh, tm):
    G, R, D = x.shape
    rr = gt.shape[1]
    mod_spec = (pl.BlockSpec((1, 1, D), lambda b, i: (b, 0, 0)) if rr == 1
                else pl.BlockSpec((1, tm, D), lambda b, i: (b, i, 0)))

    def act(width):
        return pl.BlockSpec((1, tm, width), lambda b, i: (b, i, 0))

    return pl.pallas_call(
        _outproj_kernel,
        grid=(G, R // tm),
        in_specs=[act(ATTN_WIDTH), act(MLSTM_WIDTH), act(GLA_V_WIDTH),
                  pl.BlockSpec(w_out.shape, lambda b, i: (0, 0)),
                  act(D), mod_spec, pl.BlockSpec((1, D), lambda b, i: (0, 0)), mod_spec, mod_spec],
        out_specs=[act(D), act(D)],
        out_shape=[jax.ShapeDtypeStruct((G, R, D), F32), jax.ShapeDtypeStruct((G, R, D), BF16)],
        compiler_params=_cparams(("arbitrary", "arbitrary")),
        name="outproj",
    )(ya, ym, yg, w_out, x, gt, g2.reshape(1, D), sc, sh)


def _ffn_kernel(h_ref, wg_ref, wu_ref, wc_ref, bc_ref, wo_ref, x_ref, gt_ref, init_ref,
                xo_ref, st_ref, ubuf, cbuf, *, tm, u, R, tiles_per_seq):
    m = pl.program_id(1)
    f = pl.program_id(2)
    nf = pl.num_programs(2)
    h = h_ref[0]

    @pl.when(m % tiles_per_seq == 0)
    def _():
        ubuf[0:R, :] = init_ref[0]

    @pl.when(m % tiles_per_seq != 0)
    def _():
        ubuf[0:R, :] = cbuf[f]

    tf = wg_ref.shape[1]
    part = None
    for c0 in range(0, tf, FFN_SUB):
        cs_ = slice(c0, min(c0 + FFN_SUB, tf))
        ug = _dot(h, wg_ref[:, cs_])
        uu = _dot(h, wu_ref[:, cs_])
        ubuf[R:R + tm, cs_] = ug
        w = wc_ref[:, cs_]
        gate = (bc_ref[:, cs_] + ug * w[2:3] + ubuf[R - u:R - u + tm, cs_] * w[1:2]
                + ubuf[R - 2 * u:R - 2 * u + tm, cs_] * w[0:1])
        act = (gate * _sigmoid(gate) * uu).astype(BF16)
        p = _dot(act, wo_ref[cs_, :])
        part = p if part is None else part + p
    tail = ubuf[tm:tm + R, :]
    cbuf[f] = tail
    st_ref[0, 0] = tail

    @pl.when(f == 0)
    def _():
        xo_ref[0] = part

    @pl.when(f != 0)
    def _():
        xo_ref[0] = xo_ref[0] + part

    @pl.when(f == nf - 1)
    def _():
        xo_ref[0] = x_ref[0] + gt_ref[0] * xo_ref[0]


def _ffn_call(h2, w_gu, w_conv, b_conv, w_o, x, gt, init, tm, tf, u):
    G, rows, D = x.shape
    F = w_gu.shape[1] // 2
    R = max(8, 2 * u)
    nm, nf = rows // tm, F // tf
    rr = gt.shape[1]
    mod_spec = (pl.BlockSpec((1, 1, D), lambda b, i, f: (b, 0, 0)) if rr == 1
                else pl.BlockSpec((1, tm, D), lambda b, i, f: (b, i, 0)))
    xo, st = pl.pallas_call(
        functools.partial(_ffn_kernel, tm=tm, u=u, R=R, tiles_per_seq=nm),
        grid=(G, nm, nf),
        in_specs=[pl.BlockSpec((1, tm, D), lambda b, i, f: (b, i, 0)),
                  pl.BlockSpec((D, tf), lambda b, i, f: (0, f)),
                  pl.BlockSpec((D, tf), lambda b, i, f: (0, nf + f)),
                  pl.BlockSpec((FFN_CONV, tf), lambda b, i, f: (0, f)),
                  pl.BlockSpec((1, tf), lambda b, i, f: (0, f)),
                  pl.BlockSpec((tf, D), lambda b, i, f: (f, 0)),
                  pl.BlockSpec((1, tm, D), lambda b, i, f: (b, i, 0)),
                  mod_spec,
                  pl.BlockSpec((1, R, tf), lambda b, i, f: (b, 0, f))],
        out_specs=[pl.BlockSpec((1, tm, D), lambda b, i, f: (b, i, 0)),
                   pl.BlockSpec((1, 1, R, tf), lambda b, i, f: (b, i, 0, f))],
        out_shape=[jax.ShapeDtypeStruct((G, rows, D), F32),
                   jax.ShapeDtypeStruct((G, nm, R, F), F32)],
        scratch_shapes=[pltpu.VMEM((R + tm, tf), F32), pltpu.VMEM((nf, R, tf), F32)],
        compiler_params=_cparams(("arbitrary", "arbitrary", "arbitrary")),
        name="ffn",
    )(h2, w_gu, w_gu, w_conv, b_conv.reshape(1, F), w_o, x, gt, init)
    return xo, st[:, nm - 1]


def _pack_w_kernel(w_ref, o_ref):
    x = w_ref[0]
    rows = x.shape[0]
    n_in = x.shape[1]
    attn = 3 * ATTN_WIDTH
    gates = attn + 3 * MLSTM_WIDTH
    tail = gates + 2 * MLSTM_HEADS
    o_ref[0, :, 0:attn] = x[:, 0:attn].astype(BF16)
    lane = lax.broadcasted_iota(jnp.int32, (rows, LANE), 1)
    gwin = x[:, gates:gates + LANE]
    o_ref[0, :, C_MI:C_MI + LANE] = jnp.where(lane < MLSTM_HEADS, gwin, 0.0).astype(BF16)
    o_ref[0, :, C_MF:C_MF + LANE] = jnp.where(lane < MLSTM_HEADS, pltpu.roll(gwin, LANE - MLSTM_HEADS, 1),
                                              0.0).astype(BF16)
    o_ref[0, :, C_MQK:C_MQK + 3 * MLSTM_WIDTH] = x[:, attn:gates].astype(BF16)
    o_ref[0, :, C_GA:N_PACK] = jnp.zeros((rows, N_PACK - C_GA), BF16)
    o_ref[0, :, C_MO:C_MO + n_in - tail] = x[:, tail:n_in].astype(BF16)


def _pack_w_call(w_in):
    depth, d, n_in = w_in.shape
    tk = 256
    return pl.pallas_call(
        _pack_w_kernel,
        grid=(depth, d // tk),
        in_specs=[pl.BlockSpec((1, tk, n_in), lambda l, i: (l, i, 0))],
        out_specs=pl.BlockSpec((1, tk, N_PACK), lambda l, i: (l, i, 0)),
        out_shape=jax.ShapeDtypeStruct((depth, d, N_PACK), BF16),
        compiler_params=_cparams(("arbitrary", "arbitrary")),
        name="pack_w_in",
    )(w_in)


def _pack_in_proj(w_in, b_in):
    def split(a):
        out, off = [], 0
        for s in IN_SPLITS:
            out.append(a[..., off:off + s])
            off += s
        return out

    def pad(a, n):
        return jnp.pad(a, [(0, 0)] * (a.ndim - 1) + [(0, n - a.shape[-1])])

    def pack(a):
        aq, ak, av, mqk, mv, mi, mf, mo, gq, gk, gv, gg, ga = split(a)
        return jnp.concatenate([aq, ak, av, pad(mi, LANE), pad(mf, LANE), mqk, mv, mo, gq, gk, gv, gg,
                                pad(ga, LANE)], axis=-1)

    return _pack_w_call(w_in), pack(b_in)


def _rope_tables(pos):
    half = HEAD_DIM // 2
    inv_freq = jnp.power(ROPE_THETA, -jnp.arange(half, dtype=F32) / half)
    ang = pos.astype(F32)[:, None] * inv_freq[None, :]
    cos, sin = jnp.cos(ang), jnp.sin(ang)
    zero = jnp.zeros_like(sin)
    reps = LANE // HEAD_DIM
    return (jnp.tile(jnp.concatenate([cos, cos], -1), (1, reps)),
            jnp.tile(jnp.concatenate([-sin, zero], -1), (1, reps)),
            jnp.tile(jnp.concatenate([zero, sin], -1), (1, reps)))


def _pick_tile(n, pref):
    t = math.gcd(n, pref)
    return t


def kernel(x_prompt, x_sample, c_prompt, c_sample, cache_win0_kv, cache_win1_kv, cache_win2_kv, state_mlstm_C, state_mlstm_n, state_mlstm_m, state_mlstm_conv, state_gla_S, state_ffn_conv, w_ada, b_ada, g_norm1, g_norm2, w_in, b_in, w_mconv, b_mconv, g_mlstm, w_gla_a2, b_gla_a2, g_gla, w_out, w_ff_in, w_fconv, b_fconv, w_ff_out, g_final):
    B, S, D = x_prompt.shape
    Bs, Ts, _ = x_sample.shape
    depth = w_ada.shape[0]
    d_ff = w_fconv.shape[-1]
    caches = (cache_win0_kv, cache_win1_kv, cache_win2_kv)
    Ms = Bs * Ts

    n_c = B + Bs
    rows_c = -(-n_c // 8) * 8
    c_all = jnp.pad(jnp.concatenate([c_prompt, c_sample], axis=0), ((0, rows_c - n_c), (0, 0)))
    mod = _ada_call(c_all, w_ada, b_ada).reshape(depth, rows_c, 6, D)

    w_in_p, b_in_p = _pack_in_proj(w_in, b_in)
    w_out_b = w_out.astype(BF16)
    w_gu = w_ff_in.astype(BF16)
    w_o = w_ff_out.astype(BF16)

    rope_p = _rope_tables(jnp.arange(S))
    rope_s = _rope_tables(PAST_LEN + jnp.repeat(jnp.arange(Ts), Bs))

    tm_p = _pick_tile(S, 512)
    tn = 1024
    tf = _pick_tile(d_ff, 512)

    xp = x_prompt
    xs = x_sample.transpose(1, 0, 2).reshape(1, Ms, D)
    zeros_p = {
        'mconv': jnp.zeros((B, MLSTM_CONV - 1, 2 * MLSTM_WIDTH), F32),
        'C': jnp.zeros((B, MLSTM_HEADS, HEAD_DIM, HEAD_DIM), F32),
        'n': jnp.zeros((B, MLSTM_HEADS, HEAD_DIM), F32),
        'm': jnp.zeros((B, MLSTM_HEADS), F32),
        'S': jnp.zeros((B, GLA_HEADS, GLA_DK, GLA_DV), F32),
        'fconv': jnp.zeros((B, 8, d_ff), F32),
    }
    names = ('win0', 'win1', 'win2', 'C', 'n', 'm', 'mconv', 'S', 'fconv')
    col_p = {k: [] for k in names}
    col_s = {k: [] for k in names}
    caches_t = [c.transpose(0, 1, 3, 4, 5, 2) for c in caches]
    new_caches = None

    for l in range(depth):
        mp = mod[l, :B]
        ms_ = jnp.tile(mod[l, B:B + Bs], (Ts, 1, 1))

        def mods_p(i):
            return mp[:, i:i + 1, :]

        def mods_s(i):
            return ms_[None, :, i, :]

        h = _normmod_call(xp, g_norm1[l], mods_p(1), mods_p(0), BF16, tm_p)
        proj = _inproj_call(h.reshape(B * S, D), w_in_p[l], b_in_p[l][None], *rope_p, tm_p, tn)
        proj = proj.reshape(B, S, N_PACK)
        os_, ls_ = [], []
        for g, (win, dil) in enumerate(ATTN_GROUPS):
            o, lse = _attn_prompt_call(proj, g, dil)
            os_.append(o.reshape(B * S, GROUP_W))
            ls_.append(lse.reshape(B * S, GROUP_W))
            col_p[f'win{g}'].append(_win_extract_call(proj, g, min(win, S)))
        ya = _attn_mix_call(os_, ls_, tm_p).reshape(B, S, ATTN_WIDTH)
        ym, C, n, m, cso = _mlstm_call(proj, zeros_p['mconv'], w_mconv[l], b_mconv[l], zeros_p['C'],
                                        zeros_p['n'], zeros_p['m'], g_mlstm[l], math.gcd(S, MLSTM_CHUNK))
        yg, Sg = _gla_call(proj, w_gla_a2[l], b_gla_a2[l], zeros_p['S'], g_gla[l], math.gcd(S, GLA_CHUNK))
        xp, h2 = _outproj_call(ya, ym, yg, w_out_b[l], xp, mods_p(2), g_norm2[l], mods_p(4), mods_p(3), tm_p)
        xp, fst = _ffn_call(h2, w_gu[l], w_fconv[l], b_fconv[l], w_o[l], xp, mods_p(5),
                            zeros_p['fconv'], tm_p, tf, 1)
        for k_, v_ in (('C', C), ('n', n), ('m', m), ('mconv', cso), ('S', Sg), ('fconv', fst[:, 6:8])):
            col_p[k_].append(v_)

        h = _normmod_call(xs, g_norm1[l], mods_s(1), mods_s(0), BF16, Ms)
        proj = _inproj_call(h.reshape(Ms, D), w_in_p[l], b_in_p[l][None], *rope_s, Ms, tn)
        proj_b = proj.reshape(Ts, Bs, N_PACK).transpose(1, 0, 2)
        ya, new_caches = _attn_sample_call(proj_b[:, :, :3 * ATTN_WIDTH], caches_t, l, new_caches)
        ym, C, n, m, cso = _mlstm_call(proj_b, state_mlstm_conv[l], w_mconv[l], b_mconv[l], state_mlstm_C[l],
                                        state_mlstm_n[l], state_mlstm_m[l], g_mlstm[l], Ts)
        yg, Sg = _gla_call(proj_b, w_gla_a2[l], b_gla_a2[l], state_gla_S[l], g_gla[l], Ts)

        def tmaj(a):
            return a.transpose(1, 0, 2).reshape(1, Ms, a.shape[-1])

        xs, h2 = _outproj_call(tmaj(ya), tmaj(ym), tmaj(yg), w_out_b[l], xs, mods_s(2), g_norm2[l],
                               mods_s(4), mods_s(3), Ms)
        n_st = (FFN_CONV - 1) * Bs
        r_st = max(8, n_st)
        init = state_ffn_conv[l].transpose(1, 0, 2).reshape(1, n_st, d_ff)
        init = jnp.pad(init, ((0, 0), (r_st - n_st, 0), (0, 0)))
        xs, fst = _ffn_call(h2, w_gu[l], w_fconv[l], b_fconv[l], w_o[l], xs, mods_s(5), init, Ms, tf, Bs)
        fst = fst[:, r_st - n_st:].reshape(FFN_CONV - 1, Bs, d_ff).transpose(1, 0, 2)
        for k_, v_ in (('C', C), ('n', n), ('m', m), ('mconv', cso), ('S', Sg), ('fconv', fst)):
            col_s[k_].append(v_)

    y_prompt = _normmod_call(xp, g_final, jnp.zeros((B, 1, D), F32), jnp.zeros((B, 1, D), F32), F32, tm_p)
    y_sample = _normmod_call(xs, g_final, jnp.zeros((1, 1, D), F32), jnp.zeros((1, 1, D), F32), F32, Ms)
    y_sample = y_sample.reshape(Ts, Bs, D).transpose(1, 0, 2)
    sp = {k: jnp.stack(v, axis=0) for k, v in col_p.items()}
    ss = {k: jnp.stack(v, axis=0) for k, v in col_s.items() if v}
    for g in range(3):
        ss[f'win{g}'] = new_caches[g].transpose(0, 1, 5, 2, 3, 4)
    return (y_prompt, y_sample, sp['win0'], ss['win0'], sp['win1'], ss['win1'], sp['win2'], ss['win2'],
            sp['C'], ss['C'], sp['n'], ss['n'], sp['m'], ss['m'], sp['mconv'], ss['mconv'],
            sp['S'], ss['S'], sp['fconv'], ss['fconv'])
```

```python
import functools
import math

import jax
import jax.numpy as jnp
from jax import lax
from jax.experimental import pallas as pl
from jax.experimental.pallas import tpu as pltpu

F32 = jnp.float32
BF16 = jnp.bfloat16
HI = lax.Precision.HIGHEST

HEAD_DIM = 64
ATTN_GROUPS = ((128, 1), (512, 4), (2048, 16))
ATTN_HPG = 4
ATTN_WIDTH = 3 * ATTN_HPG * HEAD_DIM
GROUP_W = ATTN_HPG * HEAD_DIM
ATTN_SCALE = HEAD_DIM ** -0.5
ROPE_THETA = 10000.0
BAND = 128
RES_UNROLL = 4
MLSTM_HEADS = 10
MLSTM_WIDTH = MLSTM_HEADS * HEAD_DIM
MLSTM_CONV = 4
MLSTM_CHUNK = 128
GLA_HEADS = 10
GLA_DK = 32
GLA_DV = 64
GLA_K_WIDTH = GLA_HEADS * GLA_DK
GLA_V_WIDTH = GLA_HEADS * GLA_DV
GLA_RANK = 16
GLA_TAU = 16.0
GLA_CHUNK = 128
GLA_SAFE_DECAY = 60.0
FFN_CONV = 3
FFN_SUB = 512
EPS = 1e-6
PAST_LEN = 8192
NEG = -1e30

LANE = 128
VMEM_LIMIT = 56 * 1024 * 1024

IN_SPLITS = (ATTN_WIDTH, ATTN_WIDTH, ATTN_WIDTH, 2 * MLSTM_WIDTH, MLSTM_WIDTH, MLSTM_HEADS, MLSTM_HEADS,
             MLSTM_WIDTH, GLA_K_WIDTH, GLA_K_WIDTH, GLA_V_WIDTH, GLA_V_WIDTH, GLA_RANK)

C_AQ, C_AK, C_AV = 0, 768, 1536
C_MI, C_MF = 2304, 2432
C_MQK = 2560
C_MV, C_MO = 3840, 4480
C_GQK, C_GV, C_GG = 5120, 5760, 6400
C_GA = 7040
N_PACK = 7168
ROPE_COLS = 2 * ATTN_WIDTH


def _cparams(sem):
    return pltpu.CompilerParams(dimension_semantics=sem, vmem_limit_bytes=VMEM_LIMIT)


def _sigmoid(x):
    return 1.0 / (1.0 + jnp.exp(-x))


def _log_sigmoid(x):
    return jnp.minimum(x, 0.0) - jnp.log(1.0 + jnp.exp(-jnp.abs(x)))


def _dot(a, b):
    return jnp.dot(a, b, preferred_element_type=F32)


def _dot_nt(a, b):
    return lax.dot_general(a, b, (((1,), (1,)), ((), ())), preferred_element_type=F32)


def _split3(x):
    hi = x.astype(BF16)
    r1 = x - hi.astype(F32)
    mid = r1.astype(BF16)
    lo = (r1 - mid.astype(F32)).astype(BF16)
    return hi, mid, lo


def _dot_sel(sel, x, parts=3):
    sb = sel.astype(BF16)
    return sum(_dot(sb, p) for p in _split3(x)[:parts])


def _dot_sel_rhs(x, sel, parts=2):
    sb = sel.astype(BF16)
    return sum(_dot(p, sb) for p in _split3(x)[:parts])


def _dot_sel_nt(x, sel):
    sb = sel.astype(BF16)
    return sum(_dot_nt(p, sb) for p in _split3(x))


def _ada_kernel(c_ref, w_ref, b_ref, o_ref):
    c = c_ref[...]
    s = (c * _sigmoid(c)).astype(BF16)
    o_ref[0] = _dot(s, w_ref[0].astype(BF16)) + b_ref[0]


def _ada_call(c_all, w_ada, b_ada):
    depth, d, n = w_ada.shape
    rows = c_all.shape[0]
    tn = 1024
    return pl.pallas_call(
        _ada_kernel,
        grid=(depth, n // tn),
        in_specs=[pl.BlockSpec((rows, d), lambda l, j: (0, 0)),
                  pl.BlockSpec((1, d, tn), lambda l, j: (l, 0, j)),
                  pl.BlockSpec((1, 1, tn), lambda l, j: (l, 0, j))],
        out_specs=pl.BlockSpec((1, rows, tn), lambda l, j: (l, 0, j)),
        out_shape=jax.ShapeDtypeStruct((depth, rows, n), F32),
        compiler_params=_cparams(("arbitrary", "arbitrary")),
        name="ada",
    )(c_all, w_ada, b_ada.reshape(depth, 1, n))


def _normmod_kernel(x_ref, g_ref, sc_ref, sh_ref, o_ref):
    x = x_ref[0]
    y = x * lax.rsqrt(jnp.mean(x * x, axis=-1, keepdims=True) + EPS) * g_ref[...]
    o_ref[0] = (y * (1.0 + sc_ref[0]) + sh_ref[0]).astype(o_ref.dtype)


def _normmod_call(x, g, sc, sh, out_dtype, tm):
    G, R, D = x.shape
    rr = sc.shape[1]
    mod_spec = (pl.BlockSpec((1, 1, D), lambda b, i: (b, 0, 0)) if rr == 1
                else pl.BlockSpec((1, tm, D), lambda b, i: (b, i, 0)))
    return pl.pallas_call(
        _normmod_kernel,
        grid=(G, R // tm),
        in_specs=[pl.BlockSpec((1, tm, D), lambda b, i: (b, i, 0)),
                  pl.BlockSpec((1, D), lambda b, i: (0, 0)),
                  mod_spec, mod_spec],
        out_specs=pl.BlockSpec((1, tm, D), lambda b, i: (b, i, 0)),
        out_shape=jax.ShapeDtypeStruct((G, R, D), out_dtype),
        compiler_params=_cparams(("arbitrary", "arbitrary")),
        name="normmod",
    )(x, g.reshape(1, D), sc, sh)


def _rope_chunk(x, cos, sin_a, sin_b):
    return x * cos + pltpu.roll(x, 96, 1) * sin_a + pltpu.roll(x, 32, 1) * sin_b


def _inproj_kernel(h_ref, w_ref, b_ref, cos_ref, sa_ref, sb_ref, o_ref, *, tn):
    j = pl.program_id(0)
    o_ref[...] = _dot(h_ref[...], w_ref[...]) + b_ref[...]
    n_chunks = tn // LANE
    rope_tiles = -(-ROPE_COLS // tn)
    for jt in range(rope_tiles):
        chunks = min(n_chunks, (ROPE_COLS - jt * tn) // LANE)

        @pl.when(j == jt)
        def _():
            cos, sa, sb = cos_ref[...], sa_ref[...], sb_ref[...]
            for c in range(chunks):
                sl = slice(c * LANE, (c + 1) * LANE)
                o_ref[:, sl] = _rope_chunk(o_ref[:, sl], cos, sa, sb)


def _inproj_call(h, w, layer, b, cos, sin_a, sin_b, tm, tn):
    M, D = h.shape
    N = w.shape[2]
    tp = cos.shape[0] // tm
    tab = pl.BlockSpec((tm, LANE), lambda j, i: (i % tp, 0))
    return pl.pallas_call(
        functools.partial(_inproj_kernel, tn=tn),
        grid=(N // tn, M // tm),
        in_specs=[pl.BlockSpec((tm, D), lambda j, i: (i, 0)),
                  pl.BlockSpec((None, D, tn), lambda j, i: (layer, 0, j)),
                  pl.BlockSpec((1, tn), lambda j, i: (0, j)),
                  tab, tab, tab],
        out_specs=pl.BlockSpec((tm, tn), lambda j, i: (i, j)),
        out_shape=jax.ShapeDtypeStruct((M, N), F32),
        compiler_params=_cparams(("arbitrary", "arbitrary")),
        name="inproj",
    )(h, w, b, cos, sin_a, sin_b)


def _attn_prompt_kernel(q_ref, kp_ref, kc_ref, vp_ref, vc_ref, o_ref, l_ref, obuf, lbuf, *, dil, nq):
    i = pl.program_id(2)
    row = lax.broadcasted_iota(jnp.int32, (BAND, 2 * BAND), 0)
    col = lax.broadcasted_iota(jnp.int32, (BAND, 2 * BAND), 1)
    band = (col >= row) & (col <= row + BAND)
    band_first = band & ((col >= BAND) | (i > 0))
    nr = nq * BAND

    def residues(rs):
        rows_of, items = [], []
        for u, r in enumerate(rs):
            if dil == 1:
                rows, prev_rows = slice(None), slice(nr - BAND, nr)
            else:
                rows, prev_rows = pl.ds(r, nr, stride=dil), pl.ds((nr - BAND) * dil + r, BAND, stride=dil)
            rows_of.append(rows)
            q_all = q_ref[0, rows, :].astype(BF16)
            k_all = jnp.concatenate([kp_ref[0, prev_rows, :], kc_ref[0, rows, :]], axis=0).astype(BF16)
            v_all = jnp.concatenate([vp_ref[0, prev_rows, :], vc_ref[0, rows, :]], axis=0).astype(BF16)
            for sb in range(nq):
                qs = slice(sb * BAND, (sb + 1) * BAND)
                ks = slice(sb * BAND, (sb + 2) * BAND)
                for h in range(LANE // HEAD_DIM):
                    hs = slice(h * HEAD_DIM, (h + 1) * HEAD_DIM)
                    items.append((u, qs, hs, q_all[qs, hs], k_all[ks, hs], v_all[ks, hs],
                                  band_first if sb == 0 else band))
        ss = [jnp.where(it[6], _dot_nt(it[3], it[4]) * ATTN_SCALE, NEG) for it in items]
        mxs = [jnp.max(s, axis=-1, keepdims=True) for s in ss]
        es = [jnp.exp(s - mx) for s, mx in zip(ss, mxs)]
        dens = [jnp.sum(e, axis=-1, keepdims=True) for e in es]
        pvs = [_dot(e.astype(BF16), it[5]) for e, it in zip(es, items)]
        outs = [pv / den for pv, den in zip(pvs, dens)]
        lses = [jnp.broadcast_to(mx + jnp.log(den), (BAND, HEAD_DIM)) for mx, den in zip(mxs, dens)]
        for it, o, l in zip(items, outs, lses):
            obuf[it[0], it[1], it[2]] = o
            lbuf[it[0], it[1], it[2]] = l
        for u, rows in enumerate(rows_of):
            o_ref[0, rows, :] = obuf[u]
            l_ref[0, rows, :] = lbuf[u]

    if dil == 1:
        residues([0])
    elif dil <= RES_UNROLL:
        residues(list(range(dil)))
    else:
        def body(j, carry):
            residues([j * RES_UNROLL + u for u in range(RES_UNROLL)])
            return carry

        lax.fori_loop(0, dil // RES_UNROLL, body, 0)


def _attn_prompt_call(proj, g, dil):
    B, S, N = proj.shape
    nq = max(1, math.gcd(S, 512) // (BAND * dil))
    tb = nq * BAND * dil
    nb = S // tb
    hp = GROUP_W // LANE
    cq, ck, cv = (C_AQ + g * GROUP_W) // LANE, (C_AK + g * GROUP_W) // LANE, (C_AV + g * GROUP_W) // LANE

    def spec(cblk, prev):
        if prev:
            return pl.BlockSpec((1, tb, LANE), lambda b, p, i: (b, jnp.maximum(i - 1, 0), cblk + p))
        return pl.BlockSpec((1, tb, LANE), lambda b, p, i: (b, i, cblk + p))

    ospec = pl.BlockSpec((1, tb, LANE), lambda b, p, i: (b, i, p))
    o, lse = pl.pallas_call(
        functools.partial(_attn_prompt_kernel, dil=dil, nq=nq),
        grid=(B, hp, nb),
        in_specs=[spec(cq, False), spec(ck, True), spec(ck, False), spec(cv, True), spec(cv, False)],
        out_specs=[ospec, ospec],
        out_shape=[jax.ShapeDtypeStruct((B, S, GROUP_W), F32)] * 2,
        scratch_shapes=[pltpu.VMEM((min(dil, RES_UNROLL), nq * BAND, LANE), F32)] * 2,
        compiler_params=_cparams(("arbitrary", "arbitrary", "arbitrary")),
        name=f"attn_prompt_g{g}",
    )(proj, proj, proj, proj, proj)
    return o, lse


def _win_extract_kernel(x_ref, o_ref):
    o_ref[0, 0] = x_ref[0].T


def _win_extract_call(proj, g, keep):
    B, S, N = proj.shape
    chunk = min(keep, 512)
    first = (S - keep) // chunk
    cblk = C_AK // GROUP_W + g
    step = (C_AV - C_AK) // GROUP_W
    out = pl.pallas_call(
        _win_extract_kernel,
        grid=(B, 2, keep // chunk),
        in_specs=[pl.BlockSpec((1, chunk, GROUP_W), lambda b, kv, j: (b, first + j, cblk + step * kv))],
        out_specs=pl.BlockSpec((1, 1, GROUP_W, chunk), lambda b, kv, j: (b, kv, 0, j)),
        out_shape=jax.ShapeDtypeStruct((B, 2, GROUP_W, keep), F32),
        compiler_params=_cparams(("arbitrary", "arbitrary", "arbitrary")),
        name=f"win_extract_g{g}",
    )(proj)
    return out.reshape(B, 2, ATTN_HPG, HEAD_DIM, keep).transpose(0, 4, 1, 2, 3)


def _attn_mix_kernel(o0, o1, o2, l0, l1, l2, y_ref):
    a0, a1, a2 = l0[...], l1[...], l2[...]
    mx = jnp.maximum(jnp.maximum(a0, a1), a2)
    e0, e1, e2 = jnp.exp(a0 - mx), jnp.exp(a1 - mx), jnp.exp(a2 - mx)
    inv = 1.0 / (e0 + e1 + e2)
    y_ref[:, 0 * GROUP_W:1 * GROUP_W] = (o0[...] * (e0 * inv)).astype(y_ref.dtype)
    y_ref[:, 1 * GROUP_W:2 * GROUP_W] = (o1[...] * (e1 * inv)).astype(y_ref.dtype)
    y_ref[:, 2 * GROUP_W:3 * GROUP_W] = (o2[...] * (e2 * inv)).astype(y_ref.dtype)


def _attn_mix_call(os_, ls_, tm):
    M = os_[0].shape[0]
    spec = pl.BlockSpec((tm, GROUP_W), lambda i: (i, 0))
    return pl.pallas_call(
        _attn_mix_kernel,
        grid=(M // tm,),
        in_specs=[spec] * 6,
        out_specs=pl.BlockSpec((tm, ATTN_WIDTH), lambda i: (i, 0)),
        out_shape=jax.ShapeDtypeStruct((M, ATTN_WIDTH), BF16),
        compiler_params=_cparams(("arbitrary",)),
        name="attn_mix",
    )(*os_, *ls_)


QROWS = 16


def _attn_sample_kernel(qkv_ref, c0_ref, c1_ref, c2_ref, *rest, T, n_prev):
    y_ref, n0_ref, n1_ref, n2_ref, q16, npad, ysc = rest[n_prev:]
    q16[...] = jnp.zeros_like(q16)
    npad[...] = jnp.zeros_like(npad)
    q16[0:T, :] = qkv_ref[0, :, 0:ATTN_WIDTH]
    lane = lax.broadcasted_iota(jnp.int32, (HEAD_DIM, LANE), 1)
    items = []
    for g, (cref, nref) in enumerate(((c0_ref, n0_ref), (c1_ref, n1_ref), (c2_ref, n2_ref))):
        Lb = cref.shape[-1]
        dil = ATTN_GROUPS[g][1]
        npad[g, 0:T, 0:GROUP_W] = qkv_ref[0, :, C_AK + g * GROUP_W:C_AK + (g + 1) * GROUP_W]
        npad[g, 0:T, GROUP_W:2 * GROUP_W] = qkv_ref[0, :, C_AV + g * GROUP_W:C_AV + (g + 1) * GROUP_W]
        new_t = npad[g].T
        ncols = Lb + LANE
        t = lax.broadcasted_iota(jnp.int32, (QROWS, ncols), 0)
        c = lax.broadcasted_iota(jnp.int32, (QROWS, ncols), 1)
        if dil == 1:
            valid = ((c >= t) & (c < Lb)) | ((c >= Lb) & (c - Lb <= t) & (c - Lb < T))
        else:
            valid = ((c < Lb) & ((c & (dil - 1)) == t)) | (c == Lb + t)
        for h in range(ATTN_HPG):
            rk = slice(h * HEAD_DIM, (h + 1) * HEAD_DIM)
            rv = slice(GROUP_W + h * HEAD_DIM, GROUP_W + (h + 1) * HEAD_DIM)
            kc, vc = cref[0, 0, 0, h], cref[0, 0, 1, h]
            kt = jnp.concatenate([kc, new_t[rk, :]], axis=1).astype(BF16)
            vt = jnp.concatenate([vc, new_t[rv, :]], axis=1).astype(BF16)
            qh = q16[:, g * GROUP_W + h * HEAD_DIM:g * GROUP_W + (h + 1) * HEAD_DIM].astype(BF16)
            items.append((qh, kt, vt, valid))
            for kv, blk, rn in ((0, kc, rk), (1, vc, rv)):
                sh = pltpu.roll(blk, Lb - T, 1)
                newc = pltpu.roll(new_t[rn, :], LANE - T, 1)
                if Lb > LANE:
                    nref[0, 0, kv, h, :, 0:Lb - LANE] = sh[:, 0:Lb - LANE]
                nref[0, 0, kv, h, :, Lb - LANE:Lb] = jnp.where(lane >= LANE - T, newc, sh[:, Lb - LANE:Lb])
    ss = [jnp.where(valid, _dot(qh, kt) * ATTN_SCALE, NEG) for qh, kt, vt, valid in items]
    mxs = [jnp.max(s, axis=-1, keepdims=True) for s in ss]
    es = [jnp.exp(s - mx) for s, mx in zip(ss, mxs)]
    dens = [jnp.sum(e, axis=-1, keepdims=True) for e in es]
    pvs = [_dot_nt(e.astype(BF16), it[2]) for e, it in zip(es, items)]
    outs = [pv / den for pv, den in zip(pvs, dens)]
    lses = [mx + jnp.log(den) for mx, den in zip(mxs, dens)]
    for h in range(ATTN_HPG):
        l0, l1, l2 = lses[h], lses[ATTN_HPG + h], lses[2 * ATTN_HPG + h]
        mx = jnp.maximum(jnp.maximum(l0, l1), l2)
        es = [jnp.exp(l0 - mx), jnp.exp(l1 - mx), jnp.exp(l2 - mx)]
        inv = 1.0 / (es[0] + es[1] + es[2])
        for g in range(3):
            c0 = g * GROUP_W + h * HEAD_DIM
            ysc[:, c0:c0 + HEAD_DIM] = outs[g * ATTN_HPG + h] * (es[g] * inv)
    y_ref[0] = ysc[0:T, :].astype(y_ref.dtype)


def _attn_sample_call(qkv, caches_t, layer, prev):
    B, T, _ = qkv.shape
    depth = caches_t[0].shape[0]
    for g, (win, dil) in enumerate(ATTN_GROUPS):
        assert caches_t[g].shape[-1] == win and win // dil == BAND and (g == 0 or T <= dil), \
            "sample attention assumes full window caches"
    cspecs = [pl.BlockSpec((1, 1) + c.shape[2:], lambda b: (layer, b, 0, 0, 0, 0)) for c in caches_t]
    n_prev = 0 if prev is None else 3
    prev_args = [] if prev is None else list(prev)
    prev_specs = [pl.BlockSpec(memory_space=pl.ANY)] * n_prev
    aliases = {} if prev is None else {4 + g: 1 + g for g in range(3)}
    outs = pl.pallas_call(
        functools.partial(_attn_sample_kernel, T=T, n_prev=n_prev),
        grid=(B,),
        in_specs=[pl.BlockSpec((1, T, 3 * ATTN_WIDTH), lambda b: (b, 0, 0))] + cspecs + prev_specs,
        out_specs=[pl.BlockSpec((1, T, ATTN_WIDTH), lambda b: (b, 0, 0))] + cspecs,
        out_shape=[jax.ShapeDtypeStruct((B, T, ATTN_WIDTH), BF16)]
        + [jax.ShapeDtypeStruct(c.shape, F32) for c in caches_t],
        scratch_shapes=[pltpu.VMEM((QROWS, ATTN_WIDTH), F32),
                        pltpu.VMEM((3, LANE, 2 * GROUP_W), F32),
                        pltpu.VMEM((QROWS, ATTN_WIDTH), F32)],
        input_output_aliases=aliases,
        compiler_params=_cparams(("arbitrary",)),
        name="attn_sample",
    )(qkv, *caches_t, *prev_args)
    return outs[0], outs[1:]


def _mlstm_kernel(mqk_ref, mv_ref, mi_ref, mf_ref, mo_ref, cst_ref, wc_ref, bc_ref, c0_ref, n0_ref, m0_ref,
                  g_ref, tri_ref, y_ref, c_ref, n_ref, m_ref, cso_ref,
                  xbuf, vbuf, cs, ns, ms, tpad, kwp, *, T, Lp):
    c = pl.program_id(1)
    last = pl.num_programs(1) - 1
    W = MLSTM_WIDTH

    @pl.when(c == 0)
    def _():
        xbuf[...] = jnp.zeros_like(xbuf)
        vbuf[...] = jnp.zeros_like(vbuf)
        tpad[...] = jnp.zeros_like(tpad)
        kwp[...] = jnp.zeros_like(kwp)
        xbuf[5:8, :] = cst_ref[0]
        cs[...] = c0_ref[0]
        ns[...] = n0_ref[0]
        ms[...] = m0_ref[0]

    xbuf[8:8 + T, :] = mqk_ref[0]
    vbuf[0:T, :] = mv_ref[0]
    w = wc_ref[...]
    y = (bc_ref[...] + xbuf[8:8 + Lp, :] * w[3:4] + xbuf[7:7 + Lp, :] * w[2:3]
         + xbuf[6:6 + Lp, :] * w[1:2] + xbuf[5:5 + Lp, :] * w[0:1])
    tail = xbuf[8 + T - 3:8 + T, :]
    xbuf[5:8, :] = tail

    @pl.when(c == last)
    def _():
        cso_ref[0] = tail

    qk = y * _sigmoid(y)

    rowid = lax.broadcasted_iota(jnp.int32, (Lp, LANE), 0)
    real = rowid < T
    tpad[0:T, :] = mi_ref[0]
    ig = jnp.where(real, tpad[0:Lp, :], NEG)
    tpad[0:T, :] = mf_ref[0]
    lf = jnp.where(real, _log_sigmoid(tpad[0:Lp, :]), 0.0)
    tri = tri_ref[...]
    F = _dot_sel(tri, lf)
    inter = F + ms[...]
    tpad[0:Lp, :] = F
    FT = tpad[...].T
    tpad[0:Lp, :] = ig
    IT = tpad[...].T
    causal = tri > 0.5
    v_all = vbuf[0:Lp, :]
    heads = range(MLSTM_HEADS)
    hsl = [slice(h * HEAD_DIM, (h + 1) * HEAD_DIM) for h in heads]
    n_all = ns[...]
    m_row = ms[...]
    lane_row = lax.broadcasted_iota(jnp.int32, (1, LANE), 1)

    qs = [qk[:, hsl[h]] for h in heads]
    ks = [qk[:, W + h * HEAD_DIM:W + (h + 1) * HEAD_DIM] * (HEAD_DIM ** -0.5) for h in heads]
    qbs = [q.astype(BF16) for q in qs]
    vbs = [v_all[:, hsl[h]].astype(BF16) for h in heads]
    sqk = [_dot_nt(qbs[h], ks[h].astype(BF16)) for h in heads]
    qcs = [_dot(qbs[h], cs[h].astype(BF16)) for h in heads]

    fcols = [F[:, h:h + 1] for h in heads]
    icols = [ig[:, h:h + 1] for h in heads]
    inters = [inter[:, h:h + 1] for h in heads]
    logws = [jnp.where(causal, fcols[h] - FT[h:h + 1, 0:Lp] + IT[h:h + 1, 0:Lp], NEG) for h in heads]
    rmax = [jnp.max(logws[h], axis=-1, keepdims=True) for h in heads]
    mts = [jnp.maximum(rmax[h], inters[h]) for h in heads]
    As = [jnp.exp(logws[h] - mts[h]) * sqk[h] for h in heads]
    gqs = [jnp.exp(inters[h] - mts[h]) for h in heads]
    nrows = [n_all[h:h + 1, :] for h in heads]
    asum = [jnp.sum(As[h], axis=-1, keepdims=True) for h in heads]
    qn = [jnp.sum(qs[h] * nrows[h], axis=-1, keepdims=True) for h in heads]
    dens = [asum[h] + gqs[h] * qn[h] for h in heads]
    a_bf = [As[h].astype(BF16) for h in heads]
    avs = [_dot(a_bf[h], vbs[h]) for h in heads]
    mLs = [mts[h][Lp - 1:Lp, :] for h in heads]
    flast = [F[Lp - 1:Lp, h:h + 1] for h in heads]
    wLs = [jnp.exp(flast[h] - fcols[h] + icols[h] - mLs[h]) for h in heads]
    gls = [jnp.exp(flast[h] + m_row[0:1, h:h + 1] - mLs[h]) for h in heads]
    kws = [ks[h] * wLs[h] for h in heads]
    for h in heads:
        kwp[0:Lp, hsl[h]] = kws[h]
    ksum = [jnp.sum(kws[h], axis=0, keepdims=True) for h in heads]
    for h in heads:
        ns[h:h + 1, :] = gls[h] * nrows[h] + ksum[h]
    m_new = m_row
    for h in heads:
        m_new = jnp.where(lane_row == h, mLs[h], m_new)
    ms[...] = m_new

    kwt = kwp[...].T
    upds = [_dot(kwt[hsl[h], 0:Lp].astype(BF16), vbs[h]) for h in heads]

    nums = [avs[h] + gqs[h] * qcs[h] for h in heads]
    hhs = [nums[h] / jnp.maximum(jnp.abs(dens[h]), jnp.exp(-mts[h])) for h in heads]
    msq = [jnp.mean(hhs[h] * hhs[h], axis=-1, keepdims=True) for h in heads]
    hns = [hhs[h] * lax.rsqrt(msq[h] + EPS) * g_ref[0:1, hsl[h]] for h in heads]
    ogs = [_sigmoid(mo_ref[0, :, hsl[h]]) for h in heads]
    for h in heads:
        yo = hns[h] * ogs[h] if Lp == T else hns[h][0:T] * ogs[h]
        y_ref[0, :, hsl[h]] = yo.astype(y_ref.dtype)
    for h in heads:
        cs[h] = gls[h] * cs[h] + upds[h]

    @pl.when(c == last)
    def _():
        c_ref[0] = cs[...]
        n_ref[0] = ns[...]
        m_ref[0] = ms[...]


def _mlstm_call(proj, conv_state, w_conv, b_conv, c0, n0, m0, g_mlstm, chunk):
    B, S, N = proj.shape
    T = chunk
    nc = S // T
    Lp = max(16, T)
    H = MLSTM_HEADS
    m0p = jnp.pad(m0, ((0, 0), (0, LANE - H))).reshape(B, 1, LANE)
    tri = jnp.tril(jnp.ones((Lp, Lp), F32))

    def col(width, off):
        blk = off // width
        return pl.BlockSpec((1, T, width), lambda b, c: (b, c, blk))

    def const(shape):
        nd = len(shape)
        return pl.BlockSpec(shape, lambda b, c: (0,) * nd)

    def per_b(shape):
        nd = len(shape)
        return pl.BlockSpec((1,) + shape, lambda b, c: (b,) + (0,) * nd)

    outs = pl.pallas_call(
        functools.partial(_mlstm_kernel, T=T, Lp=Lp),
        grid=(B, nc),
        in_specs=[col(2 * MLSTM_WIDTH, C_MQK), col(MLSTM_WIDTH, C_MV), col(LANE, C_MI), col(LANE, C_MF),
                  col(MLSTM_WIDTH, C_MO), per_b((MLSTM_CONV - 1, 2 * MLSTM_WIDTH)),
                  const((MLSTM_CONV, 2 * MLSTM_WIDTH)), const((1, 2 * MLSTM_WIDTH)),
                  per_b((H, HEAD_DIM, HEAD_DIM)), per_b((H, HEAD_DIM)), per_b((1, LANE)),
                  const((1, MLSTM_WIDTH)), const((Lp, Lp))],
        out_specs=[pl.BlockSpec((1, T, MLSTM_WIDTH), lambda b, c: (b, c, 0)),
                   per_b((H, HEAD_DIM, HEAD_DIM)), per_b((H, HEAD_DIM)), per_b((1, LANE)),
                   per_b((MLSTM_CONV - 1, 2 * MLSTM_WIDTH))],
        out_shape=[jax.ShapeDtypeStruct((B, S, MLSTM_WIDTH), BF16),
                   jax.ShapeDtypeStruct((B, H, HEAD_DIM, HEAD_DIM), F32),
                   jax.ShapeDtypeStruct((B, H, HEAD_DIM), F32),
                   jax.ShapeDtypeStruct((B, 1, LANE), F32),
                   jax.ShapeDtypeStruct((B, MLSTM_CONV - 1, 2 * MLSTM_WIDTH), F32)],
        scratch_shapes=[pltpu.VMEM((8 + Lp, 2 * MLSTM_WIDTH), F32),
                        pltpu.VMEM((Lp, MLSTM_WIDTH), F32),
                        pltpu.VMEM((H, HEAD_DIM, HEAD_DIM), F32),
                        pltpu.VMEM((H, HEAD_DIM), F32),
                        pltpu.VMEM((1, LANE), F32),
                        pltpu.VMEM((LANE, LANE), F32),
                        pltpu.VMEM((LANE, MLSTM_WIDTH), F32)],
        compiler_params=_cparams(("arbitrary", "arbitrary")),
        name="mlstm",
    )(proj, proj, proj, proj, proj, conv_state, w_conv, b_conv.reshape(1, -1), c0, n0, m0p,
      g_mlstm.reshape(1, -1), tri)
    y, C, n, m, cso = outs
    return y, C, n, m[:, 0, :H], cso


def _gla_kernel(gqk_ref, gv_ref, gg_ref, ga_ref, wa_ref, wat_ref, s0_ref, g_ref, tri_ref, bd_ref, ee_ref,
                y_ref, s_ref, qkp, vp, gap, sbd, osc, *, T, Lp):
    c = pl.program_id(1)
    last = pl.num_programs(1) - 1
    KW, VW = GLA_K_WIDTH, GLA_V_WIDTH

    @pl.when(c == 0)
    def _():
        qkp[...] = jnp.zeros_like(qkp)
        vp[...] = jnp.zeros_like(vp)
        gap[...] = jnp.zeros_like(gap)
        sbd[...] = jnp.zeros_like(sbd)
        osc[...] = jnp.zeros_like(osc)
        for h in range(GLA_HEADS):
            sbd[h * GLA_DK:(h + 1) * GLA_DK, h * GLA_DV:(h + 1) * GLA_DV] = s0_ref[0, h]

    qkp[0:T, :] = gqk_ref[0]
    vp[0:T, :] = gv_ref[0]
    lane = lax.broadcasted_iota(jnp.int32, (T, LANE), 1)
    gap[0:T, :] = jnp.where(lane == GLA_RANK, 1.0, ga_ref[0])

    tri = tri_ref[...]
    ga = gap[0:Lp, :]
    rowid = lax.broadcasted_iota(jnp.int32, (Lp, KW), 0)
    wa_hi, wa_lo, _ = _split3(wa_ref[...])
    wat_hi, wat_lo, _ = _split3(wat_ref[...])
    ga_b = ga.astype(BF16)
    gat_b = gap[...].T.astype(BF16)
    la = _dot(ga_b, wa_hi) + _dot(ga_b, wa_lo)
    la = jnp.where(rowid < T, _log_sigmoid(la) / GLA_TAU, 0.0)
    bc = _dot_sel(tri[0:Lp, 0:Lp], la)
    colid = lax.broadcasted_iota(jnp.int32, (KW, LANE), 1)
    lat = _dot(wat_hi, gat_b) + _dot(wat_lo, gat_b)
    lat = jnp.where(colid < T, _log_sigmoid(lat) / GLA_TAU, 0.0)
    bct = _dot_sel_nt(lat, tri)
    blcol = bct[:, LANE - 1:LANE]
    blrow = bc[Lp - 1:Lp, :]
    kt = qkp[...].T[KW:2 * KW, :]
    q = qkp[0:Lp, 0:KW] * (GLA_DK ** -0.5)
    k = qkp[0:Lp, KW:2 * KW]
    v = vp[0:Lp, :]
    vfull = vp[...]
    bd = bd_ref[...]
    safe = jnp.min(blrow) >= -GLA_SAFE_DECAY

    @pl.when(safe)
    def _():
        qb = (q * jnp.exp(bc)).astype(BF16)
        kb = (k * jnp.exp(-bc)).astype(BF16)
        o_inter = _dot(qb, sbd[...].astype(BF16))
        causal = tri[0:Lp, 0:Lp] > 0.5
        vb = v.astype(BF16)
        heads = range(GLA_HEADS)
        scores = [_dot_nt(qb[:, h * GLA_DK:(h + 1) * GLA_DK], kb[:, h * GLA_DK:(h + 1) * GLA_DK]) for h in heads]
        a_bf = [jnp.where(causal, s, 0.0).astype(BF16) for s in scores]
        ovs = [_dot(a_bf[h], vb[:, h * GLA_DV:(h + 1) * GLA_DV]) for h in heads]
        osc[0:Lp, :] = o_inter + jnp.concatenate(ovs, axis=1)
        klt = (kt * jnp.exp(blcol - bct)).astype(BF16)
        sbd[...] = bd * (jnp.exp(blcol) * sbd[...] + _dot(klt, vfull.astype(BF16)))

    @pl.when(jnp.logical_not(safe))
    def _():
        srow = lax.broadcasted_iota(jnp.int32, (LANE, LANE), 0)

        def body(t, carry):
            sel = (srow == t).astype(F32)
            lac = jnp.dot(lat, sel, precision=HI, preferred_element_type=F32)
            kc = jnp.dot(kt, sel, precision=HI, preferred_element_type=F32)
            dec = jnp.concatenate([jnp.exp(lac)] * (VW // LANE), axis=1)
            kcw = jnp.concatenate([kc] * (VW // LANE), axis=1)
            vrow = vp[pl.ds(t, 1), :]
            snew = bd * (dec * sbd[...] + kcw * vrow)
            sbd[...] = snew
            qrow = jnp.broadcast_to(qkp[pl.ds(t, 1), 0:KW] * (GLA_DK ** -0.5), (8, KW))
            orow = jnp.dot(qrow, snew, precision=HI, preferred_element_type=F32)
            osc[pl.ds(t, 1), :] = orow[0:1]
            return carry

        lax.fori_loop(0, T, body, 0)

    o = osc[0:Lp, :]
    ms = _dot_sel_rhs(o * o, ee_ref[...])
    og = o * lax.rsqrt(ms + EPS) * g_ref[...]
    gg = gg_ref[0]
    yo = (og if Lp == T else og[0:T]) * (gg * _sigmoid(gg))
    y_ref[0] = yo.astype(y_ref.dtype)

    @pl.when(c == last)
    def _():
        for h in range(GLA_HEADS):
            s_ref[0, h] = sbd[h * GLA_DK:(h + 1) * GLA_DK, h * GLA_DV:(h + 1) * GLA_DV]


def _gla_call(proj, w_a2, b_a2, s0, g_gla, chunk):
    B, S, N = proj.shape
    T = chunk
    nc = S // T
    Lp = max(16, T)
    H, KW, VW = GLA_HEADS, GLA_K_WIDTH, GLA_V_WIDTH
    wa = jnp.zeros((LANE, KW), F32).at[:GLA_RANK].set(w_a2).at[GLA_RANK].set(b_a2)
    tri = jnp.tril(jnp.ones((LANE, LANE), F32))
    hk = jnp.arange(KW) // GLA_DK
    hv = jnp.arange(VW) // GLA_DV
    bd = (hk[:, None] == hv[None, :]).astype(F32)
    ee = (hv[:, None] == hv[None, :]).astype(F32) / GLA_DV

    def col(width, off):
        blk = off // width
        return pl.BlockSpec((1, T, width), lambda b, c: (b, c, blk))

    def const(shape):
        nd = len(shape)
        return pl.BlockSpec(shape, lambda b, c: (0,) * nd)

    y, s = pl.pallas_call(
        functools.partial(_gla_kernel, T=T, Lp=Lp),
        grid=(B, nc),
        in_specs=[col(2 * KW, C_GQK), col(VW, C_GV), col(VW, C_GG), col(LANE, C_GA),
                  const((LANE, KW)), const((KW, LANE)),
                  pl.BlockSpec((1, H, GLA_DK, GLA_DV), lambda b, c: (b, 0, 0, 0)),
                  const((1, VW)), const((LANE, LANE)), const((KW, VW)), const((VW, VW))],
        out_specs=[pl.BlockSpec((1, T, VW), lambda b, c: (b, c, 0)),
                   pl.BlockSpec((1, H, GLA_DK, GLA_DV), lambda b, c: (b, 0, 0, 0))],
        out_shape=[jax.ShapeDtypeStruct((B, S, VW), BF16),
                   jax.ShapeDtypeStruct((B, H, GLA_DK, GLA_DV), F32)],
        scratch_shapes=[pltpu.VMEM((LANE, 2 * KW), F32),
                        pltpu.VMEM((LANE, VW), F32),
                        pltpu.VMEM((LANE, LANE), F32),
                        pltpu.VMEM((KW, VW), F32),
                        pltpu.VMEM((LANE, VW), F32)],
        compiler_params=_cparams(("arbitrary", "arbitrary")),
        name="gla",
    )(proj, proj, proj, proj, wa, wa.T, s0, g_gla.reshape(1, -1), tri, bd, ee)
    return y, s


def _outproj_kernel(ya_ref, ym_ref, yg_ref, w_ref, x_ref, gt_ref, g2_ref, sc_ref, sh_ref, xo_ref, h2_ref):
    a, b = ATTN_WIDTH, ATTN_WIDTH + MLSTM_WIDTH
    acc = _dot(ya_ref[0], w_ref[0:a, :])
    acc = acc + _dot(ym_ref[0], w_ref[a:b, :])
    acc = acc + _dot(yg_ref[0], w_ref[b:, :])
    x = x_ref[0] + gt_ref[0] * acc
    xo_ref[0] = x
    y = x * lax.rsqrt(jnp.mean(x * x, axis=-1, keepdims=True) + EPS) * g2_ref[...]
    h2_ref[0] = (y * (1.0 + sc_ref[0]) + sh_ref[0]).astype(h2_ref.dtype)


def _outproj_call(ya, ym, yg, w_out, layer, x, gt, g2, sc, sh, tm):
    G, R, D = x.shape
    rr = gt.shape[1]
    mod_spec = (pl.BlockSpec((1, 1, D), lambda b, i: (b, 0, 0)) if rr == 1
                else pl.BlockSpec((1, tm, D), lambda b, i: (b, i, 0)))

    def act(width):
        return pl.BlockSpec((1, tm, width), lambda b, i: (b, i, 0))

    return pl.pallas_call(
        _outproj_kernel,
        grid=(G, R // tm),
        in_specs=[act(ATTN_WIDTH), act(MLSTM_WIDTH), act(GLA_V_WIDTH),
                  pl.BlockSpec((None,) + w_out.shape[1:], lambda b, i: (layer, 0, 0)),
                  act(D), mod_spec, pl.BlockSpec((1, D), lambda b, i: (0, 0)), mod_spec, mod_spec],
        out_specs=[act(D), act(D)],
        out_shape=[jax.ShapeDtypeStruct((G, R, D), F32), jax.ShapeDtypeStruct((G, R, D), BF16)],
        compiler_params=_cparams(("arbitrary", "arbitrary")),
        name="outproj",
    )(ya, ym, yg, w_out, x, gt, g2.reshape(1, D), sc, sh)


def _ffn_kernel(h_ref, wg_ref, wu_ref, wc_ref, bc_ref, wo_ref, x_ref, gt_ref, init_ref,
                xo_ref, st_ref, ubuf, cbuf, *, tm, u, R, tiles_per_seq):
    m = pl.program_id(1)
    f = pl.program_id(2)
    nf = pl.num_programs(2)
    h = h_ref[0]

    @pl.when(m % tiles_per_seq == 0)
    def _():
        ubuf[0:R, :] = init_ref[0]

    @pl.when(m % tiles_per_seq != 0)
    def _():
        ubuf[0:R, :] = cbuf[f]

    @pl.when(f == 0)
    def _():
        xo_ref[0] = jnp.zeros_like(xo_ref[0])

    tf = wg_ref.shape[1]
    chunks = [slice(c0, min(c0 + FFN_SUB, tf)) for c0 in range(0, tf, FFN_SUB)]

    def up_matmuls(cs_):
        return _dot(h, wg_ref[:, cs_]), _dot(h, wu_ref[:, cs_])

    pending = up_matmuls(chunks[0])
    acc = xo_ref[0]
    for ci, cs_ in enumerate(chunks):
        ug, uu = pending
        if ci + 1 < len(chunks):
            pending = up_matmuls(chunks[ci + 1])
        ubuf[R:R + tm, cs_] = ug
        w = wc_ref[:, cs_]
        gate = (bc_ref[:, cs_] + ug * w[2:3] + ubuf[R - u:R - u + tm, cs_] * w[1:2]
                + ubuf[R - 2 * u:R - 2 * u + tm, cs_] * w[0:1])
        act = (gate * _sigmoid(gate) * uu).astype(BF16)
        acc = _dot(act, wo_ref[cs_, :]) + acc
    xo_ref[0] = acc
    tail = ubuf[tm:tm + R, :]
    cbuf[f] = tail
    st_ref[0, 0] = tail

    @pl.when(f == nf - 1)
    def _():
        xo_ref[0] = x_ref[0] + gt_ref[0] * xo_ref[0]


def _ffn_call(h2, w_gu, w_conv, b_conv, w_o, layer, x, gt, init, tm, tf, u):
    G, rows, D = x.shape
    F = w_gu.shape[2] // 2
    R = max(8, 2 * u)
    nm, nf = rows // tm, F // tf
    rr = gt.shape[1]
    mod_spec = (pl.BlockSpec((1, 1, D), lambda b, i, f: (b, 0, 0)) if rr == 1
                else pl.BlockSpec((1, tm, D), lambda b, i, f: (b, i, 0)))
    xo, st = pl.pallas_call(
        functools.partial(_ffn_kernel, tm=tm, u=u, R=R, tiles_per_seq=nm),
        grid=(G, nm, nf),
        in_specs=[pl.BlockSpec((1, tm, D), lambda b, i, f: (b, i, 0)),
                  pl.BlockSpec((None, D, tf), lambda b, i, f: (layer, 0, f)),
                  pl.BlockSpec((None, D, tf), lambda b, i, f: (layer, 0, nf + f)),
                  pl.BlockSpec((FFN_CONV, tf), lambda b, i, f: (0, f)),
                  pl.BlockSpec((1, tf), lambda b, i, f: (0, f)),
                  pl.BlockSpec((None, tf, D), lambda b, i, f: (layer, f, 0)),
                  pl.BlockSpec((1, tm, D), lambda b, i, f: (b, i, 0), pipeline_mode=pl.Buffered(1)),
                  mod_spec,
                  pl.BlockSpec((1, R, tf), lambda b, i, f: (b, 0, f))],
        out_specs=[pl.BlockSpec((1, tm, D), lambda b, i, f: (b, i, 0)),
                   pl.BlockSpec((1, 1, R, tf), lambda b, i, f: (b, i, 0, f))],
        out_shape=[jax.ShapeDtypeStruct((G, rows, D), F32),
                   jax.ShapeDtypeStruct((G, nm, R, F), F32)],
        scratch_shapes=[pltpu.VMEM((R + tm, tf), F32), pltpu.VMEM((nf, R, tf), F32)],
        compiler_params=_cparams(("arbitrary", "arbitrary", "arbitrary")),
        name="ffn",
    )(h2, w_gu, w_gu, w_conv, b_conv.reshape(1, F), w_o, x, gt, init)
    return xo, st[:, nm - 1]


def _pack_w_kernel(w_ref, o_ref):
    x = w_ref[0]
    rows = x.shape[0]
    n_in = x.shape[1]
    attn = 3 * ATTN_WIDTH
    gates = attn + 3 * MLSTM_WIDTH
    tail = gates + 2 * MLSTM_HEADS
    o_ref[0, :, 0:attn] = x[:, 0:attn].astype(BF16)
    lane = lax.broadcasted_iota(jnp.int32, (rows, LANE), 1)
    gwin = x[:, gates:gates + LANE]
    o_ref[0, :, C_MI:C_MI + LANE] = jnp.where(lane < MLSTM_HEADS, gwin, 0.0).astype(BF16)
    o_ref[0, :, C_MF:C_MF + LANE] = jnp.where(lane < MLSTM_HEADS, pltpu.roll(gwin, LANE - MLSTM_HEADS, 1),
                                              0.0).astype(BF16)
    o_ref[0, :, C_MQK:C_MQK + 3 * MLSTM_WIDTH] = x[:, attn:gates].astype(BF16)
    o_ref[0, :, C_GA:N_PACK] = jnp.zeros((rows, N_PACK - C_GA), BF16)
    o_ref[0, :, C_MO:C_MO + n_in - tail] = x[:, tail:n_in].astype(BF16)


def _pack_w_call(w_in):
    depth, d, n_in = w_in.shape
    tk = 256
    return pl.pallas_call(
        _pack_w_kernel,
        grid=(depth, d // tk),
        in_specs=[pl.BlockSpec((1, tk, n_in), lambda l, i: (l, i, 0))],
        out_specs=pl.BlockSpec((1, tk, N_PACK), lambda l, i: (l, i, 0)),
        out_shape=jax.ShapeDtypeStruct((depth, d, N_PACK), BF16),
        compiler_params=_cparams(("arbitrary", "arbitrary")),
        name="pack_w_in",
    )(w_in)


def _pack_in_proj(w_in, b_in):
    def split(a):
        out, off = [], 0
        for s in IN_SPLITS:
            out.append(a[..., off:off + s])
            off += s
        return out

    def pad(a, n):
        return jnp.pad(a, [(0, 0)] * (a.ndim - 1) + [(0, n - a.shape[-1])])

    def pack(a):
        aq, ak, av, mqk, mv, mi, mf, mo, gq, gk, gv, gg, ga = split(a)
        return jnp.concatenate([aq, ak, av, pad(mi, LANE), pad(mf, LANE), mqk, mv, mo, gq, gk, gv, gg,
                                pad(ga, LANE)], axis=-1)

    return _pack_w_call(w_in), pack(b_in)


def _rope_tables(pos):
    half = HEAD_DIM // 2
    inv_freq = jnp.power(ROPE_THETA, -jnp.arange(half, dtype=F32) / half)
    ang = pos.astype(F32)[:, None] * inv_freq[None, :]
    cos, sin = jnp.cos(ang), jnp.sin(ang)
    zero = jnp.zeros_like(sin)
    reps = LANE // HEAD_DIM
    return (jnp.tile(jnp.concatenate([cos, cos], -1), (1, reps)),
            jnp.tile(jnp.concatenate([-sin, zero], -1), (1, reps)),
            jnp.tile(jnp.concatenate([zero, sin], -1), (1, reps)))


def _pick_tile(n, pref):
    t = math.gcd(n, pref)
    return t


def kernel(x_prompt, x_sample, c_prompt, c_sample, cache_win0_kv, cache_win1_kv, cache_win2_kv, state_mlstm_C, state_mlstm_n, state_mlstm_m, state_mlstm_conv, state_gla_S, state_ffn_conv, w_ada, b_ada, g_norm1, g_norm2, w_in, b_in, w_mconv, b_mconv, g_mlstm, w_gla_a2, b_gla_a2, g_gla, w_out, w_ff_in, w_fconv, b_fconv, w_ff_out, g_final):
    B, S, D = x_prompt.shape
    Bs, Ts, _ = x_sample.shape
    depth = w_ada.shape[0]
    d_ff = w_fconv.shape[-1]
    caches = (cache_win0_kv, cache_win1_kv, cache_win2_kv)
    Ms = Bs * Ts

    n_c = B + Bs
    rows_c = -(-n_c // 8) * 8
    c_all = jnp.pad(jnp.concatenate([c_prompt, c_sample], axis=0), ((0, rows_c - n_c), (0, 0)))
    mod = _ada_call(c_all, w_ada, b_ada).reshape(depth, rows_c, 6, D)

    w_in_p, b_in_p = _pack_in_proj(w_in, b_in)
    w_out_b = w_out.astype(BF16)
    w_gu = w_ff_in.astype(BF16)
    w_o = w_ff_out.astype(BF16)

    rope_p = _rope_tables(jnp.arange(S))
    rope_s = _rope_tables(PAST_LEN + jnp.repeat(jnp.arange(Ts), Bs))

    tm_p = _pick_tile(S, 512)
    tm_in = _pick_tile(S, 1024)
    tm_ff = _pick_tile(S, 1024)
    tn = 1024
    tf = _pick_tile(d_ff, 512)

    xp = x_prompt
    xs = x_sample.transpose(1, 0, 2).reshape(1, Ms, D)
    zeros_p = {
        'mconv': jnp.zeros((B, MLSTM_CONV - 1, 2 * MLSTM_WIDTH), F32),
        'C': jnp.zeros((B, MLSTM_HEADS, HEAD_DIM, HEAD_DIM), F32),
        'n': jnp.zeros((B, MLSTM_HEADS, HEAD_DIM), F32),
        'm': jnp.zeros((B, MLSTM_HEADS), F32),
        'S': jnp.zeros((B, GLA_HEADS, GLA_DK, GLA_DV), F32),
        'fconv': jnp.zeros((B, 8, d_ff), F32),
    }
    names = ('win0', 'win1', 'win2', 'C', 'n', 'm', 'mconv', 'S', 'fconv')
    col_p = {k: [] for k in names}
    col_s = {k: [] for k in names}
    caches_t = [c.transpose(0, 1, 3, 4, 5, 2) for c in caches]
    new_caches = None

    for l in range(depth):
        mp = mod[l, :B]
        ms_ = jnp.tile(mod[l, B:B + Bs], (Ts, 1, 1))

        def mods_p(i):
            return mp[:, i:i + 1, :]

        def mods_s(i):
            return ms_[None, :, i, :]

        h = _normmod_call(xp, g_norm1[l], mods_p(1), mods_p(0), BF16, tm_p)
        proj = _inproj_call(h.reshape(B * S, D), w_in_p, l, b_in_p[l][None], *rope_p, tm_in, tn)
        proj = proj.reshape(B, S, N_PACK)
        os_, ls_ = [], []
        for g, (win, dil) in enumerate(ATTN_GROUPS):
            o, lse = _attn_prompt_call(proj, g, dil)
            os_.append(o.reshape(B * S, GROUP_W))
            ls_.append(lse.reshape(B * S, GROUP_W))
            col_p[f'win{g}'].append(_win_extract_call(proj, g, min(win, S)))
        ya = _attn_mix_call(os_, ls_, tm_p).reshape(B, S, ATTN_WIDTH)
        ym, C, n, m, cso = _mlstm_call(proj, zeros_p['mconv'], w_mconv[l], b_mconv[l], zeros_p['C'],
                                        zeros_p['n'], zeros_p['m'], g_mlstm[l], math.gcd(S, MLSTM_CHUNK))
        yg, Sg = _gla_call(proj, w_gla_a2[l], b_gla_a2[l], zeros_p['S'], g_gla[l], math.gcd(S, GLA_CHUNK))
        xp, h2 = _outproj_call(ya, ym, yg, w_out_b, l, xp, mods_p(2), g_norm2[l], mods_p(4), mods_p(3), tm_p)
        xp, fst = _ffn_call(h2, w_gu, w_fconv[l], b_fconv[l], w_o, l, xp, mods_p(5),
                            zeros_p['fconv'], tm_ff, tf, 1)
        for k_, v_ in (('C', C), ('n', n), ('m', m), ('mconv', cso), ('S', Sg), ('fconv', fst[:, 6:8])):
            col_p[k_].append(v_)

        h = _normmod_call(xs, g_norm1[l], mods_s(1), mods_s(0), BF16, Ms)
        proj = _inproj_call(h.reshape(Ms, D), w_in_p, l, b_in_p[l][None], *rope_s, Ms, tn)
        proj_b = proj.reshape(Ts, Bs, N_PACK).transpose(1, 0, 2)
        ya, new_caches = _attn_sample_call(proj_b[:, :, :3 * ATTN_WIDTH], caches_t, l, new_caches)
        ym, C, n, m, cso = _mlstm_call(proj_b, state_mlstm_conv[l], w_mconv[l], b_mconv[l], state_mlstm_C[l],
                                        state_mlstm_n[l], state_mlstm_m[l], g_mlstm[l], Ts)
        yg, Sg = _gla_call(proj_b, w_gla_a2[l], b_gla_a2[l], state_gla_S[l], g_gla[l], Ts)

        def tmaj(a):
            return a.transpose(1, 0, 2).reshape(1, Ms, a.shape[-1])

        xs, h2 = _outproj_call(tmaj(ya), tmaj(ym), tmaj(yg), w_out_b, l, xs, mods_s(2), g_norm2[l],
                               mods_s(4), mods_s(3), Ms)
        n_st = (FFN_CONV - 1) * Bs
        r_st = max(8, n_st)
        init = state_ffn_conv[l].transpose(1, 0, 2).reshape(1, n_st, d_ff)
        init = jnp.pad(init, ((0, 0), (r_st - n_st, 0), (0, 0)))
        xs, fst = _ffn_call(h2, w_gu, w_fconv[l], b_fconv[l], w_o, l, xs, mods_s(5), init, Ms, tf, Bs)
        fst = fst[:, r_st - n_st:].reshape(FFN_CONV - 1, Bs, d_ff).transpose(1, 0, 2)
        for k_, v_ in (('C', C), ('n', n), ('m', m), ('mconv', cso), ('S', Sg), ('fconv', fst)):
            col_s[k_].append(v_)

    y_prompt = _normmod_call(xp, g_final, jnp.zeros((B, 1, D), F32), jnp.zeros((B, 1, D), F32), F32, tm_p)
    y_sample = _normmod_call(xs, g_final, jnp.zeros((1, 1, D), F32), jnp.zeros((1, 1, D), F32), F32, Ms)
    y_sample = y_sample.reshape(Ts, Bs, D).transpose(1, 0, 2)
    sp = {k: jnp.stack(v, axis=0) for k, v in col_p.items()}
    ss = {k: jnp.stack(v, axis=0) for k, v in col_s.items() if v}
    for g in range(3):
        ss[f'win{g}'] = new_caches[g].transpose(0, 1, 5, 2, 3, 4)
    return (y_prompt, y_sample, sp['win0'], ss['win0'], sp['win1'], ss['win1'], sp['win2'], ss['win2'],
            sp['C'], ss['C'], sp['n'], ss['n'], sp['m'], ss['m'], sp['mconv'], ss['mconv'],
            sp['S'], ss['S'], sp['fconv'], ss['fconv'])
```

```python
import functools
import math

import jax
import jax.numpy as jnp
from jax import lax
from jax.experimental import pallas as pl
from jax.experimental.pallas import tpu as pltpu

F32 = jnp.float32
BF16 = jnp.bfloat16
HI = lax.Precision.HIGHEST

HEAD_DIM = 64
ATTN_GROUPS = ((128, 1), (512, 4), (2048, 16))
ATTN_HPG = 4
ATTN_WIDTH = 3 * ATTN_HPG * HEAD_DIM
GROUP_W = ATTN_HPG * HEAD_DIM
ATTN_SCALE = HEAD_DIM ** -0.5
ROPE_THETA = 10000.0
BAND = 128
RES_UNROLL = 4
MIXER_BATCH = 2
MLSTM_HEADS = 10
MLSTM_WIDTH = MLSTM_HEADS * HEAD_DIM
MLSTM_CONV = 4
MLSTM_CHUNK = 128
GLA_HEADS = 10
GLA_DK = 32
GLA_DV = 64
GLA_K_WIDTH = GLA_HEADS * GLA_DK
GLA_V_WIDTH = GLA_HEADS * GLA_DV
GLA_RANK = 16
GLA_TAU = 16.0
GLA_CHUNK = 128
GLA_SAFE_DECAY = 60.0
FFN_CONV = 3
FFN_SUB = 512
EPS = 1e-6
PAST_LEN = 8192
NEG = -1e30

LANE = 128
VMEM_LIMIT = 56 * 1024 * 1024

IN_SPLITS = (ATTN_WIDTH, ATTN_WIDTH, ATTN_WIDTH, 2 * MLSTM_WIDTH, MLSTM_WIDTH, MLSTM_HEADS, MLSTM_HEADS,
             MLSTM_WIDTH, GLA_K_WIDTH, GLA_K_WIDTH, GLA_V_WIDTH, GLA_V_WIDTH, GLA_RANK)

C_AQ, C_AK, C_AV = 0, 768, 1536
C_MI, C_MF = 2304, 2432
C_MQK = 2560
C_MV, C_MO = 3840, 4480
C_GQK, C_GV, C_GG = 5120, 5760, 6400
C_GA = 7040
N_PACK = 7168
ROPE_COLS = 2 * ATTN_WIDTH


def _cparams(sem):
    return pltpu.CompilerParams(dimension_semantics=sem, vmem_limit_bytes=VMEM_LIMIT)


def _sigmoid(x):
    return 1.0 / (1.0 + jnp.exp(-x))


def _log_sigmoid(x):
    return jnp.minimum(x, 0.0) - jnp.log(1.0 + jnp.exp(-jnp.abs(x)))


def _dot(a, b):
    return jnp.dot(a, b, preferred_element_type=F32)


def _dot_nt(a, b):
    return lax.dot_general(a, b, (((1,), (1,)), ((), ())), preferred_element_type=F32)


def _split3(x):
    hi = x.astype(BF16)
    r1 = x - hi.astype(F32)
    mid = r1.astype(BF16)
    lo = (r1 - mid.astype(F32)).astype(BF16)
    return hi, mid, lo


def _dot_sel(sel, x, parts=3):
    sb = sel.astype(BF16)
    return sum(_dot(sb, p) for p in _split3(x)[:parts])


def _dot_sel_rhs(x, sel, parts=2):
    sb = sel.astype(BF16)
    return sum(_dot(p, sb) for p in _split3(x)[:parts])


def _dot_sel_nt(x, sel):
    sb = sel.astype(BF16)
    return sum(_dot_nt(p, sb) for p in _split3(x))


def _ada_kernel(c_ref, w_ref, b_ref, o_ref):
    c = c_ref[...]
    s = (c * _sigmoid(c)).astype(BF16)
    o_ref[0] = _dot(s, w_ref[0].astype(BF16)) + b_ref[0]


def _ada_call(c_all, w_ada, b_ada):
    depth, d, n = w_ada.shape
    rows = c_all.shape[0]
    tn = 1024
    return pl.pallas_call(
        _ada_kernel,
        grid=(depth, n // tn),
        in_specs=[pl.BlockSpec((rows, d), lambda l, j: (0, 0)),
                  pl.BlockSpec((1, d, tn), lambda l, j: (l, 0, j)),
                  pl.BlockSpec((1, 1, tn), lambda l, j: (l, 0, j))],
        out_specs=pl.BlockSpec((1, rows, tn), lambda l, j: (l, 0, j)),
        out_shape=jax.ShapeDtypeStruct((depth, rows, n), F32),
        compiler_params=_cparams(("arbitrary", "arbitrary")),
        name="ada",
    )(c_all, w_ada, b_ada.reshape(depth, 1, n))


def _normmod_kernel(x_ref, g_ref, sc_ref, sh_ref, o_ref):
    x = x_ref[0]
    y = x * lax.rsqrt(jnp.mean(x * x, axis=-1, keepdims=True) + EPS) * g_ref[...]
    o_ref[0] = (y * (1.0 + sc_ref[0]) + sh_ref[0]).astype(o_ref.dtype)


def _normmod_call(x, g, sc, sh, out_dtype, tm):
    G, R, D = x.shape
    rr = sc.shape[1]
    mod_spec = (pl.BlockSpec((1, 1, D), lambda b, i: (b, 0, 0)) if rr == 1
                else pl.BlockSpec((1, tm, D), lambda b, i: (b, i, 0)))
    return pl.pallas_call(
        _normmod_kernel,
        grid=(G, R // tm),
        in_specs=[pl.BlockSpec((1, tm, D), lambda b, i: (b, i, 0)),
                  pl.BlockSpec((1, D), lambda b, i: (0, 0)),
                  mod_spec, mod_spec],
        out_specs=pl.BlockSpec((1, tm, D), lambda b, i: (b, i, 0)),
        out_shape=jax.ShapeDtypeStruct((G, R, D), out_dtype),
        compiler_params=_cparams(("arbitrary", "arbitrary")),
        name="normmod",
    )(x, g.reshape(1, D), sc, sh)


def _rope_chunk(x, cos, sin_a, sin_b):
    return x * cos + pltpu.roll(x, 96, 1) * sin_a + pltpu.roll(x, 32, 1) * sin_b


def _inproj_kernel(h_ref, w_ref, b_ref, cos_ref, sa_ref, sb_ref, o_ref, *, tn):
    j = pl.program_id(0)
    o_ref[...] = _dot(h_ref[...], w_ref[...]) + b_ref[...]
    n_chunks = tn // LANE
    rope_tiles = -(-ROPE_COLS // tn)
    for jt in range(rope_tiles):
        chunks = min(n_chunks, (ROPE_COLS - jt * tn) // LANE)

        @pl.when(j == jt)
        def _():
            cos, sa, sb = cos_ref[...], sa_ref[...], sb_ref[...]
            for c in range(chunks):
                sl = slice(c * LANE, (c + 1) * LANE)
                o_ref[:, sl] = _rope_chunk(o_ref[:, sl], cos, sa, sb)


def _inproj_call(h, w, layer, b, cos, sin_a, sin_b, tm, tn):
    M, D = h.shape
    N = w.shape[2]
    tp = cos.shape[0] // tm
    tab = pl.BlockSpec((tm, LANE), lambda j, i: (i % tp, 0))
    return pl.pallas_call(
        functools.partial(_inproj_kernel, tn=tn),
        grid=(N // tn, M // tm),
        in_specs=[pl.BlockSpec((tm, D), lambda j, i: (i, 0)),
                  pl.BlockSpec((None, D, tn), lambda j, i: (layer, 0, j)),
                  pl.BlockSpec((1, tn), lambda j, i: (0, j)),
                  tab, tab, tab],
        out_specs=pl.BlockSpec((tm, tn), lambda j, i: (i, j)),
        out_shape=jax.ShapeDtypeStruct((M, N), F32),
        compiler_params=_cparams(("arbitrary", "arbitrary")),
        name="inproj",
    )(h, w, b, cos, sin_a, sin_b)


def _attn_prompt_kernel(q_ref, kp_ref, kc_ref, vp_ref, vc_ref, o_ref, l_ref, obuf, lbuf, *, dil, nq):
    i = pl.program_id(2)
    row = lax.broadcasted_iota(jnp.int32, (BAND, 2 * BAND), 0)
    col = lax.broadcasted_iota(jnp.int32, (BAND, 2 * BAND), 1)
    band = (col >= row) & (col <= row + BAND)
    band_first = band & ((col >= BAND) | (i > 0))
    nr = nq * BAND

    def residues(rs):
        rows_of, items = [], []
        for u, r in enumerate(rs):
            if dil == 1:
                rows, prev_rows = slice(None), slice(nr - BAND, nr)
            else:
                rows, prev_rows = pl.ds(r, nr, stride=dil), pl.ds((nr - BAND) * dil + r, BAND, stride=dil)
            rows_of.append(rows)
            q_all = q_ref[0, rows, :].astype(BF16)
            k_all = jnp.concatenate([kp_ref[0, prev_rows, :], kc_ref[0, rows, :]], axis=0).astype(BF16)
            v_all = jnp.concatenate([vp_ref[0, prev_rows, :], vc_ref[0, rows, :]], axis=0).astype(BF16)
            for sb in range(nq):
                qs = slice(sb * BAND, (sb + 1) * BAND)
                ks = slice(sb * BAND, (sb + 2) * BAND)
                for h in range(LANE // HEAD_DIM):
                    hs = slice(h * HEAD_DIM, (h + 1) * HEAD_DIM)
                    items.append((u, qs, hs, q_all[qs, hs], k_all[ks, hs], v_all[ks, hs],
                                  band_first if sb == 0 else band))
        ss = [jnp.where(it[6], _dot_nt(it[3], it[4]) * ATTN_SCALE, NEG) for it in items]
        mxs = [jnp.max(s, axis=-1, keepdims=True) for s in ss]
        es = [jnp.exp(s - mx) for s, mx in zip(ss, mxs)]
        dens = [jnp.sum(e, axis=-1, keepdims=True) for e in es]
        pvs = [_dot(e.astype(BF16), it[5]) for e, it in zip(es, items)]
        outs = [pv / den for pv, den in zip(pvs, dens)]
        lses = [jnp.broadcast_to(mx + jnp.log(den), (BAND, HEAD_DIM)) for mx, den in zip(mxs, dens)]
        for it, o, l in zip(items, outs, lses):
            obuf[it[0], it[1], it[2]] = o
            lbuf[it[0], it[1], it[2]] = l
        for u, rows in enumerate(rows_of):
            o_ref[0, rows, :] = obuf[u]
            l_ref[0, rows, :] = lbuf[u]

    if dil == 1:
        residues([0])
    elif dil <= RES_UNROLL:
        residues(list(range(dil)))
    else:
        def body(j, carry):
            residues([j * RES_UNROLL + u for u in range(RES_UNROLL)])
            return carry

        lax.fori_loop(0, dil // RES_UNROLL, body, 0)


def _attn_prompt_call(proj, g, dil):
    B, S, N = proj.shape
    nq = max(1, math.gcd(S, 512) // (BAND * dil))
    tb = nq * BAND * dil
    nb = S // tb
    hp = GROUP_W // LANE
    cq, ck, cv = (C_AQ + g * GROUP_W) // LANE, (C_AK + g * GROUP_W) // LANE, (C_AV + g * GROUP_W) // LANE

    def spec(cblk, prev):
        if prev:
            return pl.BlockSpec((1, tb, LANE), lambda b, p, i: (b, jnp.maximum(i - 1, 0), cblk + p))
        return pl.BlockSpec((1, tb, LANE), lambda b, p, i: (b, i, cblk + p))

    ospec = pl.BlockSpec((1, tb, LANE), lambda b, p, i: (b, i, p))
    o, lse = pl.pallas_call(
        functools.partial(_attn_prompt_kernel, dil=dil, nq=nq),
        grid=(B, hp, nb),
        in_specs=[spec(cq, False), spec(ck, True), spec(ck, False), spec(cv, True), spec(cv, False)],
        out_specs=[ospec, ospec],
        out_shape=[jax.ShapeDtypeStruct((B, S, GROUP_W), F32)] * 2,
        scratch_shapes=[pltpu.VMEM((min(dil, RES_UNROLL), nq * BAND, LANE), F32)] * 2,
        compiler_params=_cparams(("arbitrary", "arbitrary", "arbitrary")),
        name=f"attn_prompt_g{g}",
    )(proj, proj, proj, proj, proj)
    return o, lse


def _win_extract_kernel(x_ref, o_ref):
    o_ref[0, 0] = x_ref[0].T


def _win_extract_call(proj, g, keep):
    B, S, N = proj.shape
    chunk = min(keep, 512)
    first = (S - keep) // chunk
    cblk = C_AK // GROUP_W + g
    step = (C_AV - C_AK) // GROUP_W
    out = pl.pallas_call(
        _win_extract_kernel,
        grid=(B, 2, keep // chunk),
        in_specs=[pl.BlockSpec((1, chunk, GROUP_W), lambda b, kv, j: (b, first + j, cblk + step * kv))],
        out_specs=pl.BlockSpec((1, 1, GROUP_W, chunk), lambda b, kv, j: (b, kv, 0, j)),
        out_shape=jax.ShapeDtypeStruct((B, 2, GROUP_W, keep), F32),
        compiler_params=_cparams(("arbitrary", "arbitrary", "arbitrary")),
        name=f"win_extract_g{g}",
    )(proj)
    return out.reshape(B, 2, ATTN_HPG, HEAD_DIM, keep).transpose(0, 4, 1, 2, 3)


def _attn_mix_kernel(o0, o1, o2, l0, l1, l2, y_ref):
    a0, a1, a2 = l0[...], l1[...], l2[...]
    mx = jnp.maximum(jnp.maximum(a0, a1), a2)
    e0, e1, e2 = jnp.exp(a0 - mx), jnp.exp(a1 - mx), jnp.exp(a2 - mx)
    inv = 1.0 / (e0 + e1 + e2)
    y_ref[:, 0 * GROUP_W:1 * GROUP_W] = (o0[...] * (e0 * inv)).astype(y_ref.dtype)
    y_ref[:, 1 * GROUP_W:2 * GROUP_W] = (o1[...] * (e1 * inv)).astype(y_ref.dtype)
    y_ref[:, 2 * GROUP_W:3 * GROUP_W] = (o2[...] * (e2 * inv)).astype(y_ref.dtype)


def _attn_mix_call(os_, ls_, tm):
    M = os_[0].shape[0]
    spec = pl.BlockSpec((tm, GROUP_W), lambda i: (i, 0))
    return pl.pallas_call(
        _attn_mix_kernel,
        grid=(M // tm,),
        in_specs=[spec] * 6,
        out_specs=pl.BlockSpec((tm, ATTN_WIDTH), lambda i: (i, 0)),
        out_shape=jax.ShapeDtypeStruct((M, ATTN_WIDTH), BF16),
        compiler_params=_cparams(("arbitrary",)),
        name="attn_mix",
    )(*os_, *ls_)


QROWS = 16


def _attn_sample_kernel(qkv_ref, c0_ref, c1_ref, c2_ref, *rest, T, n_prev):
    y_ref, n0_ref, n1_ref, n2_ref, q16, npad, ysc = rest[n_prev:]
    q16[...] = jnp.zeros_like(q16)
    npad[...] = jnp.zeros_like(npad)
    q16[0:T, :] = qkv_ref[0, :, 0:ATTN_WIDTH]
    lane = lax.broadcasted_iota(jnp.int32, (HEAD_DIM, LANE), 1)
    items = []
    for g, (cref, nref) in enumerate(((c0_ref, n0_ref), (c1_ref, n1_ref), (c2_ref, n2_ref))):
        Lb = cref.shape[-1]
        dil = ATTN_GROUPS[g][1]
        npad[g, 0:T, 0:GROUP_W] = qkv_ref[0, :, C_AK + g * GROUP_W:C_AK + (g + 1) * GROUP_W]
        npad[g, 0:T, GROUP_W:2 * GROUP_W] = qkv_ref[0, :, C_AV + g * GROUP_W:C_AV + (g + 1) * GROUP_W]
        new_t = npad[g].T
        ncols = Lb + LANE
        t = lax.broadcasted_iota(jnp.int32, (QROWS, ncols), 0)
        c = lax.broadcasted_iota(jnp.int32, (QROWS, ncols), 1)
        if dil == 1:
            valid = ((c >= t) & (c < Lb)) | ((c >= Lb) & (c - Lb <= t) & (c - Lb < T))
        else:
            valid = ((c < Lb) & ((c & (dil - 1)) == t)) | (c == Lb + t)
        for h in range(ATTN_HPG):
            rk = slice(h * HEAD_DIM, (h + 1) * HEAD_DIM)
            rv = slice(GROUP_W + h * HEAD_DIM, GROUP_W + (h + 1) * HEAD_DIM)
            kc, vc = cref[0, 0, 0, h], cref[0, 0, 1, h]
            kt = jnp.concatenate([kc, new_t[rk, :]], axis=1).astype(BF16)
            vt = jnp.concatenate([vc, new_t[rv, :]], axis=1).astype(BF16)
            qh = q16[:, g * GROUP_W + h * HEAD_DIM:g * GROUP_W + (h + 1) * HEAD_DIM].astype(BF16)
            items.append((qh, kt, vt, valid))
            for kv, blk, rn in ((0, kc, rk), (1, vc, rv)):
                sh = pltpu.roll(blk, Lb - T, 1)
                newc = pltpu.roll(new_t[rn, :], LANE - T, 1)
                if Lb > LANE:
                    nref[0, 0, kv, h, :, 0:Lb - LANE] = sh[:, 0:Lb - LANE]
                nref[0, 0, kv, h, :, Lb - LANE:Lb] = jnp.where(lane >= LANE - T, newc, sh[:, Lb - LANE:Lb])
    ss = [jnp.where(valid, _dot(qh, kt) * ATTN_SCALE, NEG) for qh, kt, vt, valid in items]
    mxs = [jnp.max(s, axis=-1, keepdims=True) for s in ss]
    es = [jnp.exp(s - mx) for s, mx in zip(ss, mxs)]
    dens = [jnp.sum(e, axis=-1, keepdims=True) for e in es]
    pvs = [_dot_nt(e.astype(BF16), it[2]) for e, it in zip(es, items)]
    outs = [pv / den for pv, den in zip(pvs, dens)]
    lses = [mx + jnp.log(den) for mx, den in zip(mxs, dens)]
    for h in range(ATTN_HPG):
        l0, l1, l2 = lses[h], lses[ATTN_HPG + h], lses[2 * ATTN_HPG + h]
        mx = jnp.maximum(jnp.maximum(l0, l1), l2)
        es = [jnp.exp(l0 - mx), jnp.exp(l1 - mx), jnp.exp(l2 - mx)]
        inv = 1.0 / (es[0] + es[1] + es[2])
        for g in range(3):
            c0 = g * GROUP_W + h * HEAD_DIM
            ysc[:, c0:c0 + HEAD_DIM] = outs[g * ATTN_HPG + h] * (es[g] * inv)
    y_ref[0] = ysc[0:T, :].astype(y_ref.dtype)


def _attn_sample_call(qkv, caches_t, layer, prev):
    B, T, _ = qkv.shape
    depth = caches_t[0].shape[0]
    for g, (win, dil) in enumerate(ATTN_GROUPS):
        assert caches_t[g].shape[-1] == win and win // dil == BAND and (g == 0 or T <= dil), \
            "sample attention assumes full window caches"
    cspecs = [pl.BlockSpec((1, 1) + c.shape[2:], lambda b: (layer, b, 0, 0, 0, 0)) for c in caches_t]
    n_prev = 0 if prev is None else 3
    prev_args = [] if prev is None else list(prev)
    prev_specs = [pl.BlockSpec(memory_space=pl.ANY)] * n_prev
    aliases = {} if prev is None else {4 + g: 1 + g for g in range(3)}
    outs = pl.pallas_call(
        functools.partial(_attn_sample_kernel, T=T, n_prev=n_prev),
        grid=(B,),
        in_specs=[pl.BlockSpec((1, T, 3 * ATTN_WIDTH), lambda b: (b, 0, 0))] + cspecs + prev_specs,
        out_specs=[pl.BlockSpec((1, T, ATTN_WIDTH), lambda b: (b, 0, 0))] + cspecs,
        out_shape=[jax.ShapeDtypeStruct((B, T, ATTN_WIDTH), BF16)]
        + [jax.ShapeDtypeStruct(c.shape, F32) for c in caches_t],
        scratch_shapes=[pltpu.VMEM((QROWS, ATTN_WIDTH), F32),
                        pltpu.VMEM((3, LANE, 2 * GROUP_W), F32),
                        pltpu.VMEM((QROWS, ATTN_WIDTH), F32)],
        input_output_aliases=aliases,
        compiler_params=_cparams(("arbitrary",)),
        name="attn_sample",
    )(qkv, *caches_t, *prev_args)
    return outs[0], outs[1:]


def _mlstm_kernel(mqk_ref, mv_ref, mi_ref, mf_ref, mo_ref, cst_ref, wc_ref, bc_ref, c0_ref, n0_ref, m0_ref,
                  g_ref, tri_ref, y_ref, c_ref, n_ref, m_ref, cso_ref,
                  xbuf, vbuf, cs, ns, ms, tpad, kwp, *, T, Lp, nb):
    c = pl.program_id(1)
    last = pl.num_programs(1) - 1
    W = MLSTM_WIDTH
    bs = range(nb)
    heads = range(MLSTM_HEADS)
    hsl = [slice(h * HEAD_DIM, (h + 1) * HEAD_DIM) for h in heads]
    pairs = [(b, h) for b in bs for h in heads]

    @pl.when(c == 0)
    def _():
        xbuf[...] = jnp.zeros_like(xbuf)
        vbuf[...] = jnp.zeros_like(vbuf)
        tpad[...] = jnp.zeros_like(tpad)
        kwp[...] = jnp.zeros_like(kwp)
        xbuf[:, 5:8, :] = cst_ref[...]
        cs[...] = c0_ref[...]
        ns[...] = n0_ref[...]
        ms[...] = m0_ref[...]

    w = wc_ref[...]
    tri = tri_ref[...]
    causal = tri > 0.5
    rowid = lax.broadcasted_iota(jnp.int32, (Lp, LANE), 0)
    real = rowid < T
    lane_row = lax.broadcasted_iota(jnp.int32, (1, LANE), 1)

    for b in bs:
        xbuf[b, 8:8 + T, :] = mqk_ref[b]
        vbuf[b, 0:T, :] = mv_ref[b]
    ys = [bc_ref[...] + xbuf[b, 8:8 + Lp, :] * w[3:4] + xbuf[b, 7:7 + Lp, :] * w[2:3]
          + xbuf[b, 6:6 + Lp, :] * w[1:2] + xbuf[b, 5:5 + Lp, :] * w[0:1] for b in bs]
    tails = [xbuf[b, 8 + T - 3:8 + T, :] for b in bs]
    for b in bs:
        xbuf[b, 5:8, :] = tails[b]

    @pl.when(c == last)
    def _():
        for b in bs:
            cso_ref[b] = tails[b]

    qks = [y * _sigmoid(y) for y in ys]
    igs, lfs = [], []
    for b in bs:
        tpad[b, 0:T, :] = mi_ref[b]
        igs.append(jnp.where(real, tpad[b, 0:Lp, :], NEG))
        tpad[b, 0:T, :] = mf_ref[b]
        lfs.append(jnp.where(real, _log_sigmoid(tpad[b, 0:Lp, :]), 0.0))
    Fs = [_dot_sel(tri, lf) for lf in lfs]
    m_rows = [ms[b] for b in bs]
    inter_all = [Fs[b] + m_rows[b] for b in bs]
    FTs, ITs = [], []
    for b in bs:
        tpad[b, 0:Lp, :] = Fs[b]
        FTs.append(tpad[b].T)
        tpad[b, 0:Lp, :] = igs[b]
        ITs.append(tpad[b].T)
    v_alls = [vbuf[b, 0:Lp, :] for b in bs]
    n_alls = [ns[b] for b in bs]

    qs = [qks[b][:, hsl[h]] for b, h in pairs]
    ks = [qks[b][:, W + h * HEAD_DIM:W + (h + 1) * HEAD_DIM] * (HEAD_DIM ** -0.5) for b, h in pairs]
    qbs = [q.astype(BF16) for q in qs]
    vbs = [v_alls[b][:, hsl[h]].astype(BF16) for b, h in pairs]
    sqk = [_dot_nt(qb, k.astype(BF16)) for qb, k in zip(qbs, ks)]
    qcs = [_dot(qbs[i], cs[b, h].astype(BF16)) for i, (b, h) in enumerate(pairs)]
    nrows = [n_alls[b][h:h + 1, :] for b, h in pairs]
    qn = [_dot_nt(qbs[i], jnp.broadcast_to(nrows[i], (16, HEAD_DIM)).astype(BF16))[:, 0:1]
          for i in range(len(pairs))]

    fcols = [Fs[b][:, h:h + 1] for b, h in pairs]
    icols = [igs[b][:, h:h + 1] for b, h in pairs]
    inters = [inter_all[b][:, h:h + 1] for b, h in pairs]
    logws = [jnp.where(causal, fcols[i] - FTs[b][h:h + 1, 0:Lp] + ITs[b][h:h + 1, 0:Lp], NEG)
             for i, (b, h) in enumerate(pairs)]
    rmax = [jnp.max(lw, axis=-1, keepdims=True) for lw in logws]
    mts = [jnp.maximum(r, it) for r, it in zip(rmax, inters)]
    As = [jnp.exp(lw - mt) * s for lw, mt, s in zip(logws, mts, sqk)]
    gqs = [jnp.exp(it - mt) for it, mt in zip(inters, mts)]
    asum = [jnp.sum(A, axis=-1, keepdims=True) for A in As]
    dens = [a + g * q for a, g, q in zip(asum, gqs, qn)]
    a_bf = [A.astype(BF16) for A in As]
    avs = [_dot(a, v) for a, v in zip(a_bf, vbs)]
    mLs = [mt[Lp - 1:Lp, :] for mt in mts]
    flast = [Fs[b][Lp - 1:Lp, h:h + 1] for b, h in pairs]
    wLs = [jnp.exp(fl - fc + ic - mL) for fl, fc, ic, mL in zip(flast, fcols, icols, mLs)]
    gls = [jnp.exp(flast[i] + m_rows[b][0:1, h:h + 1] - mLs[i]) for i, (b, h) in enumerate(pairs)]
    kws = [k * wl for k, wl in zip(ks, wLs)]
    for i, (b, h) in enumerate(pairs):
        kwp[b, 0:Lp, hsl[h]] = kws[i]
    ksum = [jnp.sum(kw, axis=0, keepdims=True) for kw in kws]
    for i, (b, h) in enumerate(pairs):
        ns[b, h:h + 1, :] = gls[i] * nrows[i] + ksum[i]
    for b in bs:
        m_new = m_rows[b]
        for h in heads:
            m_new = jnp.where(lane_row == h, mLs[b * MLSTM_HEADS + h], m_new)
        ms[b] = m_new

    kwts = [kwp[b].T for b in bs]
    upds = [_dot(kwts[b][hsl[h], 0:Lp].astype(BF16), vbs[i]) for i, (b, h) in enumerate(pairs)]

    nums = [av + g * qc for av, g, qc in zip(avs, gqs, qcs)]
    hhs = [nu / jnp.maximum(jnp.abs(de), jnp.exp(-mt)) for nu, de, mt in zip(nums, dens, mts)]
    msq = [jnp.mean(hh * hh, axis=-1, keepdims=True) for hh in hhs]
    hns = [hhs[i] * lax.rsqrt(msq[i] + EPS) * g_ref[0:1, hsl[h]] for i, (b, h) in enumerate(pairs)]
    ogs = [_sigmoid(mo_ref[b, :, hsl[h]]) for b, h in pairs]
    for i, (b, h) in enumerate(pairs):
        yo = hns[i] * ogs[i] if Lp == T else hns[i][0:T] * ogs[i]
        y_ref[b, :, hsl[h]] = yo.astype(y_ref.dtype)
    for i, (b, h) in enumerate(pairs):
        cs[b, h] = gls[i] * cs[b, h] + upds[i]

    @pl.when(c == last)
    def _():
        c_ref[...] = cs[...]
        n_ref[...] = ns[...]
        m_ref[...] = ms[...]


def _mlstm_call(proj, conv_state, w_conv, b_conv, c0, n0, m0, g_mlstm, chunk):
    B, S, N = proj.shape
    T = chunk
    nc = S // T
    Lp = max(16, T)
    H = MLSTM_HEADS
    nb = math.gcd(B, MIXER_BATCH)
    m0p = jnp.pad(m0, ((0, 0), (0, LANE - H))).reshape(B, 1, LANE)
    tri = jnp.tril(jnp.ones((Lp, Lp), F32))

    def col(width, off):
        blk = off // width
        return pl.BlockSpec((nb, T, width), lambda b, c: (b, c, blk))

    def const(shape):
        nd = len(shape)
        return pl.BlockSpec(shape, lambda b, c: (0,) * nd)

    def per_b(shape):
        nd = len(shape)
        return pl.BlockSpec((nb,) + shape, lambda b, c: (b,) + (0,) * nd)

    outs = pl.pallas_call(
        functools.partial(_mlstm_kernel, T=T, Lp=Lp, nb=nb),
        grid=(B // nb, nc),
        in_specs=[col(2 * MLSTM_WIDTH, C_MQK), col(MLSTM_WIDTH, C_MV), col(LANE, C_MI), col(LANE, C_MF),
                  col(MLSTM_WIDTH, C_MO), per_b((MLSTM_CONV - 1, 2 * MLSTM_WIDTH)),
                  const((MLSTM_CONV, 2 * MLSTM_WIDTH)), const((1, 2 * MLSTM_WIDTH)),
                  per_b((H, HEAD_DIM, HEAD_DIM)), per_b((H, HEAD_DIM)), per_b((1, LANE)),
                  const((1, MLSTM_WIDTH)), const((Lp, Lp))],
        out_specs=[pl.BlockSpec((nb, T, MLSTM_WIDTH), lambda b, c: (b, c, 0)),
                   per_b((H, HEAD_DIM, HEAD_DIM)), per_b((H, HEAD_DIM)), per_b((1, LANE)),
                   per_b((MLSTM_CONV - 1, 2 * MLSTM_WIDTH))],
        out_shape=[jax.ShapeDtypeStruct((B, S, MLSTM_WIDTH), BF16),
                   jax.ShapeDtypeStruct((B, H, HEAD_DIM, HEAD_DIM), F32),
                   jax.ShapeDtypeStruct((B, H, HEAD_DIM), F32),
                   jax.ShapeDtypeStruct((B, 1, LANE), F32),
                   jax.ShapeDtypeStruct((B, MLSTM_CONV - 1, 2 * MLSTM_WIDTH), F32)],
        scratch_shapes=[pltpu.VMEM((nb, 8 + Lp, 2 * MLSTM_WIDTH), F32),
                        pltpu.VMEM((nb, Lp, MLSTM_WIDTH), F32),
                        pltpu.VMEM((nb, H, HEAD_DIM, HEAD_DIM), F32),
                        pltpu.VMEM((nb, H, HEAD_DIM), F32),
                        pltpu.VMEM((nb, 1, LANE), F32),
                        pltpu.VMEM((nb, LANE, LANE), F32),
                        pltpu.VMEM((nb, LANE, MLSTM_WIDTH), F32)],
        compiler_params=_cparams(("arbitrary", "arbitrary")),
        name="mlstm",
    )(proj, proj, proj, proj, proj, conv_state, w_conv, b_conv.reshape(1, -1), c0, n0, m0p,
      g_mlstm.reshape(1, -1), tri)
    y, C, n, m, cso = outs
    return y, C, n, m[:, 0, :H], cso


def _gla_kernel(gqk_ref, gv_ref, gg_ref, ga_ref, wa_ref, wat_ref, s0_ref, g_ref, tri_ref, bd_ref, ee_ref,
                y_ref, s_ref, qkp, vp, gap, sbd, osc, *, T, Lp, nb):
    c = pl.program_id(1)
    last = pl.num_programs(1) - 1
    KW, VW = GLA_K_WIDTH, GLA_V_WIDTH
    bs = range(nb)
    heads = range(GLA_HEADS)

    @pl.when(c == 0)
    def _():
        qkp[...] = jnp.zeros_like(qkp)
        vp[...] = jnp.zeros_like(vp)
        gap[...] = jnp.zeros_like(gap)
        sbd[...] = jnp.zeros_like(sbd)
        osc[...] = jnp.zeros_like(osc)
        for b in bs:
            for h in heads:
                sbd[b, h * GLA_DK:(h + 1) * GLA_DK, h * GLA_DV:(h + 1) * GLA_DV] = s0_ref[b, h]

    lane = lax.broadcasted_iota(jnp.int32, (T, LANE), 1)
    for b in bs:
        qkp[b, 0:T, :] = gqk_ref[b]
        vp[b, 0:T, :] = gv_ref[b]
        gap[b, 0:T, :] = jnp.where(lane == GLA_RANK, 1.0, ga_ref[b])

    tri = tri_ref[...]
    rowid = lax.broadcasted_iota(jnp.int32, (Lp, KW), 0)
    colid = lax.broadcasted_iota(jnp.int32, (KW, LANE), 1)
    wa_hi, wa_lo, _ = _split3(wa_ref[...])
    wat_hi, wat_lo, _ = _split3(wat_ref[...])
    ga_bs = [gap[b, 0:Lp, :].astype(BF16) for b in bs]
    gat_bs = [gap[b].T.astype(BF16) for b in bs]
    las = [_dot(g, wa_hi) + _dot(g, wa_lo) for g in ga_bs]
    las = [jnp.where(rowid < T, _log_sigmoid(la) / GLA_TAU, 0.0) for la in las]
    bcs = [_dot_sel(tri[0:Lp, 0:Lp], la) for la in las]
    lats = [_dot(wat_hi, g) + _dot(wat_lo, g) for g in gat_bs]
    lats = [jnp.where(colid < T, _log_sigmoid(lat) / GLA_TAU, 0.0) for lat in lats]
    bcts = [_dot_sel_nt(lat, tri) for lat in lats]
    blcols = [bct[:, LANE - 1:LANE] for bct in bcts]
    blrows = [bc[Lp - 1:Lp, :] for bc in bcs]
    kts = [qkp[b].T[KW:2 * KW, :] for b in bs]
    bd = bd_ref[...]
    min_decay = jnp.min(blrows[0])
    for b in bs[1:]:
        min_decay = jnp.minimum(min_decay, jnp.min(blrows[b]))
    safe = min_decay >= -GLA_SAFE_DECAY

    @pl.when(safe)
    def _():
        causal = tri[0:Lp, 0:Lp] > 0.5
        qbs = [(qkp[b, 0:Lp, 0:KW] * (GLA_DK ** -0.5) * jnp.exp(bcs[b])).astype(BF16) for b in bs]
        kbs = [(qkp[b, 0:Lp, KW:2 * KW] * jnp.exp(-bcs[b])).astype(BF16) for b in bs]
        vbs = [vp[b, 0:Lp, :].astype(BF16) for b in bs]
        o_inters = [_dot(qbs[b], sbd[b].astype(BF16)) for b in bs]
        pairs = [(b, h) for b in bs for h in heads]
        scores = [_dot_nt(qbs[b][:, h * GLA_DK:(h + 1) * GLA_DK], kbs[b][:, h * GLA_DK:(h + 1) * GLA_DK])
                  for b, h in pairs]
        a_bf = [jnp.where(causal, s, 0.0).astype(BF16) for s in scores]
        ovs = [_dot(a_bf[i], vbs[b][:, h * GLA_DV:(h + 1) * GLA_DV]) for i, (b, h) in enumerate(pairs)]
        klts = [(kts[b] * jnp.exp(blcols[b] - bcts[b])).astype(BF16) for b in bs]
        upds = [_dot(klts[b], vp[b].astype(BF16)) for b in bs]
        for b in bs:
            osc[b, 0:Lp, :] = o_inters[b] + jnp.concatenate(ovs[b * GLA_HEADS:(b + 1) * GLA_HEADS], axis=1)
            sbd[b] = bd * (jnp.exp(blcols[b]) * sbd[b] + upds[b])

    @pl.when(jnp.logical_not(safe))
    def _():
        srow = lax.broadcasted_iota(jnp.int32, (LANE, LANE), 0)
        for b in bs:
            def body(t, carry, b=b):
                sel = (srow == t).astype(F32)
                lac = jnp.dot(lats[b], sel, precision=HI, preferred_element_type=F32)
                kc = jnp.dot(kts[b], sel, precision=HI, preferred_element_type=F32)
                dec = jnp.concatenate([jnp.exp(lac)] * (VW // LANE), axis=1)
                kcw = jnp.concatenate([kc] * (VW // LANE), axis=1)
                vrow = vp[b, pl.ds(t, 1), :]
                snew = bd * (dec * sbd[b] + kcw * vrow)
                sbd[b] = snew
                qrow = jnp.broadcast_to(qkp[b, pl.ds(t, 1), 0:KW] * (GLA_DK ** -0.5), (8, KW))
                orow = jnp.dot(qrow, snew, precision=HI, preferred_element_type=F32)
                osc[b, pl.ds(t, 1), :] = orow[0:1]
                return carry

            lax.fori_loop(0, T, body, 0)

    os_ = [osc[b, 0:Lp, :] for b in bs]
    mss = [_dot_sel_rhs(o * o, ee_ref[...]) for o in os_]
    ogs = [o * lax.rsqrt(ms + EPS) * g_ref[...] for o, ms in zip(os_, mss)]
    for b in bs:
        gg = gg_ref[b]
        yo = (ogs[b] if Lp == T else ogs[b][0:T]) * (gg * _sigmoid(gg))
        y_ref[b] = yo.astype(y_ref.dtype)

    @pl.when(c == last)
    def _():
        for b in bs:
            for h in heads:
                s_ref[b, h] = sbd[b, h * GLA_DK:(h + 1) * GLA_DK, h * GLA_DV:(h + 1) * GLA_DV]


def _gla_call(proj, w_a2, b_a2, s0, g_gla, chunk):
    B, S, N = proj.shape
    T = chunk
    nc = S // T
    Lp = max(16, T)
    H, KW, VW = GLA_HEADS, GLA_K_WIDTH, GLA_V_WIDTH
    wa = jnp.zeros((LANE, KW), F32).at[:GLA_RANK].set(w_a2).at[GLA_RANK].set(b_a2)
    tri = jnp.tril(jnp.ones((LANE, LANE), F32))
    hk = jnp.arange(KW) // GLA_DK
    hv = jnp.arange(VW) // GLA_DV
    bd = (hk[:, None] == hv[None, :]).astype(F32)
    ee = (hv[:, None] == hv[None, :]).astype(F32) / GLA_DV

    nb = math.gcd(B, MIXER_BATCH)

    def col(width, off):
        blk = off // width
        return pl.BlockSpec((nb, T, width), lambda b, c: (b, c, blk))

    def const(shape):
        nd = len(shape)
        return pl.BlockSpec(shape, lambda b, c: (0,) * nd)

    y, s = pl.pallas_call(
        functools.partial(_gla_kernel, T=T, Lp=Lp, nb=nb),
        grid=(B // nb, nc),
        in_specs=[col(2 * KW, C_GQK), col(VW, C_GV), col(VW, C_GG), col(LANE, C_GA),
                  const((LANE, KW)), const((KW, LANE)),
                  pl.BlockSpec((nb, H, GLA_DK, GLA_DV), lambda b, c: (b, 0, 0, 0)),
                  const((1, VW)), const((LANE, LANE)), const((KW, VW)), const((VW, VW))],
        out_specs=[pl.BlockSpec((nb, T, VW), lambda b, c: (b, c, 0)),
                   pl.BlockSpec((nb, H, GLA_DK, GLA_DV), lambda b, c: (b, 0, 0, 0))],
        out_shape=[jax.ShapeDtypeStruct((B, S, VW), BF16),
                   jax.ShapeDtypeStruct((B, H, GLA_DK, GLA_DV), F32)],
        scratch_shapes=[pltpu.VMEM((nb, LANE, 2 * KW), F32),
                        pltpu.VMEM((nb, LANE, VW), F32),
                        pltpu.VMEM((nb, LANE, LANE), F32),
                        pltpu.VMEM((nb, KW, VW), F32),
                        pltpu.VMEM((nb, LANE, VW), F32)],
        compiler_params=_cparams(("arbitrary", "arbitrary")),
        name="gla",
    )(proj, proj, proj, proj, wa, wa.T, s0, g_gla.reshape(1, -1), tri, bd, ee)
    return y, s


def _outproj_kernel(ya_ref, ym_ref, yg_ref, w_ref, x_ref, gt_ref, g2_ref, sc_ref, sh_ref, xo_ref, h2_ref):
    a, b = ATTN_WIDTH, ATTN_WIDTH + MLSTM_WIDTH
    acc = _dot(ya_ref[0], w_ref[0:a, :])
    acc = acc + _dot(ym_ref[0], w_ref[a:b, :])
    acc = acc + _dot(yg_ref[0], w_ref[b:, :])
    x = x_ref[0] + gt_ref[0] * acc
    xo_ref[0] = x
    y = x * lax.rsqrt(jnp.mean(x * x, axis=-1, keepdims=True) + EPS) * g2_ref[...]
    h2_ref[0] = (y * (1.0 + sc_ref[0]) + sh_ref[0]).astype(h2_ref.dtype)


def _outproj_call(ya, ym, yg, w_out, layer, x, gt, g2, sc, sh, tm):
    G, R, D = x.shape
    rr = gt.shape[1]
    mod_spec = (pl.BlockSpec((1, 1, D), lambda b, i: (b, 0, 0)) if rr == 1
                else pl.BlockSpec((1, tm, D), lambda b, i: (b, i, 0)))

    def act(width):
        return pl.BlockSpec((1, tm, width), lambda b, i: (b, i, 0))

    return pl.pallas_call(
        _outproj_kernel,
        grid=(G, R // tm),
        in_specs=[act(ATTN_WIDTH), act(MLSTM_WIDTH), act(GLA_V_WIDTH),
                  pl.BlockSpec((None,) + w_out.shape[1:], lambda b, i: (layer, 0, 0)),
                  act(D), mod_spec, pl.BlockSpec((1, D), lambda b, i: (0, 0)), mod_spec, mod_spec],
        out_specs=[act(D), act(D)],
        out_shape=[jax.ShapeDtypeStruct((G, R, D), F32), jax.ShapeDtypeStruct((G, R, D), BF16)],
        compiler_params=_cparams(("arbitrary", "arbitrary")),
        name="outproj",
    )(ya, ym, yg, w_out, x, gt, g2.reshape(1, D), sc, sh)


def _ffn_kernel(h_ref, wg_ref, wu_ref, wc_ref, bc_ref, wo_ref, x_ref, gt_ref, init_ref,
                xo_ref, st_ref, ubuf, cbuf, *, tm, u, R, tiles_per_seq):
    m = pl.program_id(1)
    f = pl.program_id(2)
    nf = pl.num_programs(2)
    h = h_ref[0]

    @pl.when(m % tiles_per_seq == 0)
    def _():
        ubuf[0:R, :] = init_ref[0]

    @pl.when(m % tiles_per_seq != 0)
    def _():
        ubuf[0:R, :] = cbuf[f]

    @pl.when(f == 0)
    def _():
        xo_ref[0] = jnp.zeros_like(xo_ref[0])

    tf = wg_ref.shape[1]
    chunks = [slice(c0, min(c0 + FFN_SUB, tf)) for c0 in range(0, tf, FFN_SUB)]

    def up_matmuls(cs_):
        return _dot(h, wg_ref[:, cs_]), _dot(h, wu_ref[:, cs_])

    pending = up_matmuls(chunks[0])
    acc = xo_ref[0]
    for ci, cs_ in enumerate(chunks):
        ug, uu = pending
        if ci + 1 < len(chunks):
            pending = up_matmuls(chunks[ci + 1])
        ubuf[R:R + tm, cs_] = ug
        w = wc_ref[:, cs_]
        gate = (bc_ref[:, cs_] + ug * w[2:3] + ubuf[R - u:R - u + tm, cs_] * w[1:2]
                + ubuf[R - 2 * u:R - 2 * u + tm, cs_] * w[0:1])
        act = (gate * _sigmoid(gate) * uu).astype(BF16)
        acc = _dot(act, wo_ref[cs_, :]) + acc
    xo_ref[0] = acc
    tail = ubuf[tm:tm + R, :]
    cbuf[f] = tail
    st_ref[0, 0] = tail

    @pl.when(f == nf - 1)
    def _():
        xo_ref[0] = x_ref[0] + gt_ref[0] * xo_ref[0]


def _ffn_call(h2, w_gu, w_conv, b_conv, w_o, layer, x, gt, init, tm, tf, u):
    G, rows, D = x.shape
    F = w_gu.shape[2] // 2
    R = max(8, 2 * u)
    nm, nf = rows // tm, F // tf
    rr = gt.shape[1]
    mod_spec = (pl.BlockSpec((1, 1, D), lambda b, i, f: (b, 0, 0)) if rr == 1
                else pl.BlockSpec((1, tm, D), lambda b, i, f: (b, i, 0)))
    xo, st = pl.pallas_call(
        functools.partial(_ffn_kernel, tm=tm, u=u, R=R, tiles_per_seq=nm),
        grid=(G, nm, nf),
        in_specs=[pl.BlockSpec((1, tm, D), lambda b, i, f: (b, i, 0)),
                  pl.BlockSpec((None, D, tf), lambda b, i, f: (layer, 0, f)),
                  pl.BlockSpec((None, D, tf), lambda b, i, f: (layer, 0, nf + f)),
                  pl.BlockSpec((FFN_CONV, tf), lambda b, i, f: (0, f)),
                  pl.BlockSpec((1, tf), lambda b, i, f: (0, f)),
                  pl.BlockSpec((None, tf, D), lambda b, i, f: (layer, f, 0)),
                  pl.BlockSpec((1, tm, D), lambda b, i, f: (b, i, 0), pipeline_mode=pl.Buffered(1)),
                  mod_spec,
                  pl.BlockSpec((1, R, tf), lambda b, i, f: (b, 0, f))],
        out_specs=[pl.BlockSpec((1, tm, D), lambda b, i, f: (b, i, 0)),
                   pl.BlockSpec((1, 1, R, tf), lambda b, i, f: (b, i, 0, f))],
        out_shape=[jax.ShapeDtypeStruct((G, rows, D), F32),
                   jax.ShapeDtypeStruct((G, nm, R, F), F32)],
        scratch_shapes=[pltpu.VMEM((R + tm, tf), F32), pltpu.VMEM((nf, R, tf), F32)],
        compiler_params=_cparams(("arbitrary", "arbitrary", "arbitrary")),
        name="ffn",
    )(h2, w_gu, w_gu, w_conv, b_conv.reshape(1, F), w_o, x, gt, init)
    return xo, st[:, nm - 1]


def _pack_w_kernel(w_ref, o_ref):
    x = w_ref[0]
    rows = x.shape[0]
    n_in = x.shape[1]
    attn = 3 * ATTN_WIDTH
    gates = attn + 3 * MLSTM_WIDTH
    tail = gates + 2 * MLSTM_HEADS
    o_ref[0, :, 0:attn] = x[:, 0:attn].astype(BF16)
    lane = lax.broadcasted_iota(jnp.int32, (rows, LANE), 1)
    gwin = x[:, gates:gates + LANE]
    o_ref[0, :, C_MI:C_MI + LANE] = jnp.where(lane < MLSTM_HEADS, gwin, 0.0).astype(BF16)
    o_ref[0, :, C_MF:C_MF + LANE] = jnp.where(lane < MLSTM_HEADS, pltpu.roll(gwin, LANE - MLSTM_HEADS, 1),
                                              0.0).astype(BF16)
    o_ref[0, :, C_MQK:C_MQK + 3 * MLSTM_WIDTH] = x[:, attn:gates].astype(BF16)
    o_ref[0, :, C_GA:N_PACK] = jnp.zeros((rows, N_PACK - C_GA), BF16)
    o_ref[0, :, C_MO:C_MO + n_in - tail] = x[:, tail:n_in].astype(BF16)


def _pack_w_call(w_in):
    depth, d, n_in = w_in.shape
    tk = 256
    return pl.pallas_call(
        _pack_w_kernel,
        grid=(depth, d // tk),
        in_specs=[pl.BlockSpec((1, tk, n_in), lambda l, i: (l, i, 0))],
        out_specs=pl.BlockSpec((1, tk, N_PACK), lambda l, i: (l, i, 0)),
        out_shape=jax.ShapeDtypeStruct((depth, d, N_PACK), BF16),
        compiler_params=_cparams(("arbitrary", "arbitrary")),
        name="pack_w_in",
    )(w_in)


def _pack_in_proj(w_in, b_in):
    def split(a):
        out, off = [], 0
        for s in IN_SPLITS:
            out.append(a[..., off:off + s])
            off += s
        return out

    def pad(a, n):
        return jnp.pad(a, [(0, 0)] * (a.ndim - 1) + [(0, n - a.shape[-1])])

    def pack(a):
        aq, ak, av, mqk, mv, mi, mf, mo, gq, gk, gv, gg, ga = split(a)
        return jnp.concatenate([aq, ak, av, pad(mi, LANE), pad(mf, LANE), mqk, mv, mo, gq, gk, gv, gg,
                                pad(ga, LANE)], axis=-1)

    return _pack_w_call(w_in), pack(b_in)


def _rope_tables(pos):
    half = HEAD_DIM // 2
    inv_freq = jnp.power(ROPE_THETA, -jnp.arange(half, dtype=F32) / half)
    ang = pos.astype(F32)[:, None] * inv_freq[None, :]
    cos, sin = jnp.cos(ang), jnp.sin(ang)
    zero = jnp.zeros_like(sin)
    reps = LANE // HEAD_DIM
    return (jnp.tile(jnp.concatenate([cos, cos], -1), (1, reps)),
            jnp.tile(jnp.concatenate([-sin, zero], -1), (1, reps)),
            jnp.tile(jnp.concatenate([zero, sin], -1), (1, reps)))


def _pick_tile(n, pref):
    t = math.gcd(n, pref)
    return t


def kernel(x_prompt, x_sample, c_prompt, c_sample, cache_win0_kv, cache_win1_kv, cache_win2_kv, state_mlstm_C, state_mlstm_n, state_mlstm_m, state_mlstm_conv, state_gla_S, state_ffn_conv, w_ada, b_ada, g_norm1, g_norm2, w_in, b_in, w_mconv, b_mconv, g_mlstm, w_gla_a2, b_gla_a2, g_gla, w_out, w_ff_in, w_fconv, b_fconv, w_ff_out, g_final):
    B, S, D = x_prompt.shape
    Bs, Ts, _ = x_sample.shape
    depth = w_ada.shape[0]
    d_ff = w_fconv.shape[-1]
    caches = (cache_win0_kv, cache_win1_kv, cache_win2_kv)
    Ms = Bs * Ts

    n_c = B + Bs
    rows_c = -(-n_c // 8) * 8
    c_all = jnp.pad(jnp.concatenate([c_prompt, c_sample], axis=0), ((0, rows_c - n_c), (0, 0)))
    mod = _ada_call(c_all, w_ada, b_ada).reshape(depth, rows_c, 6, D)

    w_in_p, b_in_p = _pack_in_proj(w_in, b_in)
    w_out_b = w_out.astype(BF16)
    w_gu = w_ff_in.astype(BF16)
    w_o = w_ff_out.astype(BF16)

    rope_p = _rope_tables(jnp.arange(S))
    rope_s = _rope_tables(PAST_LEN + jnp.repeat(jnp.arange(Ts), Bs))

    tm_p = _pick_tile(S, 512)
    tm_in = _pick_tile(S, 1024)
    tm_ff = _pick_tile(S, 1024)
    tn = 1024
    tf = _pick_tile(d_ff, 512)

    xp = x_prompt
    xs = x_sample.transpose(1, 0, 2).reshape(1, Ms, D)
    zeros_p = {
        'mconv': jnp.zeros((B, MLSTM_CONV - 1, 2 * MLSTM_WIDTH), F32),
        'C': jnp.zeros((B, MLSTM_HEADS, HEAD_DIM, HEAD_DIM), F32),
        'n': jnp.zeros((B, MLSTM_HEADS, HEAD_DIM), F32),
        'm': jnp.zeros((B, MLSTM_HEADS), F32),
        'S': jnp.zeros((B, GLA_HEADS, GLA_DK, GLA_DV), F32),
        'fconv': jnp.zeros((B, 8, d_ff), F32),
    }
    names = ('win0', 'win1', 'win2', 'C', 'n', 'm', 'mconv', 'S', 'fconv')
    col_p = {k: [] for k in names}
    col_s = {k: [] for k in names}
    caches_t = [c.transpose(0, 1, 3, 4, 5, 2) for c in caches]
    new_caches = None

    for l in range(depth):
        mp = mod[l, :B]
        ms_ = jnp.tile(mod[l, B:B + Bs], (Ts, 1, 1))

        def mods_p(i):
            return mp[:, i:i + 1, :]

        def mods_s(i):
            return ms_[None, :, i, :]

        h = _normmod_call(xp, g_norm1[l], mods_p(1), mods_p(0), BF16, tm_p)
        proj = _inproj_call(h.reshape(B * S, D), w_in_p, l, b_in_p[l][None], *rope_p, tm_in, tn)
        proj = proj.reshape(B, S, N_PACK)
        os_, ls_ = [], []
        for g, (win, dil) in enumerate(ATTN_GROUPS):
            o, lse = _attn_prompt_call(proj, g, dil)
            os_.append(o.reshape(B * S, GROUP_W))
            ls_.append(lse.reshape(B * S, GROUP_W))
            col_p[f'win{g}'].append(_win_extract_call(proj, g, min(win, S)))
        ya = _attn_mix_call(os_, ls_, tm_p).reshape(B, S, ATTN_WIDTH)
        ym, C, n, m, cso = _mlstm_call(proj, zeros_p['mconv'], w_mconv[l], b_mconv[l], zeros_p['C'],
                                        zeros_p['n'], zeros_p['m'], g_mlstm[l], math.gcd(S, MLSTM_CHUNK))
        yg, Sg = _gla_call(proj, w_gla_a2[l], b_gla_a2[l], zeros_p['S'], g_gla[l], math.gcd(S, GLA_CHUNK))
        xp, h2 = _outproj_call(ya, ym, yg, w_out_b, l, xp, mods_p(2), g_norm2[l], mods_p(4), mods_p(3), tm_p)
        xp, fst = _ffn_call(h2, w_gu, w_fconv[l], b_fconv[l], w_o, l, xp, mods_p(5),
                            zeros_p['fconv'], tm_ff, tf, 1)
        for k_, v_ in (('C', C), ('n', n), ('m', m), ('mconv', cso), ('S', Sg), ('fconv', fst[:, 6:8])):
            col_p[k_].append(v_)

        h = _normmod_call(xs, g_norm1[l], mods_s(1), mods_s(0), BF16, Ms)
        proj = _inproj_call(h.reshape(Ms, D), w_in_p, l, b_in_p[l][None], *rope_s, Ms, tn)
        proj_b = proj.reshape(Ts, Bs, N_PACK).transpose(1, 0, 2)
        ya, new_caches = _attn_sample_call(proj_b[:, :, :3 * ATTN_WIDTH], caches_t, l, new_caches)
        ym, C, n, m, cso = _mlstm_call(proj_b, state_mlstm_conv[l], w_mconv[l], b_mconv[l], state_mlstm_C[l],
                                        state_mlstm_n[l], state_mlstm_m[l], g_mlstm[l], Ts)
        yg, Sg = _gla_call(proj_b, w_gla_a2[l], b_gla_a2[l], state_gla_S[l], g_gla[l], Ts)

        def tmaj(a):
            return a.transpose(1, 0, 2).reshape(1, Ms, a.shape[-1])

        xs, h2 = _outproj_call(tmaj(ya), tmaj(ym), tmaj(yg), w_out_b, l, xs, mods_s(2), g_norm2[l],
                               mods_s(4), mods_s(3), Ms)
        n_st = (FFN_CONV - 1) * Bs
        r_st = max(8, n_st)
        init = state_ffn_conv[l].transpose(1, 0, 2).reshape(1, n_st, d_ff)
        init = jnp.pad(init, ((0, 0), (r_st - n_st, 0), (0, 0)))
        xs, fst = _ffn_call(h2, w_gu, w_fconv[l], b_fconv[l], w_o, l, xs, mods_s(5), init, Ms, tf, Bs)
        fst = fst[:, r_st - n_st:].reshape(FFN_CONV - 1, Bs, d_ff).transpose(1, 0, 2)
        for k_, v_ in (('C', C), ('n', n), ('m', m), ('mconv', cso), ('S', Sg), ('fconv', fst)):
            col_s[k_].append(v_)

    y_prompt = _normmod_call(xp, g_final, jnp.zeros((B, 1, D), F32), jnp.zeros((B, 1, D), F32), F32, tm_p)
    y_sample = _normmod_call(xs, g_final, jnp.zeros((1, 1, D), F32), jnp.zeros((1, 1, D), F32), F32, Ms)
    y_sample = y_sample.reshape(Ts, Bs, D).transpose(1, 0, 2)
    sp = {k: jnp.stack(v, axis=0) for k, v in col_p.items()}
    ss = {k: jnp.stack(v, axis=0) for k, v in col_s.items() if v}
    for g in range(3):
        ss[f'win{g}'] = new_caches[g].transpose(0, 1, 5, 2, 3, 4)
    return (y_prompt, y_sample, sp['win0'], ss['win0'], sp['win1'], ss['win1'], sp['win2'], ss['win2'],
            sp['C'], ss['C'], sp['n'], ss['n'], sp['m'], ss['m'], sp['mconv'], ss['mconv'],
            sp['S'], ss['S'], sp['fconv'], ss['fconv'])
```

```python
import functools
import math

import jax
import jax.numpy as jnp
from jax import lax
from jax.experimental import pallas as pl
from jax.experimental.pallas import tpu as pltpu

F32 = jnp.float32
BF16 = jnp.bfloat16
HI = lax.Precision.HIGHEST

HEAD_DIM = 64
ATTN_GROUPS = ((128, 1), (512, 4), (2048, 16))
ATTN_HPG = 4
ATTN_WIDTH = 3 * ATTN_HPG * HEAD_DIM
GROUP_W = ATTN_HPG * HEAD_DIM
ATTN_SCALE = HEAD_DIM ** -0.5
ROPE_THETA = 10000.0
BAND = 128
RES_UNROLL = 4
MIXER_BATCH = 4
MLSTM_HEADS = 10
MLSTM_WIDTH = MLSTM_HEADS * HEAD_DIM
MLSTM_CONV = 4
MLSTM_CHUNK = 128
GLA_HEADS = 10
GLA_DK = 32
GLA_DV = 64
GLA_K_WIDTH = GLA_HEADS * GLA_DK
GLA_V_WIDTH = GLA_HEADS * GLA_DV
GLA_RANK = 16
GLA_TAU = 16.0
GLA_CHUNK = 128
GLA_SAFE_DECAY = 60.0
FFN_CONV = 3
FFN_SUB = 512
EPS = 1e-6
PAST_LEN = 8192
NEG = -1e30

LANE = 128
VMEM_LIMIT = 56 * 1024 * 1024

IN_SPLITS = (ATTN_WIDTH, ATTN_WIDTH, ATTN_WIDTH, 2 * MLSTM_WIDTH, MLSTM_WIDTH, MLSTM_HEADS, MLSTM_HEADS,
             MLSTM_WIDTH, GLA_K_WIDTH, GLA_K_WIDTH, GLA_V_WIDTH, GLA_V_WIDTH, GLA_RANK)

C_AQ, C_AK, C_AV = 0, 768, 1536
C_MI, C_MF = 2304, 2432
C_MQK = 2560
C_MV, C_MO = 3840, 4480
C_GQK, C_GV, C_GG = 5120, 5760, 6400
C_GA = 7040
N_PACK = 7168
ROPE_COLS = 2 * ATTN_WIDTH


def _cparams(sem):
    return pltpu.CompilerParams(dimension_semantics=sem, vmem_limit_bytes=VMEM_LIMIT)


def _sigmoid(x):
    return 1.0 / (1.0 + jnp.exp(-x))


def _log_sigmoid(x):
    return jnp.minimum(x, 0.0) - jnp.log(1.0 + jnp.exp(-jnp.abs(x)))


def _dot(a, b):
    return jnp.dot(a, b, preferred_element_type=F32)


def _dot_nt(a, b):
    return lax.dot_general(a, b, (((1,), (1,)), ((), ())), preferred_element_type=F32)


def _split3(x):
    hi = x.astype(BF16)
    r1 = x - hi.astype(F32)
    mid = r1.astype(BF16)
    lo = (r1 - mid.astype(F32)).astype(BF16)
    return hi, mid, lo


def _dot_sel(sel, x, parts=3):
    sb = sel.astype(BF16)
    return sum(_dot(sb, p) for p in _split3(x)[:parts])


def _dot_sel_rhs(x, sel, parts=2):
    sb = sel.astype(BF16)
    return sum(_dot(p, sb) for p in _split3(x)[:parts])


def _dot_sel_nt(x, sel):
    sb = sel.astype(BF16)
    return sum(_dot_nt(p, sb) for p in _split3(x))


def _ada_kernel(c_ref, w_ref, b_ref, o_ref):
    c = c_ref[...]
    s = (c * _sigmoid(c)).astype(BF16)
    o_ref[0] = _dot(s, w_ref[0].astype(BF16)) + b_ref[0]


def _ada_call(c_all, w_ada, b_ada):
    depth, d, n = w_ada.shape
    rows = c_all.shape[0]
    tn = 1024
    return pl.pallas_call(
        _ada_kernel,
        grid=(depth, n // tn),
        in_specs=[pl.BlockSpec((rows, d), lambda l, j: (0, 0)),
                  pl.BlockSpec((1, d, tn), lambda l, j: (l, 0, j)),
                  pl.BlockSpec((1, 1, tn), lambda l, j: (l, 0, j))],
        out_specs=pl.BlockSpec((1, rows, tn), lambda l, j: (l, 0, j)),
        out_shape=jax.ShapeDtypeStruct((depth, rows, n), F32),
        compiler_params=_cparams(("arbitrary", "arbitrary")),
        name="ada",
    )(c_all, w_ada, b_ada.reshape(depth, 1, n))


def _normmod_kernel(x_ref, g_ref, sc_ref, sh_ref, o_ref):
    x = x_ref[0]
    y = x * lax.rsqrt(jnp.mean(x * x, axis=-1, keepdims=True) + EPS) * g_ref[...]
    o_ref[0] = (y * (1.0 + sc_ref[0]) + sh_ref[0]).astype(o_ref.dtype)


def _normmod_call(x, g, sc, sh, out_dtype, tm):
    G, R, D = x.shape
    rr = sc.shape[1]
    mod_spec = (pl.BlockSpec((1, 1, D), lambda b, i: (b, 0, 0)) if rr == 1
                else pl.BlockSpec((1, tm, D), lambda b, i: (b, i, 0)))
    return pl.pallas_call(
        _normmod_kernel,
        grid=(G, R // tm),
        in_specs=[pl.BlockSpec((1, tm, D), lambda b, i: (b, i, 0)),
                  pl.BlockSpec((1, D), lambda b, i: (0, 0)),
                  mod_spec, mod_spec],
        out_specs=pl.BlockSpec((1, tm, D), lambda b, i: (b, i, 0)),
        out_shape=jax.ShapeDtypeStruct((G, R, D), out_dtype),
        compiler_params=_cparams(("arbitrary", "arbitrary")),
        name="normmod",
    )(x, g.reshape(1, D), sc, sh)


def _rope_chunk(x, cos, sin_a, sin_b):
    return x * cos + pltpu.roll(x, 96, 1) * sin_a + pltpu.roll(x, 32, 1) * sin_b


def _inproj_kernel(h_ref, w_ref, b_ref, cos_ref, sa_ref, sb_ref, o_ref, *, tn):
    j = pl.program_id(0)
    o_ref[...] = _dot(h_ref[...], w_ref[...]) + b_ref[...]
    n_chunks = tn // LANE
    rope_tiles = -(-ROPE_COLS // tn)
    for jt in range(rope_tiles):
        chunks = min(n_chunks, (ROPE_COLS - jt * tn) // LANE)

        @pl.when(j == jt)
        def _():
            cos, sa, sb = cos_ref[...], sa_ref[...], sb_ref[...]
            for c in range(chunks):
                sl = slice(c * LANE, (c + 1) * LANE)
                o_ref[:, sl] = _rope_chunk(o_ref[:, sl], cos, sa, sb)


def _inproj_call(h, w, layer, b, cos, sin_a, sin_b, tm, tn):
    M, D = h.shape
    N = w.shape[2]
    tp = cos.shape[0] // tm
    tab = pl.BlockSpec((tm, LANE), lambda j, i: (i % tp, 0))
    return pl.pallas_call(
        functools.partial(_inproj_kernel, tn=tn),
        grid=(N // tn, M // tm),
        in_specs=[pl.BlockSpec((tm, D), lambda j, i: (i, 0)),
                  pl.BlockSpec((None, D, tn), lambda j, i: (layer, 0, j)),
                  pl.BlockSpec((1, tn), lambda j, i: (0, j)),
                  tab, tab, tab],
        out_specs=pl.BlockSpec((tm, tn), lambda j, i: (i, j)),
        out_shape=jax.ShapeDtypeStruct((M, N), F32),
        compiler_params=_cparams(("arbitrary", "arbitrary")),
        name="inproj",
    )(h, w, b, cos, sin_a, sin_b)


def _attn_prompt_kernel(q_ref, kp_ref, kc_ref, vp_ref, vc_ref, o_ref, l_ref, obuf, lbuf, *, dil, nq):
    i = pl.program_id(2)
    row = lax.broadcasted_iota(jnp.int32, (BAND, 2 * BAND), 0)
    col = lax.broadcasted_iota(jnp.int32, (BAND, 2 * BAND), 1)
    band = (col >= row) & (col <= row + BAND)
    band_first = band & ((col >= BAND) | (i > 0))
    nr = nq * BAND

    def residues(rs):
        rows_of, items = [], []
        for u, r in enumerate(rs):
            if dil == 1:
                rows, prev_rows = slice(None), slice(nr - BAND, nr)
            else:
                rows, prev_rows = pl.ds(r, nr, stride=dil), pl.ds((nr - BAND) * dil + r, BAND, stride=dil)
            rows_of.append(rows)
            q_all = q_ref[0, rows, :].astype(BF16)
            k_all = jnp.concatenate([kp_ref[0, prev_rows, :], kc_ref[0, rows, :]], axis=0).astype(BF16)
            v_all = jnp.concatenate([vp_ref[0, prev_rows, :], vc_ref[0, rows, :]], axis=0).astype(BF16)
            for sb in range(nq):
                qs = slice(sb * BAND, (sb + 1) * BAND)
                ks = slice(sb * BAND, (sb + 2) * BAND)
                for h in range(LANE // HEAD_DIM):
                    hs = slice(h * HEAD_DIM, (h + 1) * HEAD_DIM)
                    items.append((u, qs, hs, q_all[qs, hs], k_all[ks, hs], v_all[ks, hs],
                                  band_first if sb == 0 else band))
        ss = [jnp.where(it[6], _dot_nt(it[3], it[4]) * ATTN_SCALE, NEG) for it in items]
        mxs = [jnp.max(s, axis=-1, keepdims=True) for s in ss]
        es = [jnp.exp(s - mx) for s, mx in zip(ss, mxs)]
        dens = [jnp.sum(e, axis=-1, keepdims=True) for e in es]
        pvs = [_dot(e.astype(BF16), it[5]) for e, it in zip(es, items)]
        outs = [pv / den for pv, den in zip(pvs, dens)]
        lses = [jnp.broadcast_to(mx + jnp.log(den), (BAND, HEAD_DIM)) for mx, den in zip(mxs, dens)]
        for it, o, l in zip(items, outs, lses):
            obuf[it[0], it[1], it[2]] = o
            lbuf[it[0], it[1], it[2]] = l
        for u, rows in enumerate(rows_of):
            o_ref[0, rows, :] = obuf[u]
            l_ref[0, rows, :] = lbuf[u]

    if dil == 1:
        residues([0])
    elif dil <= RES_UNROLL:
        residues(list(range(dil)))
    else:
        def body(j, carry):
            residues([j * RES_UNROLL + u for u in range(RES_UNROLL)])
            return carry

        lax.fori_loop(0, dil // RES_UNROLL, body, 0)


def _attn_prompt_call(proj, g, dil):
    B, S, N = proj.shape
    nq = max(1, math.gcd(S, 512) // (BAND * dil))
    tb = nq * BAND * dil
    nb = S // tb
    hp = GROUP_W // LANE
    cq, ck, cv = (C_AQ + g * GROUP_W) // LANE, (C_AK + g * GROUP_W) // LANE, (C_AV + g * GROUP_W) // LANE

    def spec(cblk, prev):
        if prev:
            return pl.BlockSpec((1, tb, LANE), lambda b, p, i: (b, jnp.maximum(i - 1, 0), cblk + p))
        return pl.BlockSpec((1, tb, LANE), lambda b, p, i: (b, i, cblk + p))

    ospec = pl.BlockSpec((1, tb, LANE), lambda b, p, i: (b, i, p))
    o, lse = pl.pallas_call(
        functools.partial(_attn_prompt_kernel, dil=dil, nq=nq),
        grid=(B, hp, nb),
        in_specs=[spec(cq, False), spec(ck, True), spec(ck, False), spec(cv, True), spec(cv, False)],
        out_specs=[ospec, ospec],
        out_shape=[jax.ShapeDtypeStruct((B, S, GROUP_W), F32)] * 2,
        scratch_shapes=[pltpu.VMEM((min(dil, RES_UNROLL), nq * BAND, LANE), F32)] * 2,
        compiler_params=_cparams(("arbitrary", "arbitrary", "arbitrary")),
        name=f"attn_prompt_g{g}",
    )(proj, proj, proj, proj, proj)
    return o, lse


def _win_extract_kernel(x_ref, o_ref):
    o_ref[0, 0] = x_ref[0].T


def _win_extract_call(proj, g, keep):
    B, S, N = proj.shape
    chunk = min(keep, 512)
    first = (S - keep) // chunk
    cblk = C_AK // GROUP_W + g
    step = (C_AV - C_AK) // GROUP_W
    out = pl.pallas_call(
        _win_extract_kernel,
        grid=(B, 2, keep // chunk),
        in_specs=[pl.BlockSpec((1, chunk, GROUP_W), lambda b, kv, j: (b, first + j, cblk + step * kv))],
        out_specs=pl.BlockSpec((1, 1, GROUP_W, chunk), lambda b, kv, j: (b, kv, 0, j)),
        out_shape=jax.ShapeDtypeStruct((B, 2, GROUP_W, keep), F32),
        compiler_params=_cparams(("arbitrary", "arbitrary", "arbitrary")),
        name=f"win_extract_g{g}",
    )(proj)
    return out.reshape(B, 2, ATTN_HPG, HEAD_DIM, keep).transpose(0, 4, 1, 2, 3)


QROWS = 16


def _attn_sample_kernel(qkv_ref, c0_ref, c1_ref, c2_ref, *rest, T, n_prev):
    y_ref, n0_ref, n1_ref, n2_ref, q16, npad, ysc = rest[n_prev:]
    q16[...] = jnp.zeros_like(q16)
    npad[...] = jnp.zeros_like(npad)
    q16[0:T, :] = qkv_ref[0, :, 0:ATTN_WIDTH]
    lane = lax.broadcasted_iota(jnp.int32, (HEAD_DIM, LANE), 1)
    items = []
    for g, (cref, nref) in enumerate(((c0_ref, n0_ref), (c1_ref, n1_ref), (c2_ref, n2_ref))):
        Lb = cref.shape[-1]
        dil = ATTN_GROUPS[g][1]
        npad[g, 0:T, 0:GROUP_W] = qkv_ref[0, :, C_AK + g * GROUP_W:C_AK + (g + 1) * GROUP_W]
        npad[g, 0:T, GROUP_W:2 * GROUP_W] = qkv_ref[0, :, C_AV + g * GROUP_W:C_AV + (g + 1) * GROUP_W]
        new_t = npad[g].T
        ncols = Lb + LANE
        t = lax.broadcasted_iota(jnp.int32, (QROWS, ncols), 0)
        c = lax.broadcasted_iota(jnp.int32, (QROWS, ncols), 1)
        if dil == 1:
            valid = ((c >= t) & (c < Lb)) | ((c >= Lb) & (c - Lb <= t) & (c - Lb < T))
        else:
            valid = ((c < Lb) & ((c & (dil - 1)) == t)) | (c == Lb + t)
        for h in range(ATTN_HPG):
            rk = slice(h * HEAD_DIM, (h + 1) * HEAD_DIM)
            rv = slice(GROUP_W + h * HEAD_DIM, GROUP_W + (h + 1) * HEAD_DIM)
            kc, vc = cref[0, 0, 0, h], cref[0, 0, 1, h]
            kt = jnp.concatenate([kc, new_t[rk, :]], axis=1).astype(BF16)
            vt = jnp.concatenate([vc, new_t[rv, :]], axis=1).astype(BF16)
            qh = q16[:, g * GROUP_W + h * HEAD_DIM:g * GROUP_W + (h + 1) * HEAD_DIM].astype(BF16)
            items.append((qh, kt, vt, valid))
            for kv, blk, rn in ((0, kc, rk), (1, vc, rv)):
                sh = pltpu.roll(blk, Lb - T, 1)
                newc = pltpu.roll(new_t[rn, :], LANE - T, 1)
                if Lb > LANE:
                    nref[0, 0, kv, h, :, 0:Lb - LANE] = sh[:, 0:Lb - LANE]
                nref[0, 0, kv, h, :, Lb - LANE:Lb] = jnp.where(lane >= LANE - T, newc, sh[:, Lb - LANE:Lb])
    ss = [jnp.where(valid, _dot(qh, kt) * ATTN_SCALE, NEG) for qh, kt, vt, valid in items]
    mxs = [jnp.max(s, axis=-1, keepdims=True) for s in ss]
    es = [jnp.exp(s - mx) for s, mx in zip(ss, mxs)]
    dens = [jnp.sum(e, axis=-1, keepdims=True) for e in es]
    pvs = [_dot_nt(e.astype(BF16), it[2]) for e, it in zip(es, items)]
    outs = [pv / den for pv, den in zip(pvs, dens)]
    lses = [mx + jnp.log(den) for mx, den in zip(mxs, dens)]
    for h in range(ATTN_HPG):
        l0, l1, l2 = lses[h], lses[ATTN_HPG + h], lses[2 * ATTN_HPG + h]
        mx = jnp.maximum(jnp.maximum(l0, l1), l2)
        es = [jnp.exp(l0 - mx), jnp.exp(l1 - mx), jnp.exp(l2 - mx)]
        inv = 1.0 / (es[0] + es[1] + es[2])
        for g in range(3):
            c0 = g * GROUP_W + h * HEAD_DIM
            ysc[:, c0:c0 + HEAD_DIM] = outs[g * ATTN_HPG + h] * (es[g] * inv)
    y_ref[0] = ysc[0:T, :].astype(y_ref.dtype)


def _attn_sample_call(qkv, caches_t, layer, prev):
    B, T, _ = qkv.shape
    depth = caches_t[0].shape[0]
    for g, (win, dil) in enumerate(ATTN_GROUPS):
        assert caches_t[g].shape[-1] == win and win // dil == BAND and (g == 0 or T <= dil), \
            "sample attention assumes full window caches"
    cspecs = [pl.BlockSpec((1, 1) + c.shape[2:], lambda b: (layer, b, 0, 0, 0, 0)) for c in caches_t]
    n_prev = 0 if prev is None else 3
    prev_args = [] if prev is None else list(prev)
    prev_specs = [pl.BlockSpec(memory_space=pl.ANY)] * n_prev
    aliases = {} if prev is None else {4 + g: 1 + g for g in range(3)}
    outs = pl.pallas_call(
        functools.partial(_attn_sample_kernel, T=T, n_prev=n_prev),
        grid=(B,),
        in_specs=[pl.BlockSpec((1, T, 3 * ATTN_WIDTH), lambda b: (b, 0, 0))] + cspecs + prev_specs,
        out_specs=[pl.BlockSpec((1, T, ATTN_WIDTH), lambda b: (b, 0, 0))] + cspecs,
        out_shape=[jax.ShapeDtypeStruct((B, T, ATTN_WIDTH), BF16)]
        + [jax.ShapeDtypeStruct(c.shape, F32) for c in caches_t],
        scratch_shapes=[pltpu.VMEM((QROWS, ATTN_WIDTH), F32),
                        pltpu.VMEM((3, LANE, 2 * GROUP_W), F32),
                        pltpu.VMEM((QROWS, ATTN_WIDTH), F32)],
        input_output_aliases=aliases,
        compiler_params=_cparams(("arbitrary",)),
        name="attn_sample",
    )(qkv, *caches_t, *prev_args)
    return outs[0], outs[1:]


def _mlstm_kernel(mqk_ref, mv_ref, mi_ref, mf_ref, mo_ref, cst_ref, wc_ref, bc_ref, c0_ref, n0_ref, m0_ref,
                  g_ref, tri_ref, y_ref, c_ref, n_ref, m_ref, cso_ref,
                  xbuf, vbuf, cs, ns, ms, tpad, kwp, *, T, Lp, nb):
    c = pl.program_id(1)
    last = pl.num_programs(1) - 1
    W = MLSTM_WIDTH
    bs = range(nb)
    heads = range(MLSTM_HEADS)
    hsl = [slice(h * HEAD_DIM, (h + 1) * HEAD_DIM) for h in heads]
    pairs = [(b, h) for b in bs for h in heads]

    @pl.when(c == 0)
    def _():
        xbuf[...] = jnp.zeros_like(xbuf)
        vbuf[...] = jnp.zeros_like(vbuf)
        tpad[...] = jnp.zeros_like(tpad)
        kwp[...] = jnp.zeros_like(kwp)
        xbuf[:, 5:8, :] = cst_ref[...]
        cs[...] = c0_ref[...]
        ns[...] = n0_ref[...]
        ms[...] = m0_ref[...]

    w = wc_ref[...]
    tri = tri_ref[...]
    causal = tri > 0.5
    rowid = lax.broadcasted_iota(jnp.int32, (Lp, LANE), 0)
    real = rowid < T
    lane_row = lax.broadcasted_iota(jnp.int32, (1, LANE), 1)

    for b in bs:
        xbuf[b, 8:8 + T, :] = mqk_ref[b]
        vbuf[b, 0:T, :] = mv_ref[b]
    ys = [bc_ref[...] + xbuf[b, 8:8 + Lp, :] * w[3:4] + xbuf[b, 7:7 + Lp, :] * w[2:3]
          + xbuf[b, 6:6 + Lp, :] * w[1:2] + xbuf[b, 5:5 + Lp, :] * w[0:1] for b in bs]
    tails = [xbuf[b, 8 + T - 3:8 + T, :] for b in bs]
    for b in bs:
        xbuf[b, 5:8, :] = tails[b]

    @pl.when(c == last)
    def _():
        for b in bs:
            cso_ref[b] = tails[b]

    qks = [y * _sigmoid(y) for y in ys]
    igs, lfs = [], []
    for b in bs:
        tpad[b, 0:T, :] = mi_ref[b]
        igs.append(jnp.where(real, tpad[b, 0:Lp, :], NEG))
        tpad[b, 0:T, :] = mf_ref[b]
        lfs.append(jnp.where(real, _log_sigmoid(tpad[b, 0:Lp, :]), 0.0))
    Fs = [_dot_sel(tri, lf) for lf in lfs]
    m_rows = [ms[b] for b in bs]
    inter_all = [Fs[b] + m_rows[b] for b in bs]
    FTs, ITs = [], []
    for b in bs:
        tpad[b, 0:Lp, :] = Fs[b]
        FTs.append(tpad[b].T)
        tpad[b, 0:Lp, :] = igs[b]
        ITs.append(tpad[b].T)
    v_alls = [vbuf[b, 0:Lp, :] for b in bs]
    n_alls = [ns[b] for b in bs]

    qs = [qks[b][:, hsl[h]] for b, h in pairs]
    ks = [qks[b][:, W + h * HEAD_DIM:W + (h + 1) * HEAD_DIM] * (HEAD_DIM ** -0.5) for b, h in pairs]
    qbs = [q.astype(BF16) for q in qs]
    vbs = [v_alls[b][:, hsl[h]].astype(BF16) for b, h in pairs]
    sqk = [_dot_nt(qb, k.astype(BF16)) for qb, k in zip(qbs, ks)]
    qcs = [_dot(qbs[i], cs[b, h].astype(BF16)) for i, (b, h) in enumerate(pairs)]
    nrows = [n_alls[b][h:h + 1, :] for b, h in pairs]
    qn = [_dot_nt(qbs[i], jnp.broadcast_to(nrows[i], (16, HEAD_DIM)).astype(BF16))[:, 0:1]
          for i in range(len(pairs))]

    fcols = [Fs[b][:, h:h + 1] for b, h in pairs]
    icols = [igs[b][:, h:h + 1] for b, h in pairs]
    inters = [inter_all[b][:, h:h + 1] for b, h in pairs]
    logws = [jnp.where(causal, fcols[i] - FTs[b][h:h + 1, 0:Lp] + ITs[b][h:h + 1, 0:Lp], NEG)
             for i, (b, h) in enumerate(pairs)]
    rmax = [jnp.max(lw, axis=-1, keepdims=True) for lw in logws]
    mts = [jnp.maximum(r, it) for r, it in zip(rmax, inters)]
    As = [jnp.exp(lw - mt) * s for lw, mt, s in zip(logws, mts, sqk)]
    gqs = [jnp.exp(it - mt) for it, mt in zip(inters, mts)]
    asum = [jnp.sum(A, axis=-1, keepdims=True) for A in As]
    dens = [a + g * q for a, g, q in zip(asum, gqs, qn)]
    a_bf = [A.astype(BF16) for A in As]
    avs = [_dot(a, v) for a, v in zip(a_bf, vbs)]
    mLs = [mt[Lp - 1:Lp, :] for mt in mts]
    flast = [Fs[b][Lp - 1:Lp, h:h + 1] for b, h in pairs]
    wLs = [jnp.exp(fl - fc + ic - mL) for fl, fc, ic, mL in zip(flast, fcols, icols, mLs)]
    gls = [jnp.exp(flast[i] + m_rows[b][0:1, h:h + 1] - mLs[i]) for i, (b, h) in enumerate(pairs)]
    kws = [k * wl for k, wl in zip(ks, wLs)]
    for i, (b, h) in enumerate(pairs):
        kwp[b, 0:Lp, hsl[h]] = kws[i]
    ksum = [jnp.sum(kw, axis=0, keepdims=True) for kw in kws]
    for i, (b, h) in enumerate(pairs):
        ns[b, h:h + 1, :] = gls[i] * nrows[i] + ksum[i]
    for b in bs:
        m_new = m_rows[b]
        for h in heads:
            m_new = jnp.where(lane_row == h, mLs[b * MLSTM_HEADS + h], m_new)
        ms[b] = m_new

    kwts = [kwp[b].T for b in bs]
    upds = [_dot(kwts[b][hsl[h], 0:Lp].astype(BF16), vbs[i]) for i, (b, h) in enumerate(pairs)]

    nums = [av + g * qc for av, g, qc in zip(avs, gqs, qcs)]
    hhs = [nu / jnp.maximum(jnp.abs(de), jnp.exp(-mt)) for nu, de, mt in zip(nums, dens, mts)]
    msq = [jnp.mean(hh * hh, axis=-1, keepdims=True) for hh in hhs]
    hns = [hhs[i] * lax.rsqrt(msq[i] + EPS) * g_ref[0:1, hsl[h]] for i, (b, h) in enumerate(pairs)]
    ogs = [_sigmoid(mo_ref[b, :, hsl[h]]) for b, h in pairs]
    for i, (b, h) in enumerate(pairs):
        yo = hns[i] * ogs[i] if Lp == T else hns[i][0:T] * ogs[i]
        y_ref[b, :, hsl[h]] = yo.astype(y_ref.dtype)
    for i, (b, h) in enumerate(pairs):
        cs[b, h] = gls[i] * cs[b, h] + upds[i]

    @pl.when(c == last)
    def _():
        c_ref[...] = cs[...]
        n_ref[...] = ns[...]
        m_ref[...] = ms[...]


def _mlstm_call(proj, conv_state, w_conv, b_conv, c0, n0, m0, g_mlstm, chunk):
    B, S, N = proj.shape
    T = chunk
    nc = S // T
    Lp = max(16, T)
    H = MLSTM_HEADS
    nb = math.gcd(B, MIXER_BATCH)
    m0p = jnp.pad(m0, ((0, 0), (0, LANE - H))).reshape(B, 1, LANE)
    tri = jnp.tril(jnp.ones((Lp, Lp), F32))

    def col(width, off):
        blk = off // width
        return pl.BlockSpec((nb, T, width), lambda b, c: (b, c, blk))

    def const(shape):
        nd = len(shape)
        return pl.BlockSpec(shape, lambda b, c: (0,) * nd)

    def per_b(shape):
        nd = len(shape)
        return pl.BlockSpec((nb,) + shape, lambda b, c: (b,) + (0,) * nd)

    outs = pl.pallas_call(
        functools.partial(_mlstm_kernel, T=T, Lp=Lp, nb=nb),
        grid=(B // nb, nc),
        in_specs=[col(2 * MLSTM_WIDTH, C_MQK), col(MLSTM_WIDTH, C_MV), col(LANE, C_MI), col(LANE, C_MF),
                  col(MLSTM_WIDTH, C_MO), per_b((MLSTM_CONV - 1, 2 * MLSTM_WIDTH)),
                  const((MLSTM_CONV, 2 * MLSTM_WIDTH)), const((1, 2 * MLSTM_WIDTH)),
                  per_b((H, HEAD_DIM, HEAD_DIM)), per_b((H, HEAD_DIM)), per_b((1, LANE)),
                  const((1, MLSTM_WIDTH)), const((Lp, Lp))],
        out_specs=[pl.BlockSpec((nb, T, MLSTM_WIDTH), lambda b, c: (b, c, 0)),
                   per_b((H, HEAD_DIM, HEAD_DIM)), per_b((H, HEAD_DIM)), per_b((1, LANE)),
                   per_b((MLSTM_CONV - 1, 2 * MLSTM_WIDTH))],
        out_shape=[jax.ShapeDtypeStruct((B, S, MLSTM_WIDTH), BF16),
                   jax.ShapeDtypeStruct((B, H, HEAD_DIM, HEAD_DIM), F32),
                   jax.ShapeDtypeStruct((B, H, HEAD_DIM), F32),
                   jax.ShapeDtypeStruct((B, 1, LANE), F32),
                   jax.ShapeDtypeStruct((B, MLSTM_CONV - 1, 2 * MLSTM_WIDTH), F32)],
        scratch_shapes=[pltpu.VMEM((nb, 8 + Lp, 2 * MLSTM_WIDTH), F32),
                        pltpu.VMEM((nb, Lp, MLSTM_WIDTH), F32),
                        pltpu.VMEM((nb, H, HEAD_DIM, HEAD_DIM), F32),
                        pltpu.VMEM((nb, H, HEAD_DIM), F32),
                        pltpu.VMEM((nb, 1, LANE), F32),
                        pltpu.VMEM((nb, LANE, LANE), F32),
                        pltpu.VMEM((nb, LANE, MLSTM_WIDTH), F32)],
        compiler_params=_cparams(("arbitrary", "arbitrary")),
        name="mlstm",
    )(proj, proj, proj, proj, proj, conv_state, w_conv, b_conv.reshape(1, -1), c0, n0, m0p,
      g_mlstm.reshape(1, -1), tri)
    y, C, n, m, cso = outs
    return y, C, n, m[:, 0, :H], cso


def _gla_kernel(gqk_ref, gv_ref, gg_ref, ga_ref, wa_ref, wat_ref, s0_ref, g_ref, tri_ref, bd_ref, ee_ref,
                y_ref, s_ref, qkp, vp, gap, sbd, osc, *, T, Lp, nb):
    c = pl.program_id(1)
    last = pl.num_programs(1) - 1
    KW, VW = GLA_K_WIDTH, GLA_V_WIDTH
    bs = range(nb)
    heads = range(GLA_HEADS)

    @pl.when(c == 0)
    def _():
        qkp[...] = jnp.zeros_like(qkp)
        vp[...] = jnp.zeros_like(vp)
        gap[...] = jnp.zeros_like(gap)
        sbd[...] = jnp.zeros_like(sbd)
        osc[...] = jnp.zeros_like(osc)
        for b in bs:
            for h in heads:
                sbd[b, h * GLA_DK:(h + 1) * GLA_DK, h * GLA_DV:(h + 1) * GLA_DV] = s0_ref[b, h]

    lane = lax.broadcasted_iota(jnp.int32, (T, LANE), 1)
    for b in bs:
        qkp[b, 0:T, :] = gqk_ref[b]
        vp[b, 0:T, :] = gv_ref[b]
        gap[b, 0:T, :] = jnp.where(lane == GLA_RANK, 1.0, ga_ref[b])

    tri = tri_ref[...]
    rowid = lax.broadcasted_iota(jnp.int32, (Lp, KW), 0)
    colid = lax.broadcasted_iota(jnp.int32, (KW, LANE), 1)
    wa_hi, wa_lo, _ = _split3(wa_ref[...])
    wat_hi, wat_lo, _ = _split3(wat_ref[...])
    ga_bs = [gap[b, 0:Lp, :].astype(BF16) for b in bs]
    gat_bs = [gap[b].T.astype(BF16) for b in bs]
    las = [_dot(g, wa_hi) + _dot(g, wa_lo) for g in ga_bs]
    las = [jnp.where(rowid < T, _log_sigmoid(la) / GLA_TAU, 0.0) for la in las]
    bcs = [_dot_sel(tri[0:Lp, 0:Lp], la) for la in las]
    lats = [_dot(wat_hi, g) + _dot(wat_lo, g) for g in gat_bs]
    lats = [jnp.where(colid < T, _log_sigmoid(lat) / GLA_TAU, 0.0) for lat in lats]
    bcts = [_dot_sel_nt(lat, tri) for lat in lats]
    blcols = [bct[:, LANE - 1:LANE] for bct in bcts]
    blrows = [bc[Lp - 1:Lp, :] for bc in bcs]
    kts = [qkp[b].T[KW:2 * KW, :] for b in bs]
    bd = bd_ref[...]
    min_decay = jnp.min(blrows[0])
    for b in bs[1:]:
        min_decay = jnp.minimum(min_decay, jnp.min(blrows[b]))
    safe = min_decay >= -GLA_SAFE_DECAY

    @pl.when(safe)
    def _():
        causal = tri[0:Lp, 0:Lp] > 0.5
        qbs = [(qkp[b, 0:Lp, 0:KW] * (GLA_DK ** -0.5) * jnp.exp(bcs[b])).astype(BF16) for b in bs]
        kbs = [(qkp[b, 0:Lp, KW:2 * KW] * jnp.exp(-bcs[b])).astype(BF16) for b in bs]
        vbs = [vp[b, 0:Lp, :].astype(BF16) for b in bs]
        o_inters = [_dot(qbs[b], sbd[b].astype(BF16)) for b in bs]
        pairs = [(b, h) for b in bs for h in heads]
        scores = [_dot_nt(qbs[b][:, h * GLA_DK:(h + 1) * GLA_DK], kbs[b][:, h * GLA_DK:(h + 1) * GLA_DK])
                  for b, h in pairs]
        a_bf = [jnp.where(causal, s, 0.0).astype(BF16) for s in scores]
        ovs = [_dot(a_bf[i], vbs[b][:, h * GLA_DV:(h + 1) * GLA_DV]) for i, (b, h) in enumerate(pairs)]
        klts = [(kts[b] * jnp.exp(blcols[b] - bcts[b])).astype(BF16) for b in bs]
        upds = [_dot(klts[b], vp[b].astype(BF16)) for b in bs]
        for b in bs:
            osc[b, 0:Lp, :] = o_inters[b] + jnp.concatenate(ovs[b * GLA_HEADS:(b + 1) * GLA_HEADS], axis=1)
            sbd[b] = bd * (jnp.exp(blcols[b]) * sbd[b] + upds[b])

    @pl.when(jnp.logical_not(safe))
    def _():
        srow = lax.broadcasted_iota(jnp.int32, (LANE, LANE), 0)
        for b in bs:
            def body(t, carry, b=b):
                sel = (srow == t).astype(F32)
                lac = jnp.dot(lats[b], sel, precision=HI, preferred_element_type=F32)
                kc = jnp.dot(kts[b], sel, precision=HI, preferred_element_type=F32)
                dec = jnp.concatenate([jnp.exp(lac)] * (VW // LANE), axis=1)
                kcw = jnp.concatenate([kc] * (VW // LANE), axis=1)
                vrow = vp[b, pl.ds(t, 1), :]
                snew = bd * (dec * sbd[b] + kcw * vrow)
                sbd[b] = snew
                qrow = jnp.broadcast_to(qkp[b, pl.ds(t, 1), 0:KW] * (GLA_DK ** -0.5), (8, KW))
                orow = jnp.dot(qrow, snew, precision=HI, preferred_element_type=F32)
                osc[b, pl.ds(t, 1), :] = orow[0:1]
                return carry

            lax.fori_loop(0, T, body, 0)

    os_ = [osc[b, 0:Lp, :] for b in bs]
    mss = [_dot_sel_rhs(o * o, ee_ref[...]) for o in os_]
    ogs = [o * lax.rsqrt(ms + EPS) * g_ref[...] for o, ms in zip(os_, mss)]
    for b in bs:
        gg = gg_ref[b]
        yo = (ogs[b] if Lp == T else ogs[b][0:T]) * (gg * _sigmoid(gg))
        y_ref[b] = yo.astype(y_ref.dtype)

    @pl.when(c == last)
    def _():
        for b in bs:
            for h in heads:
                s_ref[b, h] = sbd[b, h * GLA_DK:(h + 1) * GLA_DK, h * GLA_DV:(h + 1) * GLA_DV]


def _gla_call(proj, w_a2, b_a2, s0, g_gla, chunk):
    B, S, N = proj.shape
    T = chunk
    nc = S // T
    Lp = max(16, T)
    H, KW, VW = GLA_HEADS, GLA_K_WIDTH, GLA_V_WIDTH
    wa = jnp.zeros((LANE, KW), F32).at[:GLA_RANK].set(w_a2).at[GLA_RANK].set(b_a2)
    tri = jnp.tril(jnp.ones((LANE, LANE), F32))
    hk = jnp.arange(KW) // GLA_DK
    hv = jnp.arange(VW) // GLA_DV
    bd = (hk[:, None] == hv[None, :]).astype(F32)
    ee = (hv[:, None] == hv[None, :]).astype(F32) / GLA_DV

    nb = math.gcd(B, MIXER_BATCH)

    def col(width, off):
        blk = off // width
        return pl.BlockSpec((nb, T, width), lambda b, c: (b, c, blk))

    def const(shape):
        nd = len(shape)
        return pl.BlockSpec(shape, lambda b, c: (0,) * nd)

    y, s = pl.pallas_call(
        functools.partial(_gla_kernel, T=T, Lp=Lp, nb=nb),
        grid=(B // nb, nc),
        in_specs=[col(2 * KW, C_GQK), col(VW, C_GV), col(VW, C_GG), col(LANE, C_GA),
                  const((LANE, KW)), const((KW, LANE)),
                  pl.BlockSpec((nb, H, GLA_DK, GLA_DV), lambda b, c: (b, 0, 0, 0)),
                  const((1, VW)), const((LANE, LANE)), const((KW, VW)), const((VW, VW))],
        out_specs=[pl.BlockSpec((nb, T, VW), lambda b, c: (b, c, 0)),
                   pl.BlockSpec((nb, H, GLA_DK, GLA_DV), lambda b, c: (b, 0, 0, 0))],
        out_shape=[jax.ShapeDtypeStruct((B, S, VW), BF16),
                   jax.ShapeDtypeStruct((B, H, GLA_DK, GLA_DV), F32)],
        scratch_shapes=[pltpu.VMEM((nb, LANE, 2 * KW), F32),
                        pltpu.VMEM((nb, LANE, VW), F32),
                        pltpu.VMEM((nb, LANE, LANE), F32),
                        pltpu.VMEM((nb, KW, VW), F32),
                        pltpu.VMEM((nb, LANE, VW), F32)],
        compiler_params=_cparams(("arbitrary", "arbitrary")),
        name="gla",
    )(proj, proj, proj, proj, wa, wa.T, s0, g_gla.reshape(1, -1), tri, bd, ee)
    return y, s


def _outproj_kernel(*refs, n_attn):
    attn_refs = refs[:n_attn]
    ym_ref, yg_ref, w_ref, x_ref, gt_ref, g2_ref, sc_ref, sh_ref, xo_ref, h2_ref = refs[n_attn:]
    a, b = ATTN_WIDTH, ATTN_WIDTH + MLSTM_WIDTH
    if n_attn == 1:
        acc = _dot(attn_refs[0][0], w_ref[0:a, :])
    else:
        l0, l1, l2 = attn_refs[3][0], attn_refs[4][0], attn_refs[5][0]
        mx = jnp.maximum(jnp.maximum(l0, l1), l2)
        es = [jnp.exp(l0 - mx), jnp.exp(l1 - mx), jnp.exp(l2 - mx)]
        inv = 1.0 / (es[0] + es[1] + es[2])
        acc = None
        for g in range(3):
            yg_attn = (attn_refs[g][0] * (es[g] * inv)).astype(BF16)
            part = _dot(yg_attn, w_ref[g * GROUP_W:(g + 1) * GROUP_W, :])
            acc = part if acc is None else part + acc
    acc = acc + _dot(ym_ref[0], w_ref[a:b, :])
    acc = acc + _dot(yg_ref[0], w_ref[b:, :])
    x = x_ref[0] + gt_ref[0] * acc
    xo_ref[0] = x
    y = x * lax.rsqrt(jnp.mean(x * x, axis=-1, keepdims=True) + EPS) * g2_ref[...]
    h2_ref[0] = (y * (1.0 + sc_ref[0]) + sh_ref[0]).astype(h2_ref.dtype)


def _outproj_call(ya, ym, yg, w_out, layer, x, gt, g2, sc, sh, tm):
    G, R, D = x.shape
    rr = gt.shape[1]
    mod_spec = (pl.BlockSpec((1, 1, D), lambda b, i: (b, 0, 0)) if rr == 1
                else pl.BlockSpec((1, tm, D), lambda b, i: (b, i, 0)))

    def act(width):
        return pl.BlockSpec((1, tm, width), lambda b, i: (b, i, 0))

    attn = list(ya) if isinstance(ya, (list, tuple)) else [ya]
    return pl.pallas_call(
        functools.partial(_outproj_kernel, n_attn=len(attn)),
        grid=(G, R // tm),
        in_specs=[act(a_.shape[-1]) for a_ in attn] + [act(MLSTM_WIDTH), act(GLA_V_WIDTH),
                  pl.BlockSpec((None,) + w_out.shape[1:], lambda b, i: (layer, 0, 0)),
                  act(D), mod_spec, pl.BlockSpec((1, D), lambda b, i: (0, 0)), mod_spec, mod_spec],
        out_specs=[act(D), act(D)],
        out_shape=[jax.ShapeDtypeStruct((G, R, D), F32), jax.ShapeDtypeStruct((G, R, D), BF16)],
        compiler_params=_cparams(("arbitrary", "arbitrary")),
        name="outproj",
    )(*attn, ym, yg, w_out, x, gt, g2.reshape(1, D), sc, sh)


def _ffn_kernel(h_ref, wg_ref, wu_ref, wc_ref, bc_ref, wo_ref, x_ref, gt_ref, init_ref,
                xo_ref, st_ref, ubuf, cbuf, *, tm, u, R, tiles_per_seq):
    m = pl.program_id(1)
    f = pl.program_id(2)
    nf = pl.num_programs(2)
    h = h_ref[0]

    @pl.when(m % tiles_per_seq == 0)
    def _():
        ubuf[0:R, :] = init_ref[0]

    @pl.when(m % tiles_per_seq != 0)
    def _():
        ubuf[0:R, :] = cbuf[f]

    @pl.when(f == 0)
    def _():
        xo_ref[0] = jnp.zeros_like(xo_ref[0])

    tf = wg_ref.shape[1]
    chunks = [slice(c0, min(c0 + FFN_SUB, tf)) for c0 in range(0, tf, FFN_SUB)]

    def up_matmuls(cs_):
        return _dot(h, wg_ref[:, cs_]), _dot(h, wu_ref[:, cs_])

    pending = up_matmuls(chunks[0])
    acc = xo_ref[0]
    for ci, cs_ in enumerate(chunks):
        ug, uu = pending
        if ci + 1 < len(chunks):
            pending = up_matmuls(chunks[ci + 1])
        ubuf[R:R + tm, cs_] = ug
        w = wc_ref[:, cs_]
        gate = (bc_ref[:, cs_] + ug * w[2:3] + ubuf[R - u:R - u + tm, cs_] * w[1:2]
                + ubuf[R - 2 * u:R - 2 * u + tm, cs_] * w[0:1])
        act = (gate * _sigmoid(gate) * uu).astype(BF16)
        acc = _dot(act, wo_ref[cs_, :]) + acc
    xo_ref[0] = acc
    tail = ubuf[tm:tm + R, :]
    cbuf[f] = tail
    st_ref[0, 0] = tail

    @pl.when(f == nf - 1)
    def _():
        xo_ref[0] = x_ref[0] + gt_ref[0] * xo_ref[0]


def _ffn_call(h2, w_gu, w_conv, b_conv, w_o, layer, x, gt, init, tm, tf, u):
    G, rows, D = x.shape
    F = w_gu.shape[2] // 2
    R = max(8, 2 * u)
    nm, nf = rows // tm, F // tf
    rr = gt.shape[1]
    mod_spec = (pl.BlockSpec((1, 1, D), lambda b, i, f: (b, 0, 0)) if rr == 1
                else pl.BlockSpec((1, tm, D), lambda b, i, f: (b, i, 0)))
    xo, st = pl.pallas_call(
        functools.partial(_ffn_kernel, tm=tm, u=u, R=R, tiles_per_seq=nm),
        grid=(G, nm, nf),
        in_specs=[pl.BlockSpec((1, tm, D), lambda b, i, f: (b, i, 0)),
                  pl.BlockSpec((None, D, tf), lambda b, i, f: (layer, 0, f)),
                  pl.BlockSpec((None, D, tf), lambda b, i, f: (layer, 0, nf + f)),
                  pl.BlockSpec((FFN_CONV, tf), lambda b, i, f: (0, f)),
                  pl.BlockSpec((1, tf), lambda b, i, f: (0, f)),
                  pl.BlockSpec((None, tf, D), lambda b, i, f: (layer, f, 0)),
                  pl.BlockSpec((1, tm, D), lambda b, i, f: (b, i, 0), pipeline_mode=pl.Buffered(1)),
                  mod_spec,
                  pl.BlockSpec((1, R, tf), lambda b, i, f: (b, 0, f))],
        out_specs=[pl.BlockSpec((1, tm, D), lambda b, i, f: (b, i, 0)),
                   pl.BlockSpec((1, 1, R, tf), lambda b, i, f: (b, i, 0, f))],
        out_shape=[jax.ShapeDtypeStruct((G, rows, D), F32),
                   jax.ShapeDtypeStruct((G, nm, R, F), F32)],
        scratch_shapes=[pltpu.VMEM((R + tm, tf), F32), pltpu.VMEM((nf, R, tf), F32)],
        compiler_params=_cparams(("arbitrary", "arbitrary", "arbitrary")),
        name="ffn",
    )(h2, w_gu, w_gu, w_conv, b_conv.reshape(1, F), w_o, x, gt, init)
    return xo, st[:, nm - 1]


def _pack_w_kernel(w_ref, o_ref):
    x = w_ref[0]
    rows = x.shape[0]
    n_in = x.shape[1]
    attn = 3 * ATTN_WIDTH
    gates = attn + 3 * MLSTM_WIDTH
    tail = gates + 2 * MLSTM_HEADS
    o_ref[0, :, 0:attn] = x[:, 0:attn].astype(BF16)
    lane = lax.broadcasted_iota(jnp.int32, (rows, LANE), 1)
    gwin = x[:, gates:gates + LANE]
    o_ref[0, :, C_MI:C_MI + LANE] = jnp.where(lane < MLSTM_HEADS, gwin, 0.0).astype(BF16)
    o_ref[0, :, C_MF:C_MF + LANE] = jnp.where(lane < MLSTM_HEADS, pltpu.roll(gwin, LANE - MLSTM_HEADS, 1),
                                              0.0).astype(BF16)
    o_ref[0, :, C_MQK:C_MQK + 3 * MLSTM_WIDTH] = x[:, attn:gates].astype(BF16)
    o_ref[0, :, C_GA:N_PACK] = jnp.zeros((rows, N_PACK - C_GA), BF16)
    o_ref[0, :, C_MO:C_MO + n_in - tail] = x[:, tail:n_in].astype(BF16)


def _pack_w_call(w_in):
    depth, d, n_in = w_in.shape
    tk = 256
    return pl.pallas_call(
        _pack_w_kernel,
        grid=(depth, d // tk),
        in_specs=[pl.BlockSpec((1, tk, n_in), lambda l, i: (l, i, 0))],
        out_specs=pl.BlockSpec((1, tk, N_PACK), lambda l, i: (l, i, 0)),
        out_shape=jax.ShapeDtypeStruct((depth, d, N_PACK), BF16),
        compiler_params=_cparams(("arbitrary", "arbitrary")),
        name="pack_w_in",
    )(w_in)


def _pack_in_proj(w_in, b_in):
    def split(a):
        out, off = [], 0
        for s in IN_SPLITS:
            out.append(a[..., off:off + s])
            off += s
        return out

    def pad(a, n):
        return jnp.pad(a, [(0, 0)] * (a.ndim - 1) + [(0, n - a.shape[-1])])

    def pack(a):
        aq, ak, av, mqk, mv, mi, mf, mo, gq, gk, gv, gg, ga = split(a)
        return jnp.concatenate([aq, ak, av, pad(mi, LANE), pad(mf, LANE), mqk, mv, mo, gq, gk, gv, gg,
                                pad(ga, LANE)], axis=-1)

    return _pack_w_call(w_in), pack(b_in)


def _rope_tables(pos):
    half = HEAD_DIM // 2
    inv_freq = jnp.power(ROPE_THETA, -jnp.arange(half, dtype=F32) / half)
    ang = pos.astype(F32)[:, None] * inv_freq[None, :]
    cos, sin = jnp.cos(ang), jnp.sin(ang)
    zero = jnp.zeros_like(sin)
    reps = LANE // HEAD_DIM
    return (jnp.tile(jnp.concatenate([cos, cos], -1), (1, reps)),
            jnp.tile(jnp.concatenate([-sin, zero], -1), (1, reps)),
            jnp.tile(jnp.concatenate([zero, sin], -1), (1, reps)))


def _pick_tile(n, pref):
    t = math.gcd(n, pref)
    return t


def kernel(x_prompt, x_sample, c_prompt, c_sample, cache_win0_kv, cache_win1_kv, cache_win2_kv, state_mlstm_C, state_mlstm_n, state_mlstm_m, state_mlstm_conv, state_gla_S, state_ffn_conv, w_ada, b_ada, g_norm1, g_norm2, w_in, b_in, w_mconv, b_mconv, g_mlstm, w_gla_a2, b_gla_a2, g_gla, w_out, w_ff_in, w_fconv, b_fconv, w_ff_out, g_final):
    B, S, D = x_prompt.shape
    Bs, Ts, _ = x_sample.shape
    depth = w_ada.shape[0]
    d_ff = w_fconv.shape[-1]
    caches = (cache_win0_kv, cache_win1_kv, cache_win2_kv)
    Ms = Bs * Ts

    n_c = B + Bs
    rows_c = -(-n_c // 8) * 8
    c_all = jnp.pad(jnp.concatenate([c_prompt, c_sample], axis=0), ((0, rows_c - n_c), (0, 0)))
    mod = _ada_call(c_all, w_ada, b_ada).reshape(depth, rows_c, 6, D)

    w_in_p, b_in_p = _pack_in_proj(w_in, b_in)
    w_out_b = w_out.astype(BF16)
    w_gu = w_ff_in.astype(BF16)
    w_o = w_ff_out.astype(BF16)

    rope_p = _rope_tables(jnp.arange(S))
    rope_s = _rope_tables(PAST_LEN + jnp.repeat(jnp.arange(Ts), Bs))

    tm_p = _pick_tile(S, 512)
    tm_in = _pick_tile(S, 1024)
    tm_ff = _pick_tile(S, 1024)
    tn = N_PACK // 4
    tf = _pick_tile(d_ff, 512)

    xp = x_prompt
    xs = x_sample.transpose(1, 0, 2).reshape(1, Ms, D)
    zeros_p = {
        'mconv': jnp.zeros((B, MLSTM_CONV - 1, 2 * MLSTM_WIDTH), F32),
        'C': jnp.zeros((B, MLSTM_HEADS, HEAD_DIM, HEAD_DIM), F32),
        'n': jnp.zeros((B, MLSTM_HEADS, HEAD_DIM), F32),
        'm': jnp.zeros((B, MLSTM_HEADS), F32),
        'S': jnp.zeros((B, GLA_HEADS, GLA_DK, GLA_DV), F32),
        'fconv': jnp.zeros((B, 8, d_ff), F32),
    }
    names = ('win0', 'win1', 'win2', 'C', 'n', 'm', 'mconv', 'S', 'fconv')
    col_p = {k: [] for k in names}
    col_s = {k: [] for k in names}
    caches_t = [c.transpose(0, 1, 3, 4, 5, 2) for c in caches]
    new_caches = None

    for l in range(depth):
        mp = mod[l, :B]
        ms_ = jnp.tile(mod[l, B:B + Bs], (Ts, 1, 1))

        def mods_p(i):
            return mp[:, i:i + 1, :]

        def mods_s(i):
            return ms_[None, :, i, :]

        h = _normmod_call(xp, g_norm1[l], mods_p(1), mods_p(0), BF16, tm_p)
        proj = _inproj_call(h.reshape(B * S, D), w_in_p, l, b_in_p[l][None], *rope_p, tm_in, tn)
        proj = proj.reshape(B, S, N_PACK)
        os_, ls_ = [], []
        for g, (win, dil) in enumerate(ATTN_GROUPS):
            o, lse = _attn_prompt_call(proj, g, dil)
            os_.append(o)
            ls_.append(lse)
            col_p[f'win{g}'].append(_win_extract_call(proj, g, min(win, S)))
        ya = os_ + ls_
        ym, C, n, m, cso = _mlstm_call(proj, zeros_p['mconv'], w_mconv[l], b_mconv[l], zeros_p['C'],
                                        zeros_p['n'], zeros_p['m'], g_mlstm[l], math.gcd(S, MLSTM_CHUNK))
        yg, Sg = _gla_call(proj, w_gla_a2[l], b_gla_a2[l], zeros_p['S'], g_gla[l], math.gcd(S, GLA_CHUNK))
        xp, h2 = _outproj_call(ya, ym, yg, w_out_b, l, xp, mods_p(2), g_norm2[l], mods_p(4), mods_p(3), tm_p)
        xp, fst = _ffn_call(h2, w_gu, w_fconv[l], b_fconv[l], w_o, l, xp, mods_p(5),
                            zeros_p['fconv'], tm_ff, tf, 1)
        for k_, v_ in (('C', C), ('n', n), ('m', m), ('mconv', cso), ('S', Sg), ('fconv', fst[:, 6:8])):
            col_p[k_].append(v_)

        h = _normmod_call(xs, g_norm1[l], mods_s(1), mods_s(0), BF16, Ms)
        proj = _inproj_call(h.reshape(Ms, D), w_in_p, l, b_in_p[l][None], *rope_s, Ms, tn)
        proj_b = proj.reshape(Ts, Bs, N_PACK).transpose(1, 0, 2)
        ya, new_caches = _attn_sample_call(proj_b[:, :, :3 * ATTN_WIDTH], caches_t, l, new_caches)
        ym, C, n, m, cso = _mlstm_call(proj_b, state_mlstm_conv[l], w_mconv[l], b_mconv[l], state_mlstm_C[l],
                                        state_mlstm_n[l], state_mlstm_m[l], g_mlstm[l], Ts)
        yg, Sg = _gla_call(proj_b, w_gla_a2[l], b_gla_a2[l], state_gla_S[l], g_gla[l], Ts)

        def tmaj(a):
            return a.transpose(1, 0, 2).reshape(1, Ms, a.shape[-1])

        xs, h2 = _outproj_call(tmaj(ya), tmaj(ym), tmaj(yg), w_out_b, l, xs, mods_s(2), g_norm2[l],
                               mods_s(4), mods_s(3), Ms)
        n_st = (FFN_CONV - 1) * Bs
        r_st = max(8, n_st)
        init = state_ffn_conv[l].transpose(1, 0, 2).reshape(1, n_st, d_ff)
        init = jnp.pad(init, ((0, 0), (r_st - n_st, 0), (0, 0)))
        xs, fst = _ffn_call(h2, w_gu, w_fconv[l], b_fconv[l], w_o, l, xs, mods_s(5), init, Ms, tf, Bs)
        fst = fst[:, r_st - n_st:].reshape(FFN_CONV - 1, Bs, d_ff).transpose(1, 0, 2)
        for k_, v_ in (('C', C), ('n', n), ('m', m), ('mconv', cso), ('S', Sg), ('fconv', fst)):
            col_s[k_].append(v_)

    y_prompt = _normmod_call(xp, g_final, jnp.zeros((B, 1, D), F32), jnp.zeros((B, 1, D), F32), F32, tm_p)
    y_sample = _normmod_call(xs, g_final, jnp.zeros((1, 1, D), F32), jnp.zeros((1, 1, D), F32), F32, Ms)
    y_sample = y_sample.reshape(Ts, Bs, D).transpose(1, 0, 2)
    sp = {k: jnp.stack(v, axis=0) for k, v in col_p.items()}
    ss = {k: jnp.stack(v, axis=0) for k, v in col_s.items() if v}
    for g in range(3):
        ss[f'win{g}'] = new_caches[g].transpose(0, 1, 5, 2, 3, 4)
    return (y_prompt, y_sample, sp['win0'], ss['win0'], sp['win1'], ss['win1'], sp['win2'], ss['win2'],
            sp['C'], ss['C'], sp['n'], ss['n'], sp['m'], ss['m'], sp['mconv'], ss['mconv'],
            sp['S'], ss['S'], sp['fconv'], ss['fconv'])
```

```python
import functools
import math

import jax
import jax.numpy as jnp
from jax import lax
from jax.experimental import pallas as pl
from jax.experimental.pallas import tpu as pltpu

F32 = jnp.float32
BF16 = jnp.bfloat16
HI = lax.Precision.HIGHEST

HEAD_DIM = 64
ATTN_GROUPS = ((128, 1), (512, 4), (2048, 16))
ATTN_HPG = 4
ATTN_WIDTH = 3 * ATTN_HPG * HEAD_DIM
GROUP_W = ATTN_HPG * HEAD_DIM
ATTN_SCALE = HEAD_DIM ** -0.5
ROPE_THETA = 10000.0
BAND = 128
RES_UNROLL = 4
MIXER_BATCH = 4
MLSTM_HEADS = 10
MLSTM_WIDTH = MLSTM_HEADS * HEAD_DIM
MLSTM_CONV = 4
MLSTM_CHUNK = 128
GLA_HEADS = 10
GLA_DK = 32
GLA_DV = 64
GLA_K_WIDTH = GLA_HEADS * GLA_DK
GLA_V_WIDTH = GLA_HEADS * GLA_DV
GLA_RANK = 16
GLA_TAU = 16.0
GLA_CHUNK = 128
GLA_SAFE_DECAY = 60.0
FFN_CONV = 3
FFN_SUB = 512
EPS = 1e-6
PAST_LEN = 8192
NEG = -1e30

LANE = 128
VMEM_LIMIT = 62 * 1024 * 1024

IN_SPLITS = (ATTN_WIDTH, ATTN_WIDTH, ATTN_WIDTH, 2 * MLSTM_WIDTH, MLSTM_WIDTH, MLSTM_HEADS, MLSTM_HEADS,
             MLSTM_WIDTH, GLA_K_WIDTH, GLA_K_WIDTH, GLA_V_WIDTH, GLA_V_WIDTH, GLA_RANK)

C_AQ, C_AK, C_AV = 0, 768, 1536
C_MI, C_MF = 2304, 2432
C_MQK = 2560
C_MV, C_MO = 3840, 4480
C_GQK, C_GV, C_GG = 5120, 5760, 6400
C_GA = 7040
N_PACK = 7168
ROPE_COLS = 2 * ATTN_WIDTH


def _cparams(sem):
    return pltpu.CompilerParams(dimension_semantics=sem, vmem_limit_bytes=VMEM_LIMIT)


def _sigmoid(x):
    return 1.0 / (1.0 + jnp.exp(-x))


def _log_sigmoid(x):
    return jnp.minimum(x, 0.0) - jnp.log(1.0 + jnp.exp(-jnp.abs(x)))


def _dot(a, b):
    return jnp.dot(a, b, preferred_element_type=F32)


def _dot_nt(a, b):
    return lax.dot_general(a, b, (((1,), (1,)), ((), ())), preferred_element_type=F32)


def _split3(x):
    hi = x.astype(BF16)
    r1 = x - hi.astype(F32)
    mid = r1.astype(BF16)
    lo = (r1 - mid.astype(F32)).astype(BF16)
    return hi, mid, lo


def _dot_sel(sel, x, parts=3):
    sb = sel.astype(BF16)
    return sum(_dot(sb, p) for p in _split3(x)[:parts])


def _dot_sel_rhs(x, sel, parts=2):
    sb = sel.astype(BF16)
    return sum(_dot(p, sb) for p in _split3(x)[:parts])


def _dot_sel_nt(x, sel):
    sb = sel.astype(BF16)
    return sum(_dot_nt(p, sb) for p in _split3(x))


def _ada_kernel(c_ref, w_ref, b_ref, o_ref):
    c = c_ref[...]
    s = (c * _sigmoid(c)).astype(BF16)
    o_ref[0] = _dot(s, w_ref[0].astype(BF16)) + b_ref[0]


def _ada_call(c_all, w_ada, b_ada):
    depth, d, n = w_ada.shape
    rows = c_all.shape[0]
    tn = 1024
    return pl.pallas_call(
        _ada_kernel,
        grid=(depth, n // tn),
        in_specs=[pl.BlockSpec((rows, d), lambda l, j: (0, 0)),
                  pl.BlockSpec((1, d, tn), lambda l, j: (l, 0, j)),
                  pl.BlockSpec((1, 1, tn), lambda l, j: (l, 0, j))],
        out_specs=pl.BlockSpec((1, rows, tn), lambda l, j: (l, 0, j)),
        out_shape=jax.ShapeDtypeStruct((depth, rows, n), F32),
        compiler_params=_cparams(("arbitrary", "arbitrary")),
        name="ada",
    )(c_all, w_ada, b_ada.reshape(depth, 1, n))


def _normmod_kernel(x_ref, g_ref, sc_ref, sh_ref, o_ref):
    x = x_ref[0]
    y = x * lax.rsqrt(jnp.mean(x * x, axis=-1, keepdims=True) + EPS) * g_ref[...]
    o_ref[0] = (y * (1.0 + sc_ref[0]) + sh_ref[0]).astype(o_ref.dtype)


def _normmod_call(x, g, sc, sh, out_dtype, tm):
    G, R, D = x.shape
    rr = sc.shape[1]
    mod_spec = (pl.BlockSpec((1, 1, D), lambda b, i: (b, 0, 0)) if rr == 1
                else pl.BlockSpec((1, tm, D), lambda b, i: (b, i, 0)))
    return pl.pallas_call(
        _normmod_kernel,
        grid=(G, R // tm),
        in_specs=[pl.BlockSpec((1, tm, D), lambda b, i: (b, i, 0)),
                  pl.BlockSpec((1, D), lambda b, i: (0, 0)),
                  mod_spec, mod_spec],
        out_specs=pl.BlockSpec((1, tm, D), lambda b, i: (b, i, 0)),
        out_shape=jax.ShapeDtypeStruct((G, R, D), out_dtype),
        compiler_params=_cparams(("arbitrary", "arbitrary")),
        name="normmod",
    )(x, g.reshape(1, D), sc, sh)


def _rope_chunk(x, cos, sin_a, sin_b):
    return x * cos + pltpu.roll(x, 96, 1) * sin_a + pltpu.roll(x, 32, 1) * sin_b


def _inproj_kernel(h_ref, w_ref, b_ref, cos_ref, sa_ref, sb_ref, o_ref, *, tn):
    j = pl.program_id(0)
    o_ref[...] = _dot(h_ref[...], w_ref[...]) + b_ref[...]
    n_chunks = tn // LANE
    rope_tiles = -(-ROPE_COLS // tn)
    for jt in range(rope_tiles):
        chunks = min(n_chunks, (ROPE_COLS - jt * tn) // LANE)

        @pl.when(j == jt)
        def _():
            cos, sa, sb = cos_ref[...], sa_ref[...], sb_ref[...]
            for c in range(chunks):
                sl = slice(c * LANE, (c + 1) * LANE)
                o_ref[:, sl] = _rope_chunk(o_ref[:, sl], cos, sa, sb)


def _inproj_call(h, w, layer, b, cos, sin_a, sin_b, tm, tn):
    M, D = h.shape
    N = w.shape[2]
    tp = cos.shape[0] // tm
    tab = pl.BlockSpec((tm, LANE), lambda j, i: (i % tp, 0))
    return pl.pallas_call(
        functools.partial(_inproj_kernel, tn=tn),
        grid=(N // tn, M // tm),
        in_specs=[pl.BlockSpec((tm, D), lambda j, i: (i, 0)),
                  pl.BlockSpec((None, D, tn), lambda j, i: (layer, 0, j)),
                  pl.BlockSpec((1, tn), lambda j, i: (0, j)),
                  tab, tab, tab],
        out_specs=pl.BlockSpec((tm, tn), lambda j, i: (i, j)),
        out_shape=jax.ShapeDtypeStruct((M, N), F32),
        compiler_params=_cparams(("arbitrary", "arbitrary")),
        name="inproj",
    )(h, w, b, cos, sin_a, sin_b)


def _attn_prompt_kernel(q_ref, kp_ref, kc_ref, vp_ref, vc_ref, o_ref, l_ref, obuf, lbuf, *, dil, nq):
    i = pl.program_id(2)
    row = lax.broadcasted_iota(jnp.int32, (BAND, 2 * BAND), 0)
    col = lax.broadcasted_iota(jnp.int32, (BAND, 2 * BAND), 1)
    band = (col >= row) & (col <= row + BAND)
    band_first = band & ((col >= BAND) | (i > 0))
    nr = nq * BAND

    def residues(rs):
        rows_of, items = [], []
        for u, r in enumerate(rs):
            if dil == 1:
                rows, prev_rows = slice(None), slice(nr - BAND, nr)
            else:
                rows, prev_rows = pl.ds(r, nr, stride=dil), pl.ds((nr - BAND) * dil + r, BAND, stride=dil)
            rows_of.append(rows)
            q_all = q_ref[0, rows, :].astype(BF16)
            k_all = jnp.concatenate([kp_ref[0, prev_rows, :], kc_ref[0, rows, :]], axis=0).astype(BF16)
            v_all = jnp.concatenate([vp_ref[0, prev_rows, :], vc_ref[0, rows, :]], axis=0).astype(BF16)
            for sb in range(nq):
                qs = slice(sb * BAND, (sb + 1) * BAND)
                ks = slice(sb * BAND, (sb + 2) * BAND)
                for h in range(LANE // HEAD_DIM):
                    hs = slice(h * HEAD_DIM, (h + 1) * HEAD_DIM)
                    items.append((u, qs, hs, q_all[qs, hs], k_all[ks, hs], v_all[ks, hs],
                                  band_first if sb == 0 else band))
        ss = [jnp.where(it[6], _dot_nt(it[3], it[4]) * ATTN_SCALE, NEG) for it in items]
        mxs = [jnp.max(s, axis=-1, keepdims=True) for s in ss]
        es = [jnp.exp(s - mx) for s, mx in zip(ss, mxs)]
        dens = [jnp.sum(e, axis=-1, keepdims=True) for e in es]
        pvs = [_dot(e.astype(BF16), it[5]) for e, it in zip(es, items)]
        outs = [pv / den for pv, den in zip(pvs, dens)]
        lses = [jnp.broadcast_to(mx + jnp.log(den), (BAND, HEAD_DIM)) for mx, den in zip(mxs, dens)]
        for it, o, l in zip(items, outs, lses):
            obuf[it[0], it[1], it[2]] = o
            lbuf[it[0], it[1], it[2]] = l
        for u, rows in enumerate(rows_of):
            o_ref[0, rows, :] = obuf[u]
            l_ref[0, rows, :] = lbuf[u]

    if dil == 1:
        residues([0])
    elif dil <= RES_UNROLL:
        residues(list(range(dil)))
    else:
        def body(j, carry):
            residues([j * RES_UNROLL + u for u in range(RES_UNROLL)])
            return carry

        lax.fori_loop(0, dil // RES_UNROLL, body, 0)


def _attn_prompt_call(proj, g, dil):
    B, S, N = proj.shape
    nq = max(1, math.gcd(S, 512) // (BAND * dil))
    tb = nq * BAND * dil
    nb = S // tb
    hp = GROUP_W // LANE
    cq, ck, cv = (C_AQ + g * GROUP_W) // LANE, (C_AK + g * GROUP_W) // LANE, (C_AV + g * GROUP_W) // LANE

    def spec(cblk, prev):
        if prev:
            return pl.BlockSpec((1, tb, LANE), lambda b, p, i: (b, jnp.maximum(i - 1, 0), cblk + p))
        return pl.BlockSpec((1, tb, LANE), lambda b, p, i: (b, i, cblk + p))

    ospec = pl.BlockSpec((1, tb, LANE), lambda b, p, i: (b, i, p))
    o, lse = pl.pallas_call(
        functools.partial(_attn_prompt_kernel, dil=dil, nq=nq),
        grid=(B, hp, nb),
        in_specs=[spec(cq, False), spec(ck, True), spec(ck, False), spec(cv, True), spec(cv, False)],
        out_specs=[ospec, ospec],
        out_shape=[jax.ShapeDtypeStruct((B, S, GROUP_W), F32)] * 2,
        scratch_shapes=[pltpu.VMEM((min(dil, RES_UNROLL), nq * BAND, LANE), F32)] * 2,
        compiler_params=_cparams(("arbitrary", "arbitrary", "arbitrary")),
        name=f"attn_prompt_g{g}",
    )(proj, proj, proj, proj, proj)
    return o, lse


def _win_extract_kernel(x_ref, o_ref):
    o_ref[0, 0] = x_ref[0].T


def _win_extract_call(proj, g, keep):
    B, S, N = proj.shape
    chunk = min(keep, 512)
    first = (S - keep) // chunk
    cblk = C_AK // GROUP_W + g
    step = (C_AV - C_AK) // GROUP_W
    out = pl.pallas_call(
        _win_extract_kernel,
        grid=(B, 2, keep // chunk),
        in_specs=[pl.BlockSpec((1, chunk, GROUP_W), lambda b, kv, j: (b, first + j, cblk + step * kv))],
        out_specs=pl.BlockSpec((1, 1, GROUP_W, chunk), lambda b, kv, j: (b, kv, 0, j)),
        out_shape=jax.ShapeDtypeStruct((B, 2, GROUP_W, keep), F32),
        compiler_params=_cparams(("arbitrary", "arbitrary", "arbitrary")),
        name=f"win_extract_g{g}",
    )(proj)
    return out.reshape(B, 2, ATTN_HPG, HEAD_DIM, keep).transpose(0, 4, 1, 2, 3)


QROWS = 16


def _attn_sample_kernel(qkv_ref, c0_ref, c1_ref, c2_ref, *rest, T, n_prev):
    y_ref, n0_ref, n1_ref, n2_ref, q16, npad, ysc = rest[n_prev:]
    q16[...] = jnp.zeros_like(q16)
    npad[...] = jnp.zeros_like(npad)
    q16[0:T, :] = qkv_ref[0, :, 0:ATTN_WIDTH]
    lane = lax.broadcasted_iota(jnp.int32, (HEAD_DIM, LANE), 1)
    items = []
    for g, (cref, nref) in enumerate(((c0_ref, n0_ref), (c1_ref, n1_ref), (c2_ref, n2_ref))):
        Lb = cref.shape[-1]
        dil = ATTN_GROUPS[g][1]
        npad[g, 0:T, 0:GROUP_W] = qkv_ref[0, :, C_AK + g * GROUP_W:C_AK + (g + 1) * GROUP_W]
        npad[g, 0:T, GROUP_W:2 * GROUP_W] = qkv_ref[0, :, C_AV + g * GROUP_W:C_AV + (g + 1) * GROUP_W]
        new_t = npad[g].T
        ncols = Lb + LANE
        t = lax.broadcasted_iota(jnp.int32, (QROWS, ncols), 0)
        c = lax.broadcasted_iota(jnp.int32, (QROWS, ncols), 1)
        if dil == 1:
            valid = ((c >= t) & (c < Lb)) | ((c >= Lb) & (c - Lb <= t) & (c - Lb < T))
        else:
            valid = ((c < Lb) & ((c & (dil - 1)) == t)) | (c == Lb + t)
        for h in range(ATTN_HPG):
            rk = slice(h * HEAD_DIM, (h + 1) * HEAD_DIM)
            rv = slice(GROUP_W + h * HEAD_DIM, GROUP_W + (h + 1) * HEAD_DIM)
            kc, vc = cref[0, 0, 0, h], cref[0, 0, 1, h]
            kt = jnp.concatenate([kc, new_t[rk, :]], axis=1).astype(BF16)
            vt = jnp.concatenate([vc, new_t[rv, :]], axis=1).astype(BF16)
            qh = q16[:, g * GROUP_W + h * HEAD_DIM:g * GROUP_W + (h + 1) * HEAD_DIM].astype(BF16)
            items.append((qh, kt, vt, valid))
            for kv, blk, rn in ((0, kc, rk), (1, vc, rv)):
                sh = pltpu.roll(blk, Lb - T, 1)
                newc = pltpu.roll(new_t[rn, :], LANE - T, 1)
                if Lb > LANE:
                    nref[0, 0, kv, h, :, 0:Lb - LANE] = sh[:, 0:Lb - LANE]
                nref[0, 0, kv, h, :, Lb - LANE:Lb] = jnp.where(lane >= LANE - T, newc, sh[:, Lb - LANE:Lb])
    ss = [jnp.where(valid, _dot(qh, kt) * ATTN_SCALE, NEG) for qh, kt, vt, valid in items]
    mxs = [jnp.max(s, axis=-1, keepdims=True) for s in ss]
    es = [jnp.exp(s - mx) for s, mx in zip(ss, mxs)]
    dens = [jnp.sum(e, axis=-1, keepdims=True) for e in es]
    pvs = [_dot_nt(e.astype(BF16), it[2]) for e, it in zip(es, items)]
    outs = [pv / den for pv, den in zip(pvs, dens)]
    lses = [mx + jnp.log(den) for mx, den in zip(mxs, dens)]
    for h in range(ATTN_HPG):
        l0, l1, l2 = lses[h], lses[ATTN_HPG + h], lses[2 * ATTN_HPG + h]
        mx = jnp.maximum(jnp.maximum(l0, l1), l2)
        es = [jnp.exp(l0 - mx), jnp.exp(l1 - mx), jnp.exp(l2 - mx)]
        inv = 1.0 / (es[0] + es[1] + es[2])
        for g in range(3):
            c0 = g * GROUP_W + h * HEAD_DIM
            ysc[:, c0:c0 + HEAD_DIM] = outs[g * ATTN_HPG + h] * (es[g] * inv)
    y_ref[0] = ysc[0:T, :].astype(y_ref.dtype)


def _attn_sample_call(qkv, caches_t, layer, prev):
    B, T, _ = qkv.shape
    depth = caches_t[0].shape[0]
    for g, (win, dil) in enumerate(ATTN_GROUPS):
        assert caches_t[g].shape[-1] == win and win // dil == BAND and (g == 0 or T <= dil), \
            "sample attention assumes full window caches"
    cspecs = [pl.BlockSpec((1, 1) + c.shape[2:], lambda b: (layer, b, 0, 0, 0, 0)) for c in caches_t]
    n_prev = 0 if prev is None else 3
    prev_args = [] if prev is None else list(prev)
    prev_specs = [pl.BlockSpec(memory_space=pl.ANY)] * n_prev
    aliases = {} if prev is None else {4 + g: 1 + g for g in range(3)}
    outs = pl.pallas_call(
        functools.partial(_attn_sample_kernel, T=T, n_prev=n_prev),
        grid=(B,),
        in_specs=[pl.BlockSpec((1, T, 3 * ATTN_WIDTH), lambda b: (b, 0, 0))] + cspecs + prev_specs,
        out_specs=[pl.BlockSpec((1, T, ATTN_WIDTH), lambda b: (b, 0, 0))] + cspecs,
        out_shape=[jax.ShapeDtypeStruct((B, T, ATTN_WIDTH), BF16)]
        + [jax.ShapeDtypeStruct(c.shape, F32) for c in caches_t],
        scratch_shapes=[pltpu.VMEM((QROWS, ATTN_WIDTH), F32),
                        pltpu.VMEM((3, LANE, 2 * GROUP_W), F32),
                        pltpu.VMEM((QROWS, ATTN_WIDTH), F32)],
        input_output_aliases=aliases,
        compiler_params=_cparams(("arbitrary",)),
        name="attn_sample",
    )(qkv, *caches_t, *prev_args)
    return outs[0], outs[1:]


def _mlstm_kernel(mqk_ref, mv_ref, mi_ref, mf_ref, mo_ref, cst_ref, wc_ref, bc_ref, c0_ref, n0_ref, m0_ref,
                  g_ref, tri_ref, y_ref, c_ref, n_ref, m_ref, cso_ref,
                  xbuf, vbuf, cs, ns, ms, tpad, kwp, *, T, Lp, nb):
    c = pl.program_id(1)
    last = pl.num_programs(1) - 1
    W = MLSTM_WIDTH
    bs = range(nb)
    heads = range(MLSTM_HEADS)
    hsl = [slice(h * HEAD_DIM, (h + 1) * HEAD_DIM) for h in heads]
    pairs = [(b, h) for b in bs for h in heads]

    @pl.when(c == 0)
    def _():
        xbuf[...] = jnp.zeros_like(xbuf)
        vbuf[...] = jnp.zeros_like(vbuf)
        tpad[...] = jnp.zeros_like(tpad)
        kwp[...] = jnp.zeros_like(kwp)
        xbuf[:, 5:8, :] = cst_ref[...]
        cs[...] = c0_ref[...]
        ns[...] = n0_ref[...]
        ms[...] = m0_ref[...]

    w = wc_ref[...]
    tri = tri_ref[...]
    causal = tri > 0.5
    rowid = lax.broadcasted_iota(jnp.int32, (Lp, LANE), 0)
    real = rowid < T
    lane_row = lax.broadcasted_iota(jnp.int32, (1, LANE), 1)

    for b in bs:
        xbuf[b, 8:8 + T, :] = mqk_ref[b]
        vbuf[b, 0:T, :] = mv_ref[b]
    ys = [bc_ref[...] + xbuf[b, 8:8 + Lp, :] * w[3:4] + xbuf[b, 7:7 + Lp, :] * w[2:3]
          + xbuf[b, 6:6 + Lp, :] * w[1:2] + xbuf[b, 5:5 + Lp, :] * w[0:1] for b in bs]
    tails = [xbuf[b, 8 + T - 3:8 + T, :] for b in bs]
    for b in bs:
        xbuf[b, 5:8, :] = tails[b]

    @pl.when(c == last)
    def _():
        for b in bs:
            cso_ref[b] = tails[b]

    qks = [y * _sigmoid(y) for y in ys]
    igs, lfs = [], []
    for b in bs:
        tpad[b, 0:T, :] = mi_ref[b]
        igs.append(jnp.where(real, tpad[b, 0:Lp, :], NEG))
        tpad[b, 0:T, :] = mf_ref[b]
        lfs.append(jnp.where(real, _log_sigmoid(tpad[b, 0:Lp, :]), 0.0))
    Fs = [_dot_sel(tri, lf) for lf in lfs]
    m_rows = [ms[b] for b in bs]
    inter_all = [Fs[b] + m_rows[b] for b in bs]
    FTs, ITs = [], []
    for b in bs:
        tpad[b, 0:Lp, :] = Fs[b]
        FTs.append(tpad[b].T)
        tpad[b, 0:Lp, :] = igs[b]
        ITs.append(tpad[b].T)
    v_alls = [vbuf[b, 0:Lp, :] for b in bs]
    n_alls = [ns[b] for b in bs]

    qs = [qks[b][:, hsl[h]] for b, h in pairs]
    ks = [qks[b][:, W + h * HEAD_DIM:W + (h + 1) * HEAD_DIM] * (HEAD_DIM ** -0.5) for b, h in pairs]
    qbs = [q.astype(BF16) for q in qs]
    vbs = [v_alls[b][:, hsl[h]].astype(BF16) for b, h in pairs]
    sqk = [_dot_nt(qb, k.astype(BF16)) for qb, k in zip(qbs, ks)]
    qcs = [_dot(qbs[i], cs[b, h].astype(BF16)) for i, (b, h) in enumerate(pairs)]
    nrows = [n_alls[b][h:h + 1, :] for b, h in pairs]
    qn = [_dot_nt(qbs[i], jnp.broadcast_to(nrows[i], (16, HEAD_DIM)).astype(BF16))[:, 0:1]
          for i in range(len(pairs))]

    fcols = [Fs[b][:, h:h + 1] for b, h in pairs]
    icols = [igs[b][:, h:h + 1] for b, h in pairs]
    inters = [inter_all[b][:, h:h + 1] for b, h in pairs]
    logws = [jnp.where(causal, fcols[i] - FTs[b][h:h + 1, 0:Lp] + ITs[b][h:h + 1, 0:Lp], NEG)
             for i, (b, h) in enumerate(pairs)]
    rmax = [jnp.max(lw, axis=-1, keepdims=True) for lw in logws]
    mts = [jnp.maximum(r, it) for r, it in zip(rmax, inters)]
    As = [jnp.exp(lw - mt) * s for lw, mt, s in zip(logws, mts, sqk)]
    gqs = [jnp.exp(it - mt) for it, mt in zip(inters, mts)]
    asum = [jnp.sum(A, axis=-1, keepdims=True) for A in As]
    dens = [a + g * q for a, g, q in zip(asum, gqs, qn)]
    a_bf = [A.astype(BF16) for A in As]
    avs = [_dot(a, v) for a, v in zip(a_bf, vbs)]
    mLs = [mt[Lp - 1:Lp, :] for mt in mts]
    flast = [Fs[b][Lp - 1:Lp, h:h + 1] for b, h in pairs]
    wLs = [jnp.exp(fl - fc + ic - mL) for fl, fc, ic, mL in zip(flast, fcols, icols, mLs)]
    gls = [jnp.exp(flast[i] + m_rows[b][0:1, h:h + 1] - mLs[i]) for i, (b, h) in enumerate(pairs)]
    kws = [k * wl for k, wl in zip(ks, wLs)]
    for i, (b, h) in enumerate(pairs):
        kwp[b, 0:Lp, hsl[h]] = kws[i]
    ksum = [jnp.sum(kw, axis=0, keepdims=True) for kw in kws]
    for i, (b, h) in enumerate(pairs):
        ns[b, h:h + 1, :] = gls[i] * nrows[i] + ksum[i]
    for b in bs:
        m_new = m_rows[b]
        for h in heads:
            m_new = jnp.where(lane_row == h, mLs[b * MLSTM_HEADS + h], m_new)
        ms[b] = m_new

    kwts = [kwp[b].T for b in bs]
    upds = [_dot(kwts[b][hsl[h], 0:Lp].astype(BF16), vbs[i]) for i, (b, h) in enumerate(pairs)]

    nums = [av + g * qc for av, g, qc in zip(avs, gqs, qcs)]
    hhs = [nu / jnp.maximum(jnp.abs(de), jnp.exp(-mt)) for nu, de, mt in zip(nums, dens, mts)]
    msq = [jnp.mean(hh * hh, axis=-1, keepdims=True) for hh in hhs]
    hns = [hhs[i] * lax.rsqrt(msq[i] + EPS) * g_ref[0:1, hsl[h]] for i, (b, h) in enumerate(pairs)]
    ogs = [_sigmoid(mo_ref[b, :, hsl[h]]) for b, h in pairs]
    for i, (b, h) in enumerate(pairs):
        yo = hns[i] * ogs[i] if Lp == T else hns[i][0:T] * ogs[i]
        y_ref[b, :, hsl[h]] = yo.astype(y_ref.dtype)
    for i, (b, h) in enumerate(pairs):
        cs[b, h] = gls[i] * cs[b, h] + upds[i]

    @pl.when(c == last)
    def _():
        c_ref[...] = cs[...]
        n_ref[...] = ns[...]
        m_ref[...] = ms[...]


def _mlstm_call(proj, conv_state, w_conv, b_conv, c0, n0, m0, g_mlstm, chunk):
    B, S, N = proj.shape
    T = chunk
    nc = S // T
    Lp = max(16, T)
    H = MLSTM_HEADS
    nb = math.gcd(B, MIXER_BATCH)
    m0p = jnp.pad(m0, ((0, 0), (0, LANE - H))).reshape(B, 1, LANE)
    tri = jnp.tril(jnp.ones((Lp, Lp), F32))

    def col(width, off):
        blk = off // width
        return pl.BlockSpec((nb, T, width), lambda b, c: (b, c, blk))

    def const(shape):
        nd = len(shape)
        return pl.BlockSpec(shape, lambda b, c: (0,) * nd)

    def per_b(shape):
        nd = len(shape)
        return pl.BlockSpec((nb,) + shape, lambda b, c: (b,) + (0,) * nd)

    outs = pl.pallas_call(
        functools.partial(_mlstm_kernel, T=T, Lp=Lp, nb=nb),
        grid=(B // nb, nc),
        in_specs=[col(2 * MLSTM_WIDTH, C_MQK), col(MLSTM_WIDTH, C_MV), col(LANE, C_MI), col(LANE, C_MF),
                  col(MLSTM_WIDTH, C_MO), per_b((MLSTM_CONV - 1, 2 * MLSTM_WIDTH)),
                  const((MLSTM_CONV, 2 * MLSTM_WIDTH)), const((1, 2 * MLSTM_WIDTH)),
                  per_b((H, HEAD_DIM, HEAD_DIM)), per_b((H, HEAD_DIM)), per_b((1, LANE)),
                  const((1, MLSTM_WIDTH)), const((Lp, Lp))],
        out_specs=[pl.BlockSpec((nb, T, MLSTM_WIDTH), lambda b, c: (b, c, 0)),
                   per_b((H, HEAD_DIM, HEAD_DIM)), per_b((H, HEAD_DIM)), per_b((1, LANE)),
                   per_b((MLSTM_CONV - 1, 2 * MLSTM_WIDTH))],
        out_shape=[jax.ShapeDtypeStruct((B, S, MLSTM_WIDTH), BF16),
                   jax.ShapeDtypeStruct((B, H, HEAD_DIM, HEAD_DIM), F32),
                   jax.ShapeDtypeStruct((B, H, HEAD_DIM), F32),
                   jax.ShapeDtypeStruct((B, 1, LANE), F32),
                   jax.ShapeDtypeStruct((B, MLSTM_CONV - 1, 2 * MLSTM_WIDTH), F32)],
        scratch_shapes=[pltpu.VMEM((nb, 8 + Lp, 2 * MLSTM_WIDTH), F32),
                        pltpu.VMEM((nb, Lp, MLSTM_WIDTH), F32),
                        pltpu.VMEM((nb, H, HEAD_DIM, HEAD_DIM), F32),
                        pltpu.VMEM((nb, H, HEAD_DIM), F32),
                        pltpu.VMEM((nb, 1, LANE), F32),
                        pltpu.VMEM((nb, LANE, LANE), F32),
                        pltpu.VMEM((nb, LANE, MLSTM_WIDTH), F32)],
        compiler_params=_cparams(("arbitrary", "arbitrary")),
        name="mlstm",
    )(proj, proj, proj, proj, proj, conv_state, w_conv, b_conv.reshape(1, -1), c0, n0, m0p,
      g_mlstm.reshape(1, -1), tri)
    y, C, n, m, cso = outs
    return y, C, n, m[:, 0, :H], cso


def _gla_kernel(gqk_ref, gv_ref, gg_ref, ga_ref, wa_ref, wat_ref, s0_ref, g_ref, tri_ref, bd_ref, ee_ref,
                y_ref, s_ref, qkp, vp, gap, sbd, osc, *, T, Lp, nb):
    c = pl.program_id(1)
    last = pl.num_programs(1) - 1
    KW, VW = GLA_K_WIDTH, GLA_V_WIDTH
    bs = range(nb)
    heads = range(GLA_HEADS)

    @pl.when(c == 0)
    def _():
        qkp[...] = jnp.zeros_like(qkp)
        vp[...] = jnp.zeros_like(vp)
        gap[...] = jnp.zeros_like(gap)
        sbd[...] = jnp.zeros_like(sbd)
        osc[...] = jnp.zeros_like(osc)
        for b in bs:
            for h in heads:
                sbd[b, h * GLA_DK:(h + 1) * GLA_DK, h * GLA_DV:(h + 1) * GLA_DV] = s0_ref[b, h]

    lane = lax.broadcasted_iota(jnp.int32, (T, LANE), 1)
    for b in bs:
        qkp[b, 0:T, :] = gqk_ref[b]
        vp[b, 0:T, :] = gv_ref[b]
        gap[b, 0:T, :] = jnp.where(lane == GLA_RANK, 1.0, ga_ref[b])

    tri = tri_ref[...]
    rowid = lax.broadcasted_iota(jnp.int32, (Lp, KW), 0)
    colid = lax.broadcasted_iota(jnp.int32, (KW, LANE), 1)
    wa_hi, wa_lo, _ = _split3(wa_ref[...])
    wat_hi, wat_lo, _ = _split3(wat_ref[...])
    ga_bs = [gap[b, 0:Lp, :].astype(BF16) for b in bs]
    gat_bs = [gap[b].T.astype(BF16) for b in bs]
    las = [_dot(g, wa_hi) + _dot(g, wa_lo) for g in ga_bs]
    las = [jnp.where(rowid < T, _log_sigmoid(la) / GLA_TAU, 0.0) for la in las]
    bcs = [_dot_sel(tri[0:Lp, 0:Lp], la) for la in las]
    lats = [_dot(wat_hi, g) + _dot(wat_lo, g) for g in gat_bs]
    lats = [jnp.where(colid < T, _log_sigmoid(lat) / GLA_TAU, 0.0) for lat in lats]
    bcts = [_dot_sel_nt(lat, tri) for lat in lats]
    blcols = [bct[:, LANE - 1:LANE] for bct in bcts]
    blrows = [bc[Lp - 1:Lp, :] for bc in bcs]
    kts = [qkp[b].T[KW:2 * KW, :] for b in bs]
    bd = bd_ref[...]
    min_decay = jnp.min(blrows[0])
    for b in bs[1:]:
        min_decay = jnp.minimum(min_decay, jnp.min(blrows[b]))
    safe = min_decay >= -GLA_SAFE_DECAY

    @pl.when(safe)
    def _():
        causal = tri[0:Lp, 0:Lp] > 0.5
        qbs = [(qkp[b, 0:Lp, 0:KW] * (GLA_DK ** -0.5) * jnp.exp(bcs[b])).astype(BF16) for b in bs]
        kbs = [(qkp[b, 0:Lp, KW:2 * KW] * jnp.exp(-bcs[b])).astype(BF16) for b in bs]
        vbs = [vp[b, 0:Lp, :].astype(BF16) for b in bs]
        o_inters = [_dot(qbs[b], sbd[b].astype(BF16)) for b in bs]
        pairs = [(b, h) for b in bs for h in heads]
        scores = [_dot_nt(qbs[b][:, h * GLA_DK:(h + 1) * GLA_DK], kbs[b][:, h * GLA_DK:(h + 1) * GLA_DK])
                  for b, h in pairs]
        a_bf = [jnp.where(causal, s, 0.0).astype(BF16) for s in scores]
        ovs = [_dot(a_bf[i], vbs[b][:, h * GLA_DV:(h + 1) * GLA_DV]) for i, (b, h) in enumerate(pairs)]
        klts = [(kts[b] * jnp.exp(blcols[b] - bcts[b])).astype(BF16) for b in bs]
        upds = [_dot(klts[b], vp[b].astype(BF16)) for b in bs]
        for b in bs:
            osc[b, 0:Lp, :] = o_inters[b] + jnp.concatenate(ovs[b * GLA_HEADS:(b + 1) * GLA_HEADS], axis=1)
            sbd[b] = bd * (jnp.exp(blcols[b]) * sbd[b] + upds[b])

    @pl.when(jnp.logical_not(safe))
    def _():
        srow = lax.broadcasted_iota(jnp.int32, (LANE, LANE), 0)
        for b in bs:
            def body(t, carry, b=b):
                sel = (srow == t).astype(F32)
                lac = jnp.dot(lats[b], sel, precision=HI, preferred_element_type=F32)
                kc = jnp.dot(kts[b], sel, precision=HI, preferred_element_type=F32)
                dec = jnp.concatenate([jnp.exp(lac)] * (VW // LANE), axis=1)
                kcw = jnp.concatenate([kc] * (VW // LANE), axis=1)
                vrow = vp[b, pl.ds(t, 1), :]
                snew = bd * (dec * sbd[b] + kcw * vrow)
                sbd[b] = snew
                qrow = jnp.broadcast_to(qkp[b, pl.ds(t, 1), 0:KW] * (GLA_DK ** -0.5), (8, KW))
                orow = jnp.dot(qrow, snew, precision=HI, preferred_element_type=F32)
                osc[b, pl.ds(t, 1), :] = orow[0:1]
                return carry

            lax.fori_loop(0, T, body, 0)

    os_ = [osc[b, 0:Lp, :] for b in bs]
    mss = [_dot_sel_rhs(o * o, ee_ref[...]) for o in os_]
    ogs = [o * lax.rsqrt(ms + EPS) * g_ref[...] for o, ms in zip(os_, mss)]
    for b in bs:
        gg = gg_ref[b]
        yo = (ogs[b] if Lp == T else ogs[b][0:T]) * (gg * _sigmoid(gg))
        y_ref[b] = yo.astype(y_ref.dtype)

    @pl.when(c == last)
    def _():
        for b in bs:
            for h in heads:
                s_ref[b, h] = sbd[b, h * GLA_DK:(h + 1) * GLA_DK, h * GLA_DV:(h + 1) * GLA_DV]


def _gla_call(proj, w_a2, b_a2, s0, g_gla, chunk):
    B, S, N = proj.shape
    T = chunk
    nc = S // T
    Lp = max(16, T)
    H, KW, VW = GLA_HEADS, GLA_K_WIDTH, GLA_V_WIDTH
    wa = jnp.zeros((LANE, KW), F32).at[:GLA_RANK].set(w_a2).at[GLA_RANK].set(b_a2)
    tri = jnp.tril(jnp.ones((LANE, LANE), F32))
    hk = jnp.arange(KW) // GLA_DK
    hv = jnp.arange(VW) // GLA_DV
    bd = (hk[:, None] == hv[None, :]).astype(F32)
    ee = (hv[:, None] == hv[None, :]).astype(F32) / GLA_DV

    nb = math.gcd(B, MIXER_BATCH)

    def col(width, off):
        blk = off // width
        return pl.BlockSpec((nb, T, width), lambda b, c: (b, c, blk))

    def const(shape):
        nd = len(shape)
        return pl.BlockSpec(shape, lambda b, c: (0,) * nd)

    y, s = pl.pallas_call(
        functools.partial(_gla_kernel, T=T, Lp=Lp, nb=nb),
        grid=(B // nb, nc),
        in_specs=[col(2 * KW, C_GQK), col(VW, C_GV), col(VW, C_GG), col(LANE, C_GA),
                  const((LANE, KW)), const((KW, LANE)),
                  pl.BlockSpec((nb, H, GLA_DK, GLA_DV), lambda b, c: (b, 0, 0, 0)),
                  const((1, VW)), const((LANE, LANE)), const((KW, VW)), const((VW, VW))],
        out_specs=[pl.BlockSpec((nb, T, VW), lambda b, c: (b, c, 0)),
                   pl.BlockSpec((nb, H, GLA_DK, GLA_DV), lambda b, c: (b, 0, 0, 0))],
        out_shape=[jax.ShapeDtypeStruct((B, S, VW), BF16),
                   jax.ShapeDtypeStruct((B, H, GLA_DK, GLA_DV), F32)],
        scratch_shapes=[pltpu.VMEM((nb, LANE, 2 * KW), F32),
                        pltpu.VMEM((nb, LANE, VW), F32),
                        pltpu.VMEM((nb, LANE, LANE), F32),
                        pltpu.VMEM((nb, KW, VW), F32),
                        pltpu.VMEM((nb, LANE, VW), F32)],
        compiler_params=_cparams(("arbitrary", "arbitrary")),
        name="gla",
    )(proj, proj, proj, proj, wa, wa.T, s0, g_gla.reshape(1, -1), tri, bd, ee)
    return y, s


def _outproj_kernel(*refs, n_attn):
    attn_refs = refs[:n_attn]
    ym_ref, yg_ref, w_ref, x_ref, gt_ref, g2_ref, sc_ref, sh_ref, xo_ref, h2_ref = refs[n_attn:]
    a, b = ATTN_WIDTH, ATTN_WIDTH + MLSTM_WIDTH
    if n_attn == 1:
        acc = _dot(attn_refs[0][0], w_ref[0:a, :])
    else:
        l0, l1, l2 = attn_refs[3][0], attn_refs[4][0], attn_refs[5][0]
        mx = jnp.maximum(jnp.maximum(l0, l1), l2)
        es = [jnp.exp(l0 - mx), jnp.exp(l1 - mx), jnp.exp(l2 - mx)]
        inv = 1.0 / (es[0] + es[1] + es[2])
        acc = None
        for g in range(3):
            yg_attn = (attn_refs[g][0] * (es[g] * inv)).astype(BF16)
            part = _dot(yg_attn, w_ref[g * GROUP_W:(g + 1) * GROUP_W, :])
            acc = part if acc is None else part + acc
    acc = acc + _dot(ym_ref[0], w_ref[a:b, :])
    acc = acc + _dot(yg_ref[0], w_ref[b:, :])
    x = x_ref[0] + gt_ref[0] * acc
    xo_ref[0] = x
    y = x * lax.rsqrt(jnp.mean(x * x, axis=-1, keepdims=True) + EPS) * g2_ref[...]
    h2_ref[0] = (y * (1.0 + sc_ref[0]) + sh_ref[0]).astype(h2_ref.dtype)


def _outproj_call(ya, ym, yg, w_out, layer, x, gt, g2, sc, sh, tm):
    G, R, D = x.shape
    rr = gt.shape[1]
    mod_spec = (pl.BlockSpec((1, 1, D), lambda b, i: (b, 0, 0)) if rr == 1
                else pl.BlockSpec((1, tm, D), lambda b, i: (b, i, 0)))

    def act(width):
        return pl.BlockSpec((1, tm, width), lambda b, i: (b, i, 0))

    attn = list(ya) if isinstance(ya, (list, tuple)) else [ya]
    return pl.pallas_call(
        functools.partial(_outproj_kernel, n_attn=len(attn)),
        grid=(G, R // tm),
        in_specs=[act(a_.shape[-1]) for a_ in attn] + [act(MLSTM_WIDTH), act(GLA_V_WIDTH),
                  pl.BlockSpec((None,) + w_out.shape[1:], lambda b, i: (layer, 0, 0)),
                  act(D), mod_spec, pl.BlockSpec((1, D), lambda b, i: (0, 0)), mod_spec, mod_spec],
        out_specs=[act(D), act(D)],
        out_shape=[jax.ShapeDtypeStruct((G, R, D), F32), jax.ShapeDtypeStruct((G, R, D), BF16)],
        compiler_params=_cparams(("arbitrary", "arbitrary")),
        name="outproj",
    )(*attn, ym, yg, w_out, x, gt, g2.reshape(1, D), sc, sh)


def _ffn_kernel(h_ref, wg_ref, wu_ref, wc_ref, bc_ref, wo_ref, x_ref, gt_ref, init_ref,
                xo_ref, st_ref, ubuf, cbuf, *, tm, u, R, tiles_per_seq):
    m = pl.program_id(1)
    f = pl.program_id(2)
    nf = pl.num_programs(2)
    h = h_ref[0]

    @pl.when(m % tiles_per_seq == 0)
    def _():
        ubuf[0:R, :] = init_ref[0]

    @pl.when(m % tiles_per_seq != 0)
    def _():
        ubuf[0:R, :] = cbuf[f]

    @pl.when(f == 0)
    def _():
        xo_ref[0] = jnp.zeros_like(xo_ref[0])

    tf = wg_ref.shape[1]
    chunks = [slice(c0, min(c0 + FFN_SUB, tf)) for c0 in range(0, tf, FFN_SUB)]

    def up_matmuls(cs_):
        return _dot(h, wg_ref[:, cs_]), _dot(h, wu_ref[:, cs_])

    pending = up_matmuls(chunks[0])
    acc = xo_ref[0]
    for ci, cs_ in enumerate(chunks):
        ug, uu = pending
        if ci + 1 < len(chunks):
            pending = up_matmuls(chunks[ci + 1])
        ubuf[R:R + tm, cs_] = ug
        w = wc_ref[:, cs_]
        gate = (bc_ref[:, cs_] + ug * w[2:3] + ubuf[R - u:R - u + tm, cs_] * w[1:2]
                + ubuf[R - 2 * u:R - 2 * u + tm, cs_] * w[0:1])
        act = (gate * _sigmoid(gate) * uu).astype(BF16)
        acc = _dot(act, wo_ref[cs_, :]) + acc
    xo_ref[0] = acc
    tail = ubuf[tm:tm + R, :]
    cbuf[f] = tail
    st_ref[0, 0] = tail

    @pl.when(f == nf - 1)
    def _():
        xo_ref[0] = x_ref[0] + gt_ref[0] * xo_ref[0]


def _ffn_call(h2, w_gu, w_conv, b_conv, w_o, layer, x, gt, init, tm, tf, u):
    G, rows, D = x.shape
    F = w_gu.shape[2] // 2
    R = max(8, 2 * u)
    nm, nf = rows // tm, F // tf
    rr = gt.shape[1]
    mod_spec = (pl.BlockSpec((1, 1, D), lambda b, i, f: (b, 0, 0)) if rr == 1
                else pl.BlockSpec((1, tm, D), lambda b, i, f: (b, i, 0)))
    xo, st = pl.pallas_call(
        functools.partial(_ffn_kernel, tm=tm, u=u, R=R, tiles_per_seq=nm),
        grid=(G, nm, nf),
        in_specs=[pl.BlockSpec((1, tm, D), lambda b, i, f: (b, i, 0)),
                  pl.BlockSpec((None, D, tf), lambda b, i, f: (layer, 0, f)),
                  pl.BlockSpec((None, D, tf), lambda b, i, f: (layer, 0, nf + f)),
                  pl.BlockSpec((FFN_CONV, tf), lambda b, i, f: (0, f)),
                  pl.BlockSpec((1, tf), lambda b, i, f: (0, f)),
                  pl.BlockSpec((None, tf, D), lambda b, i, f: (layer, f, 0)),
                  pl.BlockSpec((1, tm, D), lambda b, i, f: (b, i, 0)),
                  mod_spec,
                  pl.BlockSpec((1, R, tf), lambda b, i, f: (b, 0, f))],
        out_specs=[pl.BlockSpec((1, tm, D), lambda b, i, f: (b, i, 0)),
                   pl.BlockSpec((1, 1, R, tf), lambda b, i, f: (b, i, 0, f))],
        out_shape=[jax.ShapeDtypeStruct((G, rows, D), F32),
                   jax.ShapeDtypeStruct((G, nm, R, F), F32)],
        scratch_shapes=[pltpu.VMEM((R + tm, tf), F32), pltpu.VMEM((nf, R, tf), F32)],
        compiler_params=_cparams(("arbitrary", "arbitrary", "arbitrary")),
        name="ffn",
    )(h2, w_gu, w_gu, w_conv, b_conv.reshape(1, F), w_o, x, gt, init)
    return xo, st[:, nm - 1]


N_ATTN_COLS = 3 * ATTN_WIDTH
N_GATE_COL = N_ATTN_COLS + 3 * MLSTM_WIDTH
N_TAIL_COL = N_GATE_COL + 2 * MLSTM_HEADS
PACK_BLOCKS = N_PACK // LANE


def _pack_src(j, n_in):
    blk_mi, blk_mf, blk_mqk, blk_mo = C_MI // LANE, C_MF // LANE, C_MQK // LANE, C_MO // LANE
    last = PACK_BLOCKS - 1
    last_real = n_in - (N_TAIL_COL + (last - blk_mo) * LANE)
    start = jnp.where(j < blk_mi, j * LANE,
            jnp.where(j == blk_mi, N_GATE_COL,
            jnp.where(j == blk_mf, N_GATE_COL + MLSTM_HEADS,
            jnp.where(j < blk_mo, N_ATTN_COLS + (j - blk_mqk) * LANE,
            jnp.where(j < last, N_TAIL_COL + (j - blk_mo) * LANE, n_in - LANE)))))
    valid = jnp.where((j == blk_mi) | (j == blk_mf), MLSTM_HEADS, jnp.where(j == last, last_real, LANE))
    return start, valid, last_real


def _pack_w_kernel(wt_hbm, o_ref, buf, sem, *, n_in, depth):
    j = pl.program_id(0)
    nblk = pl.num_programs(0)

    def copies(jj, slot):
        start, _, _ = _pack_src(jj, n_in)
        return [pltpu.make_async_copy(wt_hbm.at[pl.ds(start, LANE), l, :], buf.at[slot, l], sem.at[slot, l])
                for l in range(depth)]

    @pl.when(j == 0)
    def _():
        for cp in copies(j, 0):
            cp.start()

    @pl.when(j + 1 < nblk)
    def _():
        for cp in copies(j + 1, (j + 1) % 2):
            cp.start()

    slot = j % 2
    for cp in copies(j, slot):
        cp.wait()
    _, valid, last_real = _pack_src(j, n_in)
    row = lax.broadcasted_iota(jnp.int32, (LANE, 1), 0)
    for l in range(depth):
        @pl.when(j == nblk - 1)
        def _():
            tail = buf[slot, l, LANE - last_real:LANE, :]
            buf[slot, l, 0:last_real, :] = tail

        x = jnp.where(row < valid, buf[slot, l], 0.0)
        o_ref[l] = x.T.astype(BF16)


def _pack_w_call(w_in):
    depth, d, n_in = w_in.shape
    wt = w_in.transpose(2, 0, 1)
    return pl.pallas_call(
        functools.partial(_pack_w_kernel, n_in=n_in, depth=depth),
        grid=(PACK_BLOCKS,),
        in_specs=[pl.BlockSpec(memory_space=pl.ANY)],
        out_specs=pl.BlockSpec((depth, d, LANE), lambda j: (0, 0, j)),
        out_shape=jax.ShapeDtypeStruct((depth, d, N_PACK), BF16),
        scratch_shapes=[pltpu.VMEM((2, depth, LANE, d), F32), pltpu.SemaphoreType.DMA((2, depth))],
        compiler_params=_cparams(("arbitrary",)),
        name="pack_w_in",
    )(wt)


def _pack_in_proj(w_in, b_in):
    def split(a):
        out, off = [], 0
        for s in IN_SPLITS:
            out.append(a[..., off:off + s])
            off += s
        return out

    def pad(a, n):
        return jnp.pad(a, [(0, 0)] * (a.ndim - 1) + [(0, n - a.shape[-1])])

    def pack(a):
        aq, ak, av, mqk, mv, mi, mf, mo, gq, gk, gv, gg, ga = split(a)
        return jnp.concatenate([aq, ak, av, pad(mi, LANE), pad(mf, LANE), mqk, mv, mo, gq, gk, gv, gg,
                                pad(ga, LANE)], axis=-1)

    return _pack_w_call(w_in), pack(b_in)


def _rope_tables(pos):
    half = HEAD_DIM // 2
    inv_freq = jnp.power(ROPE_THETA, -jnp.arange(half, dtype=F32) / half)
    ang = pos.astype(F32)[:, None] * inv_freq[None, :]
    cos, sin = jnp.cos(ang), jnp.sin(ang)
    zero = jnp.zeros_like(sin)
    reps = LANE // HEAD_DIM
    return (jnp.tile(jnp.concatenate([cos, cos], -1), (1, reps)),
            jnp.tile(jnp.concatenate([-sin, zero], -1), (1, reps)),
            jnp.tile(jnp.concatenate([zero, sin], -1), (1, reps)))


def _pick_tile(n, pref):
    t = math.gcd(n, pref)
    return t


def kernel(x_prompt, x_sample, c_prompt, c_sample, cache_win0_kv, cache_win1_kv, cache_win2_kv, state_mlstm_C, state_mlstm_n, state_mlstm_m, state_mlstm_conv, state_gla_S, state_ffn_conv, w_ada, b_ada, g_norm1, g_norm2, w_in, b_in, w_mconv, b_mconv, g_mlstm, w_gla_a2, b_gla_a2, g_gla, w_out, w_ff_in, w_fconv, b_fconv, w_ff_out, g_final):
    B, S, D = x_prompt.shape
    Bs, Ts, _ = x_sample.shape
    depth = w_ada.shape[0]
    d_ff = w_fconv.shape[-1]
    caches = (cache_win0_kv, cache_win1_kv, cache_win2_kv)
    Ms = Bs * Ts

    n_c = B + Bs
    rows_c = -(-n_c // 8) * 8
    c_all = jnp.pad(jnp.concatenate([c_prompt, c_sample], axis=0), ((0, rows_c - n_c), (0, 0)))
    mod = _ada_call(c_all, w_ada, b_ada).reshape(depth, rows_c, 6, D)

    w_in_p, b_in_p = _pack_in_proj(w_in, b_in)
    w_out_b = w_out.astype(BF16)
    w_gu = w_ff_in.astype(BF16)
    w_o = w_ff_out.astype(BF16)

    rope_p = _rope_tables(jnp.arange(S))
    rope_s = _rope_tables(PAST_LEN + jnp.repeat(jnp.arange(Ts), Bs))

    tm_p = _pick_tile(S, 512)
    tm_in = _pick_tile(S, 1024)
    tm_ff = _pick_tile(S, 1024)
    tn = N_PACK // 4
    tf = _pick_tile(d_ff, 512)

    xp = x_prompt
    xs = x_sample.transpose(1, 0, 2).reshape(1, Ms, D)
    zeros_p = {
        'mconv': jnp.zeros((B, MLSTM_CONV - 1, 2 * MLSTM_WIDTH), F32),
        'C': jnp.zeros((B, MLSTM_HEADS, HEAD_DIM, HEAD_DIM), F32),
        'n': jnp.zeros((B, MLSTM_HEADS, HEAD_DIM), F32),
        'm': jnp.zeros((B, MLSTM_HEADS), F32),
        'S': jnp.zeros((B, GLA_HEADS, GLA_DK, GLA_DV), F32),
        'fconv': jnp.zeros((B, 8, d_ff), F32),
    }
    names = ('win0', 'win1', 'win2', 'C', 'n', 'm', 'mconv', 'S', 'fconv')
    col_p = {k: [] for k in names}
    col_s = {k: [] for k in names}
    caches_t = [c.transpose(0, 1, 3, 4, 5, 2) for c in caches]
    new_caches = None

    for l in range(depth):
        mp = mod[l, :B]
        ms_ = jnp.tile(mod[l, B:B + Bs], (Ts, 1, 1))

        def mods_p(i):
            return mp[:, i:i + 1, :]

        def mods_s(i):
            return ms_[None, :, i, :]

        h = _normmod_call(xp, g_norm1[l], mods_p(1), mods_p(0), BF16, tm_p)
        proj = _inproj_call(h.reshape(B * S, D), w_in_p, l, b_in_p[l][None], *rope_p, tm_in, tn)
        proj = proj.reshape(B, S, N_PACK)
        os_, ls_ = [], []
        for g, (win, dil) in enumerate(ATTN_GROUPS):
            o, lse = _attn_prompt_call(proj, g, dil)
            os_.append(o)
            ls_.append(lse)
            col_p[f'win{g}'].append(_win_extract_call(proj, g, min(win, S)))
        ya = os_ + ls_
        ym, C, n, m, cso = _mlstm_call(proj, zeros_p['mconv'], w_mconv[l], b_mconv[l], zeros_p['C'],
                                        zeros_p['n'], zeros_p['m'], g_mlstm[l], math.gcd(S, MLSTM_CHUNK))
        yg, Sg = _gla_call(proj, w_gla_a2[l], b_gla_a2[l], zeros_p['S'], g_gla[l], math.gcd(S, GLA_CHUNK))
        xp, h2 = _outproj_call(ya, ym, yg, w_out_b, l, xp, mods_p(2), g_norm2[l], mods_p(4), mods_p(3), tm_p)
        xp, fst = _ffn_call(h2, w_gu, w_fconv[l], b_fconv[l], w_o, l, xp, mods_p(5),
                            zeros_p['fconv'], tm_ff, tf, 1)
        for k_, v_ in (('C', C), ('n', n), ('m', m), ('mconv', cso), ('S', Sg), ('fconv', fst[:, 6:8])):
            col_p[k_].append(v_)

        h = _normmod_call(xs, g_norm1[l], mods_s(1), mods_s(0), BF16, Ms)
        proj = _inproj_call(h.reshape(Ms, D), w_in_p, l, b_in_p[l][None], *rope_s, Ms, tn)
        proj_b = proj.reshape(Ts, Bs, N_PACK).transpose(1, 0, 2)
        ya, new_caches = _attn_sample_call(proj_b[:, :, :3 * ATTN_WIDTH], caches_t, l, new_caches)
        ym, C, n, m, cso = _mlstm_call(proj_b, state_mlstm_conv[l], w_mconv[l], b_mconv[l], state_mlstm_C[l],
                                        state_mlstm_n[l], state_mlstm_m[l], g_mlstm[l], Ts)
        yg, Sg = _gla_call(proj_b, w_gla_a2[l], b_gla_a2[l], state_gla_S[l], g_gla[l], Ts)

        def tmaj(a):
            return a.transpose(1, 0, 2).reshape(1, Ms, a.shape[-1])

        xs, h2 = _outproj_call(tmaj(ya), tmaj(ym), tmaj(yg), w_out_b, l, xs, mods_s(2), g_norm2[l],
                               mods_s(4), mods_s(3), Ms)
        n_st = (FFN_CONV - 1) * Bs
        r_st = max(8, n_st)
        init = state_ffn_conv[l].transpose(1, 0, 2).reshape(1, n_st, d_ff)
        init = jnp.pad(init, ((0, 0), (r_st - n_st, 0), (0, 0)))
        xs, fst = _ffn_call(h2, w_gu, w_fconv[l], b_fconv[l], w_o, l, xs, mods_s(5), init, Ms, tf, Bs)
        fst = fst[:, r_st - n_st:].reshape(FFN_CONV - 1, Bs, d_ff).transpose(1, 0, 2)
        for k_, v_ in (('C', C), ('n', n), ('m', m), ('mconv', cso), ('S', Sg), ('fconv', fst)):
            col_s[k_].append(v_)

    y_prompt = _normmod_call(xp, g_final, jnp.zeros((B, 1, D), F32), jnp.zeros((B, 1, D), F32), F32, tm_p)
    y_sample = _normmod_call(xs, g_final, jnp.zeros((1, 1, D), F32), jnp.zeros((1, 1, D), F32), F32, Ms)
    y_sample = y_sample.reshape(Ts, Bs, D).transpose(1, 0, 2)
    sp = {k: jnp.stack(v, axis=0) for k, v in col_p.items()}
    ss = {k: jnp.stack(v, axis=0) for k, v in col_s.items() if v}
    for g in range(3):
        ss[f'win{g}'] = new_caches[g].transpose(0, 1, 5, 2, 3, 4)
    return (y_prompt, y_sample, sp['win0'], ss['win0'], sp['win1'], ss['win1'], sp['win2'], ss['win2'],
            sp['C'], ss['C'], sp['n'], ss['n'], sp['m'], ss['m'], sp['mconv'], ss['mconv'],
            sp['S'], ss['S'], sp['fconv'], ss['fconv'])
```

```python
import functools
import math

import jax
import jax.numpy as jnp
from jax import lax
from jax.experimental import pallas as pl
from jax.experimental.pallas import tpu as pltpu

F32 = jnp.float32
BF16 = jnp.bfloat16
HI = lax.Precision.HIGHEST

HEAD_DIM = 64
ATTN_GROUPS = ((128, 1), (512, 4), (2048, 16))
ATTN_HPG = 4
ATTN_WIDTH = 3 * ATTN_HPG * HEAD_DIM
GROUP_W = ATTN_HPG * HEAD_DIM
ATTN_SCALE = HEAD_DIM ** -0.5
ROPE_THETA = 10000.0
BAND = 128
RES_UNROLL = 4
MIXER_BATCH = 4
MLSTM_HEADS = 10
MLSTM_WIDTH = MLSTM_HEADS * HEAD_DIM
MLSTM_CONV = 4
MLSTM_CHUNK = 128
GLA_HEADS = 10
GLA_DK = 32
GLA_DV = 64
GLA_K_WIDTH = GLA_HEADS * GLA_DK
GLA_V_WIDTH = GLA_HEADS * GLA_DV
GLA_RANK = 16
GLA_TAU = 16.0
GLA_CHUNK = 128
GLA_SAFE_DECAY = 60.0
FFN_CONV = 3
FFN_SUB = 512
EPS = 1e-6
PAST_LEN = 8192
NEG = -1e30

LANE = 128
VMEM_LIMIT = 62 * 1024 * 1024

IN_SPLITS = (ATTN_WIDTH, ATTN_WIDTH, ATTN_WIDTH, 2 * MLSTM_WIDTH, MLSTM_WIDTH, MLSTM_HEADS, MLSTM_HEADS,
             MLSTM_WIDTH, GLA_K_WIDTH, GLA_K_WIDTH, GLA_V_WIDTH, GLA_V_WIDTH, GLA_RANK)

C_AQ, C_AK, C_AV = 0, 768, 1536
C_MI, C_MF = 2304, 2432
C_MQK = 2560
C_MV, C_MO = 3840, 4480
C_GQK, C_GV, C_GG = 5120, 5760, 6400
C_GA = 7040
N_PACK = 7168
ROPE_COLS = 2 * ATTN_WIDTH


def _cparams(sem):
    return pltpu.CompilerParams(dimension_semantics=sem, vmem_limit_bytes=VMEM_LIMIT)


def _sigmoid(x):
    return 1.0 / (1.0 + jnp.exp(-x))


def _log_sigmoid(x):
    return jnp.minimum(x, 0.0) - jnp.log(1.0 + jnp.exp(-jnp.abs(x)))


def _dot(a, b):
    return jnp.dot(a, b, preferred_element_type=F32)


def _dot_nt(a, b):
    return lax.dot_general(a, b, (((1,), (1,)), ((), ())), preferred_element_type=F32)


def _split3(x):
    hi = x.astype(BF16)
    r1 = x - hi.astype(F32)
    mid = r1.astype(BF16)
    lo = (r1 - mid.astype(F32)).astype(BF16)
    return hi, mid, lo


def _dot_sel(sel, x, parts=3):
    sb = sel.astype(BF16)
    return sum(_dot(sb, p) for p in _split3(x)[:parts])


def _dot_sel_rhs(x, sel, parts=2):
    sb = sel.astype(BF16)
    return sum(_dot(p, sb) for p in _split3(x)[:parts])


def _dot_sel_nt(x, sel):
    sb = sel.astype(BF16)
    return sum(_dot_nt(p, sb) for p in _split3(x))


def _ada_kernel(c_ref, w_ref, b_ref, o_ref):
    c = c_ref[...]
    s = (c * _sigmoid(c)).astype(BF16)
    o_ref[0] = _dot(s, w_ref[0].astype(BF16)) + b_ref[0]


def _ada_call(c_all, w_ada, b_ada):
    depth, d, n = w_ada.shape
    rows = c_all.shape[0]
    tn = 1024
    return pl.pallas_call(
        _ada_kernel,
        grid=(depth, n // tn),
        in_specs=[pl.BlockSpec((rows, d), lambda l, j: (0, 0)),
                  pl.BlockSpec((1, d, tn), lambda l, j: (l, 0, j)),
                  pl.BlockSpec((1, 1, tn), lambda l, j: (l, 0, j))],
        out_specs=pl.BlockSpec((1, rows, tn), lambda l, j: (l, 0, j)),
        out_shape=jax.ShapeDtypeStruct((depth, rows, n), F32),
        compiler_params=_cparams(("arbitrary", "arbitrary")),
        name="ada",
    )(c_all, w_ada, b_ada.reshape(depth, 1, n))


def _normmod_kernel(x_ref, g_ref, sc_ref, sh_ref, o_ref):
    x = x_ref[0]
    y = x * lax.rsqrt(jnp.mean(x * x, axis=-1, keepdims=True) + EPS) * g_ref[...]
    o_ref[0] = (y * (1.0 + sc_ref[0]) + sh_ref[0]).astype(o_ref.dtype)


def _normmod_call(x, g, sc, sh, out_dtype, tm):
    G, R, D = x.shape
    rr = sc.shape[1]
    mod_spec = (pl.BlockSpec((1, 1, D), lambda b, i: (b, 0, 0)) if rr == 1
                else pl.BlockSpec((1, tm, D), lambda b, i: (b, i, 0)))
    return pl.pallas_call(
        _normmod_kernel,
        grid=(G, R // tm),
        in_specs=[pl.BlockSpec((1, tm, D), lambda b, i: (b, i, 0)),
                  pl.BlockSpec((1, D), lambda b, i: (0, 0)),
                  mod_spec, mod_spec],
        out_specs=pl.BlockSpec((1, tm, D), lambda b, i: (b, i, 0)),
        out_shape=jax.ShapeDtypeStruct((G, R, D), out_dtype),
        compiler_params=_cparams(("arbitrary", "arbitrary")),
        name="normmod",
    )(x, g.reshape(1, D), sc, sh)


def _rope_chunk(x, cos, sin_a, sin_b):
    return x * cos + pltpu.roll(x, 96, 1) * sin_a + pltpu.roll(x, 32, 1) * sin_b


def _inproj_kernel(h_ref, w_ref, b_ref, cos_ref, sa_ref, sb_ref, o_ref, *, tn):
    j = pl.program_id(0)
    o_ref[...] = _dot(h_ref[...], w_ref[...]) + b_ref[...]
    n_chunks = tn // LANE
    rope_tiles = -(-ROPE_COLS // tn)
    for jt in range(rope_tiles):
        chunks = min(n_chunks, (ROPE_COLS - jt * tn) // LANE)

        @pl.when(j == jt)
        def _():
            cos, sa, sb = cos_ref[...], sa_ref[...], sb_ref[...]
            for c in range(chunks):
                sl = slice(c * LANE, (c + 1) * LANE)
                o_ref[:, sl] = _rope_chunk(o_ref[:, sl], cos, sa, sb)


def _inproj_call(h, w, layer, b, cos, sin_a, sin_b, tm, tn):
    M, D = h.shape
    N = w.shape[2]
    tp = cos.shape[0] // tm
    tab = pl.BlockSpec((tm, LANE), lambda j, i: (i % tp, 0))
    return pl.pallas_call(
        functools.partial(_inproj_kernel, tn=tn),
        grid=(N // tn, M // tm),
        in_specs=[pl.BlockSpec((tm, D), lambda j, i: (i, 0)),
                  pl.BlockSpec((None, D, tn), lambda j, i: (layer, 0, j)),
                  pl.BlockSpec((1, tn), lambda j, i: (0, j)),
                  tab, tab, tab],
        out_specs=pl.BlockSpec((tm, tn), lambda j, i: (i, j)),
        out_shape=jax.ShapeDtypeStruct((M, N), F32),
        compiler_params=_cparams(("arbitrary", "arbitrary")),
        name="inproj",
    )(h, w, b, cos, sin_a, sin_b)


def _attn_prompt_kernel(q_ref, kp_ref, kc_ref, vp_ref, vc_ref, o_ref, l_ref, obuf, lbuf, *, dil, nq):
    i = pl.program_id(2)
    row = lax.broadcasted_iota(jnp.int32, (BAND, 2 * BAND), 0)
    col = lax.broadcasted_iota(jnp.int32, (BAND, 2 * BAND), 1)
    band = (col >= row) & (col <= row + BAND)
    band_first = band & ((col >= BAND) | (i > 0))
    nr = nq * BAND

    def residues(rs):
        rows_of, items = [], []
        for u, r in enumerate(rs):
            if dil == 1:
                rows, prev_rows = slice(None), slice(nr - BAND, nr)
            else:
                rows, prev_rows = pl.ds(r, nr, stride=dil), pl.ds((nr - BAND) * dil + r, BAND, stride=dil)
            rows_of.append(rows)
            q_all = q_ref[0, rows, :].astype(BF16)
            k_all = jnp.concatenate([kp_ref[0, prev_rows, :], kc_ref[0, rows, :]], axis=0).astype(BF16)
            v_all = jnp.concatenate([vp_ref[0, prev_rows, :], vc_ref[0, rows, :]], axis=0).astype(BF16)
            for sb in range(nq):
                qs = slice(sb * BAND, (sb + 1) * BAND)
                ks = slice(sb * BAND, (sb + 2) * BAND)
                for h in range(LANE // HEAD_DIM):
                    hs = slice(h * HEAD_DIM, (h + 1) * HEAD_DIM)
                    items.append((u, qs, hs, q_all[qs, hs], k_all[ks, hs], v_all[ks, hs],
                                  band_first if sb == 0 else band))
        ss = [jnp.where(it[6], _dot_nt(it[3], it[4]) * ATTN_SCALE, NEG) for it in items]
        mxs = [jnp.max(s, axis=-1, keepdims=True) for s in ss]
        es = [jnp.exp(s - mx) for s, mx in zip(ss, mxs)]
        dens = [jnp.sum(e, axis=-1, keepdims=True) for e in es]
        pvs = [_dot(e.astype(BF16), it[5]) for e, it in zip(es, items)]
        outs = [pv / den for pv, den in zip(pvs, dens)]
        lses = [jnp.broadcast_to(mx + jnp.log(den), (BAND, HEAD_DIM)) for mx, den in zip(mxs, dens)]
        for it, o, l in zip(items, outs, lses):
            obuf[it[0], it[1], it[2]] = o
            lbuf[it[0], it[1], it[2]] = l
        for u, rows in enumerate(rows_of):
            o_ref[0, rows, :] = obuf[u]
            l_ref[0, rows, :] = lbuf[u]

    if dil == 1:
        residues([0])
    elif dil <= RES_UNROLL:
        residues(list(range(dil)))
    else:
        def body(j, carry):
            residues([j * RES_UNROLL + u for u in range(RES_UNROLL)])
            return carry

        lax.fori_loop(0, dil // RES_UNROLL, body, 0)


def _attn_prompt_call(proj, g, dil):
    B, S, N = proj.shape
    nq = max(1, math.gcd(S, 512) // (BAND * dil))
    tb = nq * BAND * dil
    nb = S // tb
    hp = GROUP_W // LANE
    cq, ck, cv = (C_AQ + g * GROUP_W) // LANE, (C_AK + g * GROUP_W) // LANE, (C_AV + g * GROUP_W) // LANE

    def spec(cblk, prev):
        if prev:
            return pl.BlockSpec((1, tb, LANE), lambda b, p, i: (b, jnp.maximum(i - 1, 0), cblk + p))
        return pl.BlockSpec((1, tb, LANE), lambda b, p, i: (b, i, cblk + p))

    ospec = pl.BlockSpec((1, tb, LANE), lambda b, p, i: (b, i, p))
    o, lse = pl.pallas_call(
        functools.partial(_attn_prompt_kernel, dil=dil, nq=nq),
        grid=(B, hp, nb),
        in_specs=[spec(cq, False), spec(ck, True), spec(ck, False), spec(cv, True), spec(cv, False)],
        out_specs=[ospec, ospec],
        out_shape=[jax.ShapeDtypeStruct((B, S, GROUP_W), F32)] * 2,
        scratch_shapes=[pltpu.VMEM((min(dil, RES_UNROLL), nq * BAND, LANE), F32)] * 2,
        compiler_params=_cparams(("arbitrary", "arbitrary", "arbitrary")),
        name=f"attn_prompt_g{g}",
    )(proj, proj, proj, proj, proj)
    return o, lse


def _win_extract_kernel(x_ref, o_ref):
    o_ref[0, 0] = x_ref[0].T


def _win_extract_call(proj, g, keep):
    B, S, N = proj.shape
    chunk = min(keep, 512)
    first = (S - keep) // chunk
    cblk = C_AK // GROUP_W + g
    step = (C_AV - C_AK) // GROUP_W
    out = pl.pallas_call(
        _win_extract_kernel,
        grid=(B, 2, keep // chunk),
        in_specs=[pl.BlockSpec((1, chunk, GROUP_W), lambda b, kv, j: (b, first + j, cblk + step * kv))],
        out_specs=pl.BlockSpec((1, 1, GROUP_W, chunk), lambda b, kv, j: (b, kv, 0, j)),
        out_shape=jax.ShapeDtypeStruct((B, 2, GROUP_W, keep), F32),
        compiler_params=_cparams(("arbitrary", "arbitrary", "arbitrary")),
        name=f"win_extract_g{g}",
    )(proj)
    return out.reshape(B, 2, ATTN_HPG, HEAD_DIM, keep).transpose(0, 4, 1, 2, 3)


QROWS = 16


def _attn_sample_kernel(qkv_ref, c0_ref, c1_ref, c2_ref, *rest, T, n_prev):
    y_ref, n0_ref, n1_ref, n2_ref, q16, npad, ysc = rest[n_prev:]
    q16[...] = jnp.zeros_like(q16)
    npad[...] = jnp.zeros_like(npad)
    q16[0:T, :] = qkv_ref[0, :, 0:ATTN_WIDTH]
    lane = lax.broadcasted_iota(jnp.int32, (HEAD_DIM, LANE), 1)
    items = []
    for g, (cref, nref) in enumerate(((c0_ref, n0_ref), (c1_ref, n1_ref), (c2_ref, n2_ref))):
        Lb = cref.shape[-1]
        dil = ATTN_GROUPS[g][1]
        npad[g, 0:T, 0:GROUP_W] = qkv_ref[0, :, C_AK + g * GROUP_W:C_AK + (g + 1) * GROUP_W]
        npad[g, 0:T, GROUP_W:2 * GROUP_W] = qkv_ref[0, :, C_AV + g * GROUP_W:C_AV + (g + 1) * GROUP_W]
        new_t = npad[g].T
        ncols = Lb + LANE
        t = lax.broadcasted_iota(jnp.int32, (QROWS, ncols), 0)
        c = lax.broadcasted_iota(jnp.int32, (QROWS, ncols), 1)
        if dil == 1:
            valid = ((c >= t) & (c < Lb)) | ((c >= Lb) & (c - Lb <= t) & (c - Lb < T))
        else:
            valid = ((c < Lb) & ((c & (dil - 1)) == t)) | (c == Lb + t)
        for h in range(ATTN_HPG):
            rk = slice(h * HEAD_DIM, (h + 1) * HEAD_DIM)
            rv = slice(GROUP_W + h * HEAD_DIM, GROUP_W + (h + 1) * HEAD_DIM)
            kc, vc = cref[0, 0, 0, h], cref[0, 0, 1, h]
            kt = jnp.concatenate([kc, new_t[rk, :]], axis=1).astype(BF16)
            vt = jnp.concatenate([vc, new_t[rv, :]], axis=1).astype(BF16)
            qh = q16[:, g * GROUP_W + h * HEAD_DIM:g * GROUP_W + (h + 1) * HEAD_DIM].astype(BF16)
            items.append((qh, kt, vt, valid))
            for kv, blk, rn in ((0, kc, rk), (1, vc, rv)):
                sh = pltpu.roll(blk, Lb - T, 1)
                newc = pltpu.roll(new_t[rn, :], LANE - T, 1)
                if Lb > LANE:
                    nref[0, 0, kv, h, :, 0:Lb - LANE] = sh[:, 0:Lb - LANE]
                nref[0, 0, kv, h, :, Lb - LANE:Lb] = jnp.where(lane >= LANE - T, newc, sh[:, Lb - LANE:Lb])
    ss = [jnp.where(valid, _dot(qh, kt) * ATTN_SCALE, NEG) for qh, kt, vt, valid in items]
    mxs = [jnp.max(s, axis=-1, keepdims=True) for s in ss]
    es = [jnp.exp(s - mx) for s, mx in zip(ss, mxs)]
    dens = [jnp.sum(e, axis=-1, keepdims=True) for e in es]
    pvs = [_dot_nt(e.astype(BF16), it[2]) for e, it in zip(es, items)]
    outs = [pv / den for pv, den in zip(pvs, dens)]
    lses = [mx + jnp.log(den) for mx, den in zip(mxs, dens)]
    for h in range(ATTN_HPG):
        l0, l1, l2 = lses[h], lses[ATTN_HPG + h], lses[2 * ATTN_HPG + h]
        mx = jnp.maximum(jnp.maximum(l0, l1), l2)
        es = [jnp.exp(l0 - mx), jnp.exp(l1 - mx), jnp.exp(l2 - mx)]
        inv = 1.0 / (es[0] + es[1] + es[2])
        for g in range(3):
            c0 = g * GROUP_W + h * HEAD_DIM
            ysc[:, c0:c0 + HEAD_DIM] = outs[g * ATTN_HPG + h] * (es[g] * inv)
    y_ref[0] = ysc[0:T, :].astype(y_ref.dtype)


def _attn_sample_call(qkv, caches_t, layer, prev):
    B, T, _ = qkv.shape
    depth = caches_t[0].shape[0]
    for g, (win, dil) in enumerate(ATTN_GROUPS):
        assert caches_t[g].shape[-1] == win and win // dil == BAND and (g == 0 or T <= dil), \
            "sample attention assumes full window caches"
    cspecs = [pl.BlockSpec((1, 1) + c.shape[2:], lambda b: (layer, b, 0, 0, 0, 0)) for c in caches_t]
    n_prev = 0 if prev is None else 3
    prev_args = [] if prev is None else list(prev)
    prev_specs = [pl.BlockSpec(memory_space=pl.ANY)] * n_prev
    aliases = {} if prev is None else {4 + g: 1 + g for g in range(3)}
    outs = pl.pallas_call(
        functools.partial(_attn_sample_kernel, T=T, n_prev=n_prev),
        grid=(B,),
        in_specs=[pl.BlockSpec((1, T, 3 * ATTN_WIDTH), lambda b: (b, 0, 0))] + cspecs + prev_specs,
        out_specs=[pl.BlockSpec((1, T, ATTN_WIDTH), lambda b: (b, 0, 0))] + cspecs,
        out_shape=[jax.ShapeDtypeStruct((B, T, ATTN_WIDTH), BF16)]
        + [jax.ShapeDtypeStruct(c.shape, F32) for c in caches_t],
        scratch_shapes=[pltpu.VMEM((QROWS, ATTN_WIDTH), F32),
                        pltpu.VMEM((3, LANE, 2 * GROUP_W), F32),
                        pltpu.VMEM((QROWS, ATTN_WIDTH), F32)],
        input_output_aliases=aliases,
        compiler_params=_cparams(("arbitrary",)),
        name="attn_sample",
    )(qkv, *caches_t, *prev_args)
    return outs[0], outs[1:]


def _mlstm_kernel(mqk_ref, mv_ref, mi_ref, mf_ref, mo_ref, cst_ref, wc_ref, bc_ref, c0_ref, n0_ref, m0_ref,
                  g_ref, tri_ref, y_ref, c_ref, n_ref, m_ref, cso_ref,
                  xbuf, vbuf, cs, ns, ms, tpad, kwp, *, T, Lp, nb):
    c = pl.program_id(1)
    last = pl.num_programs(1) - 1
    W = MLSTM_WIDTH
    bs = range(nb)
    heads = range(MLSTM_HEADS)
    hsl = [slice(h * HEAD_DIM, (h + 1) * HEAD_DIM) for h in heads]
    pairs = [(b, h) for b in bs for h in heads]

    @pl.when(c == 0)
    def _():
        xbuf[...] = jnp.zeros_like(xbuf)
        vbuf[...] = jnp.zeros_like(vbuf)
        tpad[...] = jnp.zeros_like(tpad)
        kwp[...] = jnp.zeros_like(kwp)
        xbuf[:, 5:8, :] = cst_ref[...]
        cs[...] = c0_ref[...]
        ns[...] = n0_ref[...]
        ms[...] = m0_ref[...]

    w = wc_ref[...]
    tri = tri_ref[...]
    causal = tri > 0.5
    rowid = lax.broadcasted_iota(jnp.int32, (Lp, LANE), 0)
    real = rowid < T
    lane_row = lax.broadcasted_iota(jnp.int32, (1, LANE), 1)

    for b in bs:
        xbuf[b, 8:8 + T, :] = mqk_ref[b]
        vbuf[b, 0:T, :] = mv_ref[b]
    ys = [bc_ref[...] + xbuf[b, 8:8 + Lp, :] * w[3:4] + xbuf[b, 7:7 + Lp, :] * w[2:3]
          + xbuf[b, 6:6 + Lp, :] * w[1:2] + xbuf[b, 5:5 + Lp, :] * w[0:1] for b in bs]
    tails = [xbuf[b, 8 + T - 3:8 + T, :] for b in bs]
    for b in bs:
        xbuf[b, 5:8, :] = tails[b]

    @pl.when(c == last)
    def _():
        for b in bs:
            cso_ref[b] = tails[b]

    qks = [y * _sigmoid(y) for y in ys]
    igs, lfs = [], []
    for b in bs:
        tpad[b, 0:T, :] = mi_ref[b]
        igs.append(jnp.where(real, tpad[b, 0:Lp, :], NEG))
        tpad[b, 0:T, :] = mf_ref[b]
        lfs.append(jnp.where(real, _log_sigmoid(tpad[b, 0:Lp, :]), 0.0))
    Fs = [_dot_sel(tri, lf) for lf in lfs]
    m_rows = [ms[b] for b in bs]
    inter_all = [Fs[b] + m_rows[b] for b in bs]
    FTs, ITs = [], []
    for b in bs:
        tpad[b, 0:Lp, :] = Fs[b]
        FTs.append(tpad[b].T)
        tpad[b, 0:Lp, :] = igs[b]
        ITs.append(tpad[b].T)
    v_alls = [vbuf[b, 0:Lp, :] for b in bs]
    n_alls = [ns[b] for b in bs]

    qs = [qks[b][:, hsl[h]] for b, h in pairs]
    ks = [qks[b][:, W + h * HEAD_DIM:W + (h + 1) * HEAD_DIM] * (HEAD_DIM ** -0.5) for b, h in pairs]
    qbs = [q.astype(BF16) for q in qs]
    vbs = [v_alls[b][:, hsl[h]].astype(BF16) for b, h in pairs]
    sqk = [_dot_nt(qb, k.astype(BF16)) for qb, k in zip(qbs, ks)]
    qcs = [_dot(qbs[i], cs[b, h].astype(BF16)) for i, (b, h) in enumerate(pairs)]
    nrows = [n_alls[b][h:h + 1, :] for b, h in pairs]
    qn = [_dot_nt(qbs[i], jnp.broadcast_to(nrows[i], (16, HEAD_DIM)).astype(BF16))[:, 0:1]
          for i in range(len(pairs))]

    fcols = [Fs[b][:, h:h + 1] for b, h in pairs]
    icols = [igs[b][:, h:h + 1] for b, h in pairs]
    inters = [inter_all[b][:, h:h + 1] for b, h in pairs]
    logws = [jnp.where(causal, fcols[i] - FTs[b][h:h + 1, 0:Lp] + ITs[b][h:h + 1, 0:Lp], NEG)
             for i, (b, h) in enumerate(pairs)]
    rmax = [jnp.max(lw, axis=-1, keepdims=True) for lw in logws]
    mts = [jnp.maximum(r, it) for r, it in zip(rmax, inters)]
    As = [jnp.exp(lw - mt) * s for lw, mt, s in zip(logws, mts, sqk)]
    gqs = [jnp.exp(it - mt) for it, mt in zip(inters, mts)]
    asum = [jnp.sum(A, axis=-1, keepdims=True) for A in As]
    dens = [a + g * q for a, g, q in zip(asum, gqs, qn)]
    a_bf = [A.astype(BF16) for A in As]
    avs = [_dot(a, v) for a, v in zip(a_bf, vbs)]
    mLs = [mt[Lp - 1:Lp, :] for mt in mts]
    flast = [Fs[b][Lp - 1:Lp, h:h + 1] for b, h in pairs]
    wLs = [jnp.exp(fl - fc + ic - mL) for fl, fc, ic, mL in zip(flast, fcols, icols, mLs)]
    gls = [jnp.exp(flast[i] + m_rows[b][0:1, h:h + 1] - mLs[i]) for i, (b, h) in enumerate(pairs)]
    kws = [k * wl for k, wl in zip(ks, wLs)]
    for i, (b, h) in enumerate(pairs):
        kwp[b, 0:Lp, hsl[h]] = kws[i]
    ksum = [jnp.sum(kw, axis=0, keepdims=True) for kw in kws]
    for i, (b, h) in enumerate(pairs):
        ns[b, h:h + 1, :] = gls[i] * nrows[i] + ksum[i]
    for b in bs:
        m_new = m_rows[b]
        for h in heads:
            m_new = jnp.where(lane_row == h, mLs[b * MLSTM_HEADS + h], m_new)
        ms[b] = m_new

    kwts = [kwp[b].T for b in bs]
    upds = [_dot(kwts[b][hsl[h], 0:Lp].astype(BF16), vbs[i]) for i, (b, h) in enumerate(pairs)]

    nums = [av + g * qc for av, g, qc in zip(avs, gqs, qcs)]
    hhs = [nu / jnp.maximum(jnp.abs(de), jnp.exp(-mt)) for nu, de, mt in zip(nums, dens, mts)]
    msq = [jnp.mean(hh * hh, axis=-1, keepdims=True) for hh in hhs]
    hns = [hhs[i] * lax.rsqrt(msq[i] + EPS) * g_ref[0:1, hsl[h]] for i, (b, h) in enumerate(pairs)]
    ogs = [_sigmoid(mo_ref[b, :, hsl[h]]) for b, h in pairs]
    for i, (b, h) in enumerate(pairs):
        yo = hns[i] * ogs[i] if Lp == T else hns[i][0:T] * ogs[i]
        y_ref[b, :, hsl[h]] = yo.astype(y_ref.dtype)
    for i, (b, h) in enumerate(pairs):
        cs[b, h] = gls[i] * cs[b, h] + upds[i]

    @pl.when(c == last)
    def _():
        c_ref[...] = cs[...]
        n_ref[...] = ns[...]
        m_ref[...] = ms[...]


def _mlstm_call(proj, conv_state, w_conv, b_conv, c0, n0, m0, g_mlstm, chunk):
    B, S, N = proj.shape
    T = chunk
    nc = S // T
    Lp = max(16, T)
    H = MLSTM_HEADS
    nb = math.gcd(B, MIXER_BATCH)
    m0p = jnp.pad(m0, ((0, 0), (0, LANE - H))).reshape(B, 1, LANE)
    tri = jnp.tril(jnp.ones((Lp, Lp), F32))

    def col(width, off):
        blk = off // width
        return pl.BlockSpec((nb, T, width), lambda b, c: (b, c, blk))

    def const(shape):
        nd = len(shape)
        return pl.BlockSpec(shape, lambda b, c: (0,) * nd)

    def per_b(shape):
        nd = len(shape)
        return pl.BlockSpec((nb,) + shape, lambda b, c: (b,) + (0,) * nd)

    outs = pl.pallas_call(
        functools.partial(_mlstm_kernel, T=T, Lp=Lp, nb=nb),
        grid=(B // nb, nc),
        in_specs=[col(2 * MLSTM_WIDTH, C_MQK), col(MLSTM_WIDTH, C_MV), col(LANE, C_MI), col(LANE, C_MF),
                  col(MLSTM_WIDTH, C_MO), per_b((MLSTM_CONV - 1, 2 * MLSTM_WIDTH)),
                  const((MLSTM_CONV, 2 * MLSTM_WIDTH)), const((1, 2 * MLSTM_WIDTH)),
                  per_b((H, HEAD_DIM, HEAD_DIM)), per_b((H, HEAD_DIM)), per_b((1, LANE)),
                  const((1, MLSTM_WIDTH)), const((Lp, Lp))],
        out_specs=[pl.BlockSpec((nb, T, MLSTM_WIDTH), lambda b, c: (b, c, 0)),
                   per_b((H, HEAD_DIM, HEAD_DIM)), per_b((H, HEAD_DIM)), per_b((1, LANE)),
                   per_b((MLSTM_CONV - 1, 2 * MLSTM_WIDTH))],
        out_shape=[jax.ShapeDtypeStruct((B, S, MLSTM_WIDTH), BF16),
                   jax.ShapeDtypeStruct((B, H, HEAD_DIM, HEAD_DIM), F32),
                   jax.ShapeDtypeStruct((B, H, HEAD_DIM), F32),
                   jax.ShapeDtypeStruct((B, 1, LANE), F32),
                   jax.ShapeDtypeStruct((B, MLSTM_CONV - 1, 2 * MLSTM_WIDTH), F32)],
        scratch_shapes=[pltpu.VMEM((nb, 8 + Lp, 2 * MLSTM_WIDTH), F32),
                        pltpu.VMEM((nb, Lp, MLSTM_WIDTH), F32),
                        pltpu.VMEM((nb, H, HEAD_DIM, HEAD_DIM), F32),
                        pltpu.VMEM((nb, H, HEAD_DIM), F32),
                        pltpu.VMEM((nb, 1, LANE), F32),
                        pltpu.VMEM((nb, LANE, LANE), F32),
                        pltpu.VMEM((nb, LANE, MLSTM_WIDTH), F32)],
        compiler_params=_cparams(("arbitrary", "arbitrary")),
        name="mlstm",
    )(proj, proj, proj, proj, proj, conv_state, w_conv, b_conv.reshape(1, -1), c0, n0, m0p,
      g_mlstm.reshape(1, -1), tri)
    y, C, n, m, cso = outs
    return y, C, n, m[:, 0, :H], cso


def _gla_kernel(gqk_ref, gv_ref, gg_ref, ga_ref, wa_ref, wat_ref, s0_ref, g_ref, tri_ref, bd_ref, ee_ref,
                y_ref, s_ref, qkp, vp, gap, sbd, osc, *, T, Lp, nb):
    c = pl.program_id(1)
    last = pl.num_programs(1) - 1
    KW, VW = GLA_K_WIDTH, GLA_V_WIDTH
    bs = range(nb)
    heads = range(GLA_HEADS)

    @pl.when(c == 0)
    def _():
        qkp[...] = jnp.zeros_like(qkp)
        vp[...] = jnp.zeros_like(vp)
        gap[...] = jnp.zeros_like(gap)
        sbd[...] = jnp.zeros_like(sbd)
        osc[...] = jnp.zeros_like(osc)
        for b in bs:
            for h in heads:
                sbd[b, h * GLA_DK:(h + 1) * GLA_DK, h * GLA_DV:(h + 1) * GLA_DV] = s0_ref[b, h]

    lane = lax.broadcasted_iota(jnp.int32, (T, LANE), 1)
    for b in bs:
        qkp[b, 0:T, :] = gqk_ref[b]
        vp[b, 0:T, :] = gv_ref[b]
        gap[b, 0:T, :] = jnp.where(lane == GLA_RANK, 1.0, ga_ref[b])

    tri = tri_ref[...]
    rowid = lax.broadcasted_iota(jnp.int32, (Lp, KW), 0)
    colid = lax.broadcasted_iota(jnp.int32, (KW, LANE), 1)
    wa_hi, wa_lo, _ = _split3(wa_ref[...])
    wat_hi, wat_lo, _ = _split3(wat_ref[...])
    ga_bs = [gap[b, 0:Lp, :].astype(BF16) for b in bs]
    gat_bs = [gap[b].T.astype(BF16) for b in bs]
    las = [_dot(g, wa_hi) + _dot(g, wa_lo) for g in ga_bs]
    las = [jnp.where(rowid < T, _log_sigmoid(la) / GLA_TAU, 0.0) for la in las]
    bcs = [_dot_sel(tri[0:Lp, 0:Lp], la) for la in las]
    lats = [_dot(wat_hi, g) + _dot(wat_lo, g) for g in gat_bs]
    lats = [jnp.where(colid < T, _log_sigmoid(lat) / GLA_TAU, 0.0) for lat in lats]
    bcts = [_dot_sel_nt(lat, tri) for lat in lats]
    blcols = [bct[:, LANE - 1:LANE] for bct in bcts]
    blrows = [bc[Lp - 1:Lp, :] for bc in bcs]
    kts = [qkp[b].T[KW:2 * KW, :] for b in bs]
    bd = bd_ref[...]
    min_decay = jnp.min(blrows[0])
    for b in bs[1:]:
        min_decay = jnp.minimum(min_decay, jnp.min(blrows[b]))
    safe = min_decay >= -GLA_SAFE_DECAY

    @pl.when(safe)
    def _():
        causal = tri[0:Lp, 0:Lp] > 0.5
        qbs = [(qkp[b, 0:Lp, 0:KW] * (GLA_DK ** -0.5) * jnp.exp(bcs[b])).astype(BF16) for b in bs]
        kbs = [(qkp[b, 0:Lp, KW:2 * KW] * jnp.exp(-bcs[b])).astype(BF16) for b in bs]
        vbs = [vp[b, 0:Lp, :].astype(BF16) for b in bs]
        o_inters = [_dot(qbs[b], sbd[b].astype(BF16)) for b in bs]
        pairs = [(b, h) for b in bs for h in heads]
        scores = [_dot_nt(qbs[b][:, h * GLA_DK:(h + 1) * GLA_DK], kbs[b][:, h * GLA_DK:(h + 1) * GLA_DK])
                  for b, h in pairs]
        a_bf = [jnp.where(causal, s, 0.0).astype(BF16) for s in scores]
        ovs = [_dot(a_bf[i], vbs[b][:, h * GLA_DV:(h + 1) * GLA_DV]) for i, (b, h) in enumerate(pairs)]
        klts = [(kts[b] * jnp.exp(blcols[b] - bcts[b])).astype(BF16) for b in bs]
        upds = [_dot(klts[b], vp[b].astype(BF16)) for b in bs]
        for b in bs:
            osc[b, 0:Lp, :] = o_inters[b] + jnp.concatenate(ovs[b * GLA_HEADS:(b + 1) * GLA_HEADS], axis=1)
            sbd[b] = bd * (jnp.exp(blcols[b]) * sbd[b] + upds[b])

    @pl.when(jnp.logical_not(safe))
    def _():
        srow = lax.broadcasted_iota(jnp.int32, (LANE, LANE), 0)
        for b in bs:
            def body(t, carry, b=b):
                sel = (srow == t).astype(F32)
                lac = jnp.dot(lats[b], sel, precision=HI, preferred_element_type=F32)
                kc = jnp.dot(kts[b], sel, precision=HI, preferred_element_type=F32)
                dec = jnp.concatenate([jnp.exp(lac)] * (VW // LANE), axis=1)
                kcw = jnp.concatenate([kc] * (VW // LANE), axis=1)
                vrow = vp[b, pl.ds(t, 1), :]
                snew = bd * (dec * sbd[b] + kcw * vrow)
                sbd[b] = snew
                qrow = jnp.broadcast_to(qkp[b, pl.ds(t, 1), 0:KW] * (GLA_DK ** -0.5), (8, KW))
                orow = jnp.dot(qrow, snew, precision=HI, preferred_element_type=F32)
                osc[b, pl.ds(t, 1), :] = orow[0:1]
                return carry

            lax.fori_loop(0, T, body, 0)

    os_ = [osc[b, 0:Lp, :] for b in bs]
    mss = [_dot_sel_rhs(o * o, ee_ref[...]) for o in os_]
    ogs = [o * lax.rsqrt(ms + EPS) * g_ref[...] for o, ms in zip(os_, mss)]
    for b in bs:
        gg = gg_ref[b]
        yo = (ogs[b] if Lp == T else ogs[b][0:T]) * (gg * _sigmoid(gg))
        y_ref[b] = yo.astype(y_ref.dtype)

    @pl.when(c == last)
    def _():
        for b in bs:
            for h in heads:
                s_ref[b, h] = sbd[b, h * GLA_DK:(h + 1) * GLA_DK, h * GLA_DV:(h + 1) * GLA_DV]


def _gla_call(proj, w_a2, b_a2, s0, g_gla, chunk):
    B, S, N = proj.shape
    T = chunk
    nc = S // T
    Lp = max(16, T)
    H, KW, VW = GLA_HEADS, GLA_K_WIDTH, GLA_V_WIDTH
    wa = jnp.zeros((LANE, KW), F32).at[:GLA_RANK].set(w_a2).at[GLA_RANK].set(b_a2)
    tri = jnp.tril(jnp.ones((LANE, LANE), F32))
    hk = jnp.arange(KW) // GLA_DK
    hv = jnp.arange(VW) // GLA_DV
    bd = (hk[:, None] == hv[None, :]).astype(F32)
    ee = (hv[:, None] == hv[None, :]).astype(F32) / GLA_DV

    nb = math.gcd(B, MIXER_BATCH)

    def col(width, off):
        blk = off // width
        return pl.BlockSpec((nb, T, width), lambda b, c: (b, c, blk))

    def const(shape):
        nd = len(shape)
        return pl.BlockSpec(shape, lambda b, c: (0,) * nd)

    y, s = pl.pallas_call(
        functools.partial(_gla_kernel, T=T, Lp=Lp, nb=nb),
        grid=(B // nb, nc),
        in_specs=[col(2 * KW, C_GQK), col(VW, C_GV), col(VW, C_GG), col(LANE, C_GA),
                  const((LANE, KW)), const((KW, LANE)),
                  pl.BlockSpec((nb, H, GLA_DK, GLA_DV), lambda b, c: (b, 0, 0, 0)),
                  const((1, VW)), const((LANE, LANE)), const((KW, VW)), const((VW, VW))],
        out_specs=[pl.BlockSpec((nb, T, VW), lambda b, c: (b, c, 0)),
                   pl.BlockSpec((nb, H, GLA_DK, GLA_DV), lambda b, c: (b, 0, 0, 0))],
        out_shape=[jax.ShapeDtypeStruct((B, S, VW), BF16),
                   jax.ShapeDtypeStruct((B, H, GLA_DK, GLA_DV), F32)],
        scratch_shapes=[pltpu.VMEM((nb, LANE, 2 * KW), F32),
                        pltpu.VMEM((nb, LANE, VW), F32),
                        pltpu.VMEM((nb, LANE, LANE), F32),
                        pltpu.VMEM((nb, KW, VW), F32),
                        pltpu.VMEM((nb, LANE, VW), F32)],
        compiler_params=_cparams(("arbitrary", "arbitrary")),
        name="gla",
    )(proj, proj, proj, proj, wa, wa.T, s0, g_gla.reshape(1, -1), tri, bd, ee)
    return y, s


def _outproj_kernel(*refs, n_attn):
    attn_refs = refs[:n_attn]
    ym_ref, yg_ref, w_ref, x_ref, gt_ref, g2_ref, sc_ref, sh_ref, xo_ref, h2_ref = refs[n_attn:]
    a, b = ATTN_WIDTH, ATTN_WIDTH + MLSTM_WIDTH
    if n_attn == 1:
        acc = _dot(attn_refs[0][0], w_ref[0:a, :])
    else:
        l0, l1, l2 = attn_refs[3][0], attn_refs[4][0], attn_refs[5][0]
        mx = jnp.maximum(jnp.maximum(l0, l1), l2)
        es = [jnp.exp(l0 - mx), jnp.exp(l1 - mx), jnp.exp(l2 - mx)]
        inv = 1.0 / (es[0] + es[1] + es[2])
        acc = None
        for g in range(3):
            yg_attn = (attn_refs[g][0] * (es[g] * inv)).astype(BF16)
            part = _dot(yg_attn, w_ref[g * GROUP_W:(g + 1) * GROUP_W, :])
            acc = part if acc is None else part + acc
    acc = acc + _dot(ym_ref[0], w_ref[a:b, :])
    acc = acc + _dot(yg_ref[0], w_ref[b:, :])
    x = x_ref[0] + gt_ref[0] * acc
    xo_ref[0] = x
    y = x * lax.rsqrt(jnp.mean(x * x, axis=-1, keepdims=True) + EPS) * g2_ref[...]
    h2_ref[0] = (y * (1.0 + sc_ref[0]) + sh_ref[0]).astype(h2_ref.dtype)


def _outproj_call(ya, ym, yg, w_out, layer, x, gt, g2, sc, sh, tm):
    G, R, D = x.shape
    rr = gt.shape[1]
    mod_spec = (pl.BlockSpec((1, 1, D), lambda b, i: (b, 0, 0)) if rr == 1
                else pl.BlockSpec((1, tm, D), lambda b, i: (b, i, 0)))

    def act(width):
        return pl.BlockSpec((1, tm, width), lambda b, i: (b, i, 0))

    attn = list(ya) if isinstance(ya, (list, tuple)) else [ya]
    return pl.pallas_call(
        functools.partial(_outproj_kernel, n_attn=len(attn)),
        grid=(G, R // tm),
        in_specs=[act(a_.shape[-1]) for a_ in attn] + [act(MLSTM_WIDTH), act(GLA_V_WIDTH),
                  pl.BlockSpec((None,) + w_out.shape[1:], lambda b, i: (layer, 0, 0)),
                  act(D), mod_spec, pl.BlockSpec((1, D), lambda b, i: (0, 0)), mod_spec, mod_spec],
        out_specs=[act(D), act(D)],
        out_shape=[jax.ShapeDtypeStruct((G, R, D), F32), jax.ShapeDtypeStruct((G, R, D), BF16)],
        compiler_params=_cparams(("arbitrary", "arbitrary")),
        name="outproj",
    )(*attn, ym, yg, w_out, x, gt, g2.reshape(1, D), sc, sh)


def _ffn_kernel(h_ref, wg_in, wu_in, wc_ref, bc_ref, wo_in, x_ref, gt_ref, init_ref,
                xo_ref, st_ref, *rest, tm, u, R, tiles_per_seq, emit):
    m = pl.program_id(1)
    f = pl.program_id(2)
    nf = pl.num_programs(2)
    h = h_ref[0]
    if emit:
        wg_ref, wu_ref, wo_ref, ubuf, cbuf = rest
        wg_ref[...] = wg_in[...].astype(BF16)
        wu_ref[...] = wu_in[...].astype(BF16)
        wo_ref[...] = wo_in[...].astype(BF16)
    else:
        ubuf, cbuf = rest
        wg_ref, wu_ref, wo_ref = wg_in, wu_in, wo_in

    @pl.when(m % tiles_per_seq == 0)
    def _():
        ubuf[0:R, :] = init_ref[0]

    @pl.when(m % tiles_per_seq != 0)
    def _():
        ubuf[0:R, :] = cbuf[f]

    @pl.when(f == 0)
    def _():
        xo_ref[0] = jnp.zeros_like(xo_ref[0])

    tf = wg_ref.shape[1]
    chunks = [slice(c0, min(c0 + FFN_SUB, tf)) for c0 in range(0, tf, FFN_SUB)]

    def up_matmuls(cs_):
        return _dot(h, wg_ref[:, cs_]), _dot(h, wu_ref[:, cs_])

    pending = up_matmuls(chunks[0])
    acc = xo_ref[0]
    for ci, cs_ in enumerate(chunks):
        ug, uu = pending
        if ci + 1 < len(chunks):
            pending = up_matmuls(chunks[ci + 1])
        ubuf[R:R + tm, cs_] = ug
        w = wc_ref[:, cs_]
        gate = (bc_ref[:, cs_] + ug * w[2:3] + ubuf[R - u:R - u + tm, cs_] * w[1:2]
                + ubuf[R - 2 * u:R - 2 * u + tm, cs_] * w[0:1])
        act = (gate * _sigmoid(gate) * uu).astype(BF16)
        acc = _dot(act, wo_ref[cs_, :]) + acc
    xo_ref[0] = acc
    tail = ubuf[tm:tm + R, :]
    cbuf[f] = tail
    st_ref[0, 0] = tail

    @pl.when(f == nf - 1)
    def _():
        xo_ref[0] = x_ref[0] + gt_ref[0] * xo_ref[0]


def _ffn_call(h2, weights, w_conv, b_conv, x, gt, init, tm, tf, u):
    G, rows, D = x.shape
    F = w_conv.shape[-1]
    R = max(8, 2 * u)
    nm, nf = rows // tm, F // tf
    rr = gt.shape[1]
    mod_spec = (pl.BlockSpec((1, 1, D), lambda b, i, f: (b, 0, 0)) if rr == 1
                else pl.BlockSpec((1, tm, D), lambda b, i, f: (b, i, 0)))
    emit = weights[0] == 'f32'
    if emit:
        assert G == 1 and nm == 1, "weights are emitted tile by tile: every tile must be visited once"
        _, w_in_f32, w_out_f32, layer = weights
        w_args = (w_in_f32, w_in_f32, w_out_f32)
        w_specs = [pl.BlockSpec((None, D, tf), lambda b, i, f: (layer, 0, f)),
                   pl.BlockSpec((None, D, tf), lambda b, i, f: (layer, 0, nf + f)),
                   pl.BlockSpec((None, tf, D), lambda b, i, f: (layer, f, 0))]
    else:
        w_args = weights[1:]
        w_specs = [pl.BlockSpec((D, tf), lambda b, i, f: (0, f)),
                   pl.BlockSpec((D, tf), lambda b, i, f: (0, f)),
                   pl.BlockSpec((tf, D), lambda b, i, f: (f, 0))]
    w_out_specs = [pl.BlockSpec((D, tf), lambda b, i, f: (0, f)),
                   pl.BlockSpec((D, tf), lambda b, i, f: (0, f)),
                   pl.BlockSpec((tf, D), lambda b, i, f: (f, 0))] if emit else []
    w_out_shapes = [jax.ShapeDtypeStruct((D, F), BF16), jax.ShapeDtypeStruct((D, F), BF16),
                    jax.ShapeDtypeStruct((F, D), BF16)] if emit else []
    outs = pl.pallas_call(
        functools.partial(_ffn_kernel, tm=tm, u=u, R=R, tiles_per_seq=nm, emit=emit),
        grid=(G, nm, nf),
        in_specs=[pl.BlockSpec((1, tm, D), lambda b, i, f: (b, i, 0)),
                  w_specs[0], w_specs[1],
                  pl.BlockSpec((FFN_CONV, tf), lambda b, i, f: (0, f)),
                  pl.BlockSpec((1, tf), lambda b, i, f: (0, f)),
                  w_specs[2],
                  pl.BlockSpec((1, tm, D), lambda b, i, f: (b, i, 0)),
                  mod_spec,
                  pl.BlockSpec((1, R, tf), lambda b, i, f: (b, 0, f))],
        out_specs=[pl.BlockSpec((1, tm, D), lambda b, i, f: (b, i, 0)),
                   pl.BlockSpec((1, 1, R, tf), lambda b, i, f: (b, i, 0, f))] + w_out_specs,
        out_shape=[jax.ShapeDtypeStruct((G, rows, D), F32),
                   jax.ShapeDtypeStruct((G, nm, R, F), F32)] + w_out_shapes,
        scratch_shapes=[pltpu.VMEM((R + tm, tf), F32), pltpu.VMEM((nf, R, tf), F32)],
        compiler_params=_cparams(("arbitrary", "arbitrary", "arbitrary")),
        name="ffn",
    )(h2, w_args[0], w_args[1], w_conv, b_conv.reshape(1, F), w_args[2], x, gt, init)
    xo, st = outs[0], outs[1]
    return xo, st[:, nm - 1], tuple(outs[2:])


N_ATTN_COLS = 3 * ATTN_WIDTH
N_GATE_COL = N_ATTN_COLS + 3 * MLSTM_WIDTH
N_TAIL_COL = N_GATE_COL + 2 * MLSTM_HEADS
PACK_BLOCKS = N_PACK // LANE


def _pack_src(j, n_in):
    blk_mi, blk_mf, blk_mqk, blk_mo = C_MI // LANE, C_MF // LANE, C_MQK // LANE, C_MO // LANE
    last = PACK_BLOCKS - 1
    last_real = n_in - (N_TAIL_COL + (last - blk_mo) * LANE)
    start = jnp.where(j < blk_mi, j * LANE,
            jnp.where(j == blk_mi, N_GATE_COL,
            jnp.where(j == blk_mf, N_GATE_COL + MLSTM_HEADS,
            jnp.where(j < blk_mo, N_ATTN_COLS + (j - blk_mqk) * LANE,
            jnp.where(j < last, N_TAIL_COL + (j - blk_mo) * LANE, n_in - LANE)))))
    valid = jnp.where((j == blk_mi) | (j == blk_mf), MLSTM_HEADS, jnp.where(j == last, last_real, LANE))
    return start, valid, last_real


def _pack_w_kernel(wt_hbm, o_ref, buf, sem, *, n_in, depth):
    j = pl.program_id(0)
    nblk = pl.num_programs(0)

    def copies(jj, slot):
        start, _, _ = _pack_src(jj, n_in)
        return [pltpu.make_async_copy(wt_hbm.at[pl.ds(start, LANE), l, :], buf.at[slot, l], sem.at[slot, l])
                for l in range(depth)]

    @pl.when(j == 0)
    def _():
        for cp in copies(j, 0):
            cp.start()

    @pl.when(j + 1 < nblk)
    def _():
        for cp in copies(j + 1, (j + 1) % 2):
            cp.start()

    slot = j % 2
    for cp in copies(j, slot):
        cp.wait()
    _, valid, last_real = _pack_src(j, n_in)
    row = lax.broadcasted_iota(jnp.int32, (LANE, 1), 0)
    for l in range(depth):
        @pl.when(j == nblk - 1)
        def _():
            tail = buf[slot, l, LANE - last_real:LANE, :]
            buf[slot, l, 0:last_real, :] = tail

        x = jnp.where(row < valid, buf[slot, l], 0.0)
        o_ref[l] = x.T.astype(BF16)


def _pack_w_call(w_in):
    depth, d, n_in = w_in.shape
    wt = w_in.transpose(2, 0, 1)
    return pl.pallas_call(
        functools.partial(_pack_w_kernel, n_in=n_in, depth=depth),
        grid=(PACK_BLOCKS,),
        in_specs=[pl.BlockSpec(memory_space=pl.ANY)],
        out_specs=pl.BlockSpec((depth, d, LANE), lambda j: (0, 0, j)),
        out_shape=jax.ShapeDtypeStruct((depth, d, N_PACK), BF16),
        scratch_shapes=[pltpu.VMEM((2, depth, LANE, d), F32), pltpu.SemaphoreType.DMA((2, depth))],
        compiler_params=_cparams(("arbitrary",)),
        name="pack_w_in",
    )(wt)


def _pack_in_proj(w_in, b_in):
    def split(a):
        out, off = [], 0
        for s in IN_SPLITS:
            out.append(a[..., off:off + s])
            off += s
        return out

    def pad(a, n):
        return jnp.pad(a, [(0, 0)] * (a.ndim - 1) + [(0, n - a.shape[-1])])

    def pack(a):
        aq, ak, av, mqk, mv, mi, mf, mo, gq, gk, gv, gg, ga = split(a)
        return jnp.concatenate([aq, ak, av, pad(mi, LANE), pad(mf, LANE), mqk, mv, mo, gq, gk, gv, gg,
                                pad(ga, LANE)], axis=-1)

    return _pack_w_call(w_in), pack(b_in)


def _rope_tables(pos):
    half = HEAD_DIM // 2
    inv_freq = jnp.power(ROPE_THETA, -jnp.arange(half, dtype=F32) / half)
    ang = pos.astype(F32)[:, None] * inv_freq[None, :]
    cos, sin = jnp.cos(ang), jnp.sin(ang)
    zero = jnp.zeros_like(sin)
    reps = LANE // HEAD_DIM
    return (jnp.tile(jnp.concatenate([cos, cos], -1), (1, reps)),
            jnp.tile(jnp.concatenate([-sin, zero], -1), (1, reps)),
            jnp.tile(jnp.concatenate([zero, sin], -1), (1, reps)))


def _pick_tile(n, pref):
    t = math.gcd(n, pref)
    return t


def kernel(x_prompt, x_sample, c_prompt, c_sample, cache_win0_kv, cache_win1_kv, cache_win2_kv, state_mlstm_C, state_mlstm_n, state_mlstm_m, state_mlstm_conv, state_gla_S, state_ffn_conv, w_ada, b_ada, g_norm1, g_norm2, w_in, b_in, w_mconv, b_mconv, g_mlstm, w_gla_a2, b_gla_a2, g_gla, w_out, w_ff_in, w_fconv, b_fconv, w_ff_out, g_final):
    B, S, D = x_prompt.shape
    Bs, Ts, _ = x_sample.shape
    depth = w_ada.shape[0]
    d_ff = w_fconv.shape[-1]
    caches = (cache_win0_kv, cache_win1_kv, cache_win2_kv)
    Ms = Bs * Ts

    n_c = B + Bs
    rows_c = -(-n_c // 8) * 8
    c_all = jnp.pad(jnp.concatenate([c_prompt, c_sample], axis=0), ((0, rows_c - n_c), (0, 0)))
    mod = _ada_call(c_all, w_ada, b_ada).reshape(depth, rows_c, 6, D)

    w_in_p, b_in_p = _pack_in_proj(w_in, b_in)
    w_out_b = w_out.astype(BF16)

    rope_p = _rope_tables(jnp.arange(S))
    rope_s = _rope_tables(PAST_LEN + jnp.repeat(jnp.arange(Ts), Bs))

    tm_p = _pick_tile(S, 512)
    tm_in = _pick_tile(S, 1024)
    tm_ff = _pick_tile(S, 1024)
    tn = N_PACK // 4
    tf = _pick_tile(d_ff, 512)

    xp = x_prompt
    xs = x_sample.transpose(1, 0, 2).reshape(1, Ms, D)
    zeros_p = {
        'mconv': jnp.zeros((B, MLSTM_CONV - 1, 2 * MLSTM_WIDTH), F32),
        'C': jnp.zeros((B, MLSTM_HEADS, HEAD_DIM, HEAD_DIM), F32),
        'n': jnp.zeros((B, MLSTM_HEADS, HEAD_DIM), F32),
        'm': jnp.zeros((B, MLSTM_HEADS), F32),
        'S': jnp.zeros((B, GLA_HEADS, GLA_DK, GLA_DV), F32),
        'fconv': jnp.zeros((B, 8, d_ff), F32),
    }
    names = ('win0', 'win1', 'win2', 'C', 'n', 'm', 'mconv', 'S', 'fconv')
    col_p = {k: [] for k in names}
    col_s = {k: [] for k in names}
    caches_t = [c.transpose(0, 1, 3, 4, 5, 2) for c in caches]
    new_caches = None

    for l in range(depth):
        mp = mod[l, :B]
        ms_ = jnp.tile(mod[l, B:B + Bs], (Ts, 1, 1))

        def mods_p(i):
            return mp[:, i:i + 1, :]

        def mods_s(i):
            return ms_[None, :, i, :]

        h = _normmod_call(xp, g_norm1[l], mods_p(1), mods_p(0), BF16, tm_p)
        proj = _inproj_call(h.reshape(B * S, D), w_in_p, l, b_in_p[l][None], *rope_p, tm_in, tn)
        proj = proj.reshape(B, S, N_PACK)
        os_, ls_ = [], []
        for g, (win, dil) in enumerate(ATTN_GROUPS):
            o, lse = _attn_prompt_call(proj, g, dil)
            os_.append(o)
            ls_.append(lse)
            col_p[f'win{g}'].append(_win_extract_call(proj, g, min(win, S)))
        ya = os_ + ls_
        ym, C, n, m, cso = _mlstm_call(proj, zeros_p['mconv'], w_mconv[l], b_mconv[l], zeros_p['C'],
                                        zeros_p['n'], zeros_p['m'], g_mlstm[l], math.gcd(S, MLSTM_CHUNK))
        yg, Sg = _gla_call(proj, w_gla_a2[l], b_gla_a2[l], zeros_p['S'], g_gla[l], math.gcd(S, GLA_CHUNK))
        xp, h2_p = _outproj_call(ya, ym, yg, w_out_b, l, xp, mods_p(2), g_norm2[l], mods_p(4), mods_p(3), tm_p)
        for k_, v_ in (('C', C), ('n', n), ('m', m), ('mconv', cso), ('S', Sg)):
            col_p[k_].append(v_)

        h = _normmod_call(xs, g_norm1[l], mods_s(1), mods_s(0), BF16, Ms)
        proj = _inproj_call(h.reshape(Ms, D), w_in_p, l, b_in_p[l][None], *rope_s, Ms, tn)
        proj_b = proj.reshape(Ts, Bs, N_PACK).transpose(1, 0, 2)
        ya, new_caches = _attn_sample_call(proj_b[:, :, :3 * ATTN_WIDTH], caches_t, l, new_caches)
        ym, C, n, m, cso = _mlstm_call(proj_b, state_mlstm_conv[l], w_mconv[l], b_mconv[l], state_mlstm_C[l],
                                        state_mlstm_n[l], state_mlstm_m[l], g_mlstm[l], Ts)
        yg, Sg = _gla_call(proj_b, w_gla_a2[l], b_gla_a2[l], state_gla_S[l], g_gla[l], Ts)

        def tmaj(a):
            return a.transpose(1, 0, 2).reshape(1, Ms, a.shape[-1])

        xs, h2 = _outproj_call(tmaj(ya), tmaj(ym), tmaj(yg), w_out_b, l, xs, mods_s(2), g_norm2[l],
                               mods_s(4), mods_s(3), Ms)
        n_st = (FFN_CONV - 1) * Bs
        r_st = max(8, n_st)
        init = state_ffn_conv[l].transpose(1, 0, 2).reshape(1, n_st, d_ff)
        init = jnp.pad(init, ((0, 0), (r_st - n_st, 0), (0, 0)))
        xs, fst, w_bf16 = _ffn_call(h2, ('f32', w_ff_in, w_ff_out, l), w_fconv[l], b_fconv[l], xs, mods_s(5),
                                    init, Ms, tf, Bs)
        fst = fst[:, r_st - n_st:].reshape(FFN_CONV - 1, Bs, d_ff).transpose(1, 0, 2)
        for k_, v_ in (('C', C), ('n', n), ('m', m), ('mconv', cso), ('S', Sg), ('fconv', fst)):
            col_s[k_].append(v_)

        xp, fst, _ = _ffn_call(h2_p, ('bf16',) + w_bf16, w_fconv[l], b_fconv[l], xp, mods_p(5),
                               zeros_p['fconv'], tm_ff, tf, 1)
        col_p['fconv'].append(fst[:, 6:8])

    y_prompt = _normmod_call(xp, g_final, jnp.zeros((B, 1, D), F32), jnp.zeros((B, 1, D), F32), F32, tm_p)
    y_sample = _normmod_call(xs, g_final, jnp.zeros((1, 1, D), F32), jnp.zeros((1, 1, D), F32), F32, Ms)
    y_sample = y_sample.reshape(Ts, Bs, D).transpose(1, 0, 2)
    sp = {k: jnp.stack(v, axis=0) for k, v in col_p.items()}
    ss = {k: jnp.stack(v, axis=0) for k, v in col_s.items() if v}
    for g in range(3):
        ss[f'win{g}'] = new_caches[g].transpose(0, 1, 5, 2, 3, 4)
    return (y_prompt, y_sample, sp['win0'], ss['win0'], sp['win1'], ss['win1'], sp['win2'], ss['win2'],
            sp['C'], ss['C'], sp['n'], ss['n'], sp['m'], ss['m'], sp['mconv'], ss['mconv'],
            sp['S'], ss['S'], sp['fconv'], ss['fconv'])
```

```python
import functools
import math

import jax
import jax.numpy as jnp
from jax import lax
from jax.experimental import pallas as pl
from jax.experimental.pallas import tpu as pltpu

F32 = jnp.float32
BF16 = jnp.bfloat16
HI = lax.Precision.HIGHEST

HEAD_DIM = 64
ATTN_GROUPS = ((128, 1), (512, 4), (2048, 16))
ATTN_HPG = 4
ATTN_WIDTH = 3 * ATTN_HPG * HEAD_DIM
GROUP_W = ATTN_HPG * HEAD_DIM
ATTN_SCALE = HEAD_DIM ** -0.5
ROPE_THETA = 10000.0
BAND = 128
RES_UNROLL = 4
MIXER_BATCH = 4
MLSTM_HEADS = 10
MLSTM_WIDTH = MLSTM_HEADS * HEAD_DIM
MLSTM_CONV = 4
MLSTM_CHUNK = 128
GLA_HEADS = 10
GLA_DK = 32
GLA_DV = 64
GLA_K_WIDTH = GLA_HEADS * GLA_DK
GLA_V_WIDTH = GLA_HEADS * GLA_DV
GLA_RANK = 16
GLA_TAU = 16.0
GLA_CHUNK = 128
GLA_SAFE_DECAY = 60.0
FFN_CONV = 3
FFN_SUB = 512
EPS = 1e-6
PAST_LEN = 8192
NEG = -1e30

LANE = 128
VMEM_LIMIT = 61 * 1024 * 1024

IN_SPLITS = (ATTN_WIDTH, ATTN_WIDTH, ATTN_WIDTH, 2 * MLSTM_WIDTH, MLSTM_WIDTH, MLSTM_HEADS, MLSTM_HEADS,
             MLSTM_WIDTH, GLA_K_WIDTH, GLA_K_WIDTH, GLA_V_WIDTH, GLA_V_WIDTH, GLA_RANK)

C_AQ, C_AK, C_AV = 0, 768, 1536
C_MI, C_MF = 2304, 2432
C_MQK = 2560
C_MV, C_MO = 3840, 4480
C_GQK, C_GV, C_GG = 5120, 5760, 6400
C_GA = 7040
N_PACK = 7168
ROPE_COLS = 2 * ATTN_WIDTH


def _cparams(sem):
    return pltpu.CompilerParams(dimension_semantics=sem, vmem_limit_bytes=VMEM_LIMIT)


def _sigmoid(x):
    return 1.0 / (1.0 + jnp.exp(-x))


def _log_sigmoid(x):
    return jnp.minimum(x, 0.0) - jnp.log(1.0 + jnp.exp(-jnp.abs(x)))


def _dot(a, b):
    return jnp.dot(a, b, preferred_element_type=F32)


def _dot_nt(a, b):
    return lax.dot_general(a, b, (((1,), (1,)), ((), ())), preferred_element_type=F32)


def _split3(x):
    hi = x.astype(BF16)
    r1 = x - hi.astype(F32)
    mid = r1.astype(BF16)
    lo = (r1 - mid.astype(F32)).astype(BF16)
    return hi, mid, lo


def _dot_sel(sel, x, parts=3):
    sb = sel.astype(BF16)
    return sum(_dot(sb, p) for p in _split3(x)[:parts])


def _dot_sel_rhs(x, sel, parts=2):
    sb = sel.astype(BF16)
    return sum(_dot(p, sb) for p in _split3(x)[:parts])


def _dot_sel_nt(x, sel):
    sb = sel.astype(BF16)
    return sum(_dot_nt(p, sb) for p in _split3(x))


def _ada_kernel(c_ref, w_ref, b_ref, o_ref):
    c = c_ref[...]
    s = (c * _sigmoid(c)).astype(BF16)
    o_ref[0] = _dot(s, w_ref[0].astype(BF16)) + b_ref[0]


def _ada_call(c_all, w_ada, b_ada):
    depth, d, n = w_ada.shape
    rows = c_all.shape[0]
    tn = 1024
    return pl.pallas_call(
        _ada_kernel,
        grid=(depth, n // tn),
        in_specs=[pl.BlockSpec((rows, d), lambda l, j: (0, 0)),
                  pl.BlockSpec((1, d, tn), lambda l, j: (l, 0, j)),
                  pl.BlockSpec((1, 1, tn), lambda l, j: (l, 0, j))],
        out_specs=pl.BlockSpec((1, rows, tn), lambda l, j: (l, 0, j)),
        out_shape=jax.ShapeDtypeStruct((depth, rows, n), F32),
        compiler_params=_cparams(("arbitrary", "arbitrary")),
        name="ada",
    )(c_all, w_ada, b_ada.reshape(depth, 1, n))


def _normmod_kernel(x_ref, g_ref, sc_ref, sh_ref, o_ref):
    x = x_ref[0]
    y = x * lax.rsqrt(jnp.mean(x * x, axis=-1, keepdims=True) + EPS) * g_ref[...]
    o_ref[0] = (y * (1.0 + sc_ref[0]) + sh_ref[0]).astype(o_ref.dtype)


def _normmod_call(x, g, sc, sh, out_dtype, tm):
    G, R, D = x.shape
    rr = sc.shape[1]
    mod_spec = (pl.BlockSpec((1, 1, D), lambda b, i: (b, 0, 0)) if rr == 1
                else pl.BlockSpec((1, tm, D), lambda b, i: (b, i, 0)))
    return pl.pallas_call(
        _normmod_kernel,
        grid=(G, R // tm),
        in_specs=[pl.BlockSpec((1, tm, D), lambda b, i: (b, i, 0)),
                  pl.BlockSpec((1, D), lambda b, i: (0, 0)),
                  mod_spec, mod_spec],
        out_specs=pl.BlockSpec((1, tm, D), lambda b, i: (b, i, 0)),
        out_shape=jax.ShapeDtypeStruct((G, R, D), out_dtype),
        compiler_params=_cparams(("arbitrary", "arbitrary")),
        name="normmod",
    )(x, g.reshape(1, D), sc, sh)


def _rope_chunk(x, cos, sin_a, sin_b):
    return x * cos + pltpu.roll(x, 96, 1) * sin_a + pltpu.roll(x, 32, 1) * sin_b


def _inproj_kernel(h_ref, w_ref, b_ref, cos_ref, sa_ref, sb_ref, o_ref, *, tn):
    j = pl.program_id(0)
    o_ref[...] = _dot(h_ref[...], w_ref[...]) + b_ref[...]
    n_chunks = tn // LANE
    rope_tiles = -(-ROPE_COLS // tn)
    for jt in range(rope_tiles):
        chunks = min(n_chunks, (ROPE_COLS - jt * tn) // LANE)

        @pl.when(j == jt)
        def _():
            cos, sa, sb = cos_ref[...], sa_ref[...], sb_ref[...]
            for c in range(chunks):
                sl = slice(c * LANE, (c + 1) * LANE)
                o_ref[:, sl] = _rope_chunk(o_ref[:, sl], cos, sa, sb)


def _inproj_call(h, w, layer, b, cos, sin_a, sin_b, tm, tn):
    M, D = h.shape
    N = w.shape[2]
    tp = cos.shape[0] // tm
    tab = pl.BlockSpec((tm, LANE), lambda j, i: (i % tp, 0))
    return pl.pallas_call(
        functools.partial(_inproj_kernel, tn=tn),
        grid=(N // tn, M // tm),
        in_specs=[pl.BlockSpec((tm, D), lambda j, i: (i, 0)),
                  pl.BlockSpec((None, D, tn), lambda j, i: (layer, 0, j)),
                  pl.BlockSpec((1, tn), lambda j, i: (0, j)),
                  tab, tab, tab],
        out_specs=pl.BlockSpec((tm, tn), lambda j, i: (i, j)),
        out_shape=jax.ShapeDtypeStruct((M, N), F32),
        compiler_params=_cparams(("arbitrary", "arbitrary")),
        name="inproj",
    )(h, w, b, cos, sin_a, sin_b)


def _attn_prompt_kernel(q_ref, kp_ref, kc_ref, vp_ref, vc_ref, o_ref, l_ref, obuf, lbuf, *, dil, nq):
    i = pl.program_id(2)
    row = lax.broadcasted_iota(jnp.int32, (BAND, 2 * BAND), 0)
    col = lax.broadcasted_iota(jnp.int32, (BAND, 2 * BAND), 1)
    band = (col >= row) & (col <= row + BAND)
    band_first = band & ((col >= BAND) | (i > 0))
    nr = nq * BAND

    def residues(rs):
        rows_of, items = [], []
        for u, r in enumerate(rs):
            if dil == 1:
                rows, prev_rows = slice(None), slice(nr - BAND, nr)
            else:
                rows, prev_rows = pl.ds(r, nr, stride=dil), pl.ds((nr - BAND) * dil + r, BAND, stride=dil)
            rows_of.append(rows)
            q_all = q_ref[0, rows, :].astype(BF16)
            k_all = jnp.concatenate([kp_ref[0, prev_rows, :], kc_ref[0, rows, :]], axis=0).astype(BF16)
            v_all = jnp.concatenate([vp_ref[0, prev_rows, :], vc_ref[0, rows, :]], axis=0).astype(BF16)
            for sb in range(nq):
                qs = slice(sb * BAND, (sb + 1) * BAND)
                ks = slice(sb * BAND, (sb + 2) * BAND)
                for h in range(LANE // HEAD_DIM):
                    hs = slice(h * HEAD_DIM, (h + 1) * HEAD_DIM)
                    items.append((u, qs, hs, q_all[qs, hs], k_all[ks, hs], v_all[ks, hs],
                                  band_first if sb == 0 else band))
        ss = [jnp.where(it[6], _dot_nt(it[3], it[4]) * ATTN_SCALE, NEG) for it in items]
        mxs = [jnp.max(s, axis=-1, keepdims=True) for s in ss]
        es = [jnp.exp(s - mx) for s, mx in zip(ss, mxs)]
        dens = [jnp.sum(e, axis=-1, keepdims=True) for e in es]
        pvs = [_dot(e.astype(BF16), it[5]) for e, it in zip(es, items)]
        outs = [pv / den for pv, den in zip(pvs, dens)]
        lses = [jnp.broadcast_to(mx + jnp.log(den), (BAND, HEAD_DIM)) for mx, den in zip(mxs, dens)]
        for it, o, l in zip(items, outs, lses):
            obuf[it[0], it[1], it[2]] = o
            lbuf[it[0], it[1], it[2]] = l
        for u, rows in enumerate(rows_of):
            o_ref[0, rows, :] = obuf[u]
            l_ref[0, rows, :] = lbuf[u]

    if dil == 1:
        residues([0])
    elif dil <= RES_UNROLL:
        residues(list(range(dil)))
    else:
        def body(j, carry):
            residues([j * RES_UNROLL + u for u in range(RES_UNROLL)])
            return carry

        lax.fori_loop(0, dil // RES_UNROLL, body, 0)


def _attn_prompt_call(proj, g, dil):
    B, S, N = proj.shape
    nq = max(1, math.gcd(S, 512) // (BAND * dil))
    tb = nq * BAND * dil
    nb = S // tb
    hp = GROUP_W // LANE
    cq, ck, cv = (C_AQ + g * GROUP_W) // LANE, (C_AK + g * GROUP_W) // LANE, (C_AV + g * GROUP_W) // LANE

    def spec(cblk, prev):
        if prev:
            return pl.BlockSpec((1, tb, LANE), lambda b, p, i: (b, jnp.maximum(i - 1, 0), cblk + p))
        return pl.BlockSpec((1, tb, LANE), lambda b, p, i: (b, i, cblk + p))

    ospec = pl.BlockSpec((1, tb, LANE), lambda b, p, i: (b, i, p))
    o, lse = pl.pallas_call(
        functools.partial(_attn_prompt_kernel, dil=dil, nq=nq),
        grid=(B, hp, nb),
        in_specs=[spec(cq, False), spec(ck, True), spec(ck, False), spec(cv, True), spec(cv, False)],
        out_specs=[ospec, ospec],
        out_shape=[jax.ShapeDtypeStruct((B, S, GROUP_W), F32)] * 2,
        scratch_shapes=[pltpu.VMEM((min(dil, RES_UNROLL), nq * BAND, LANE), F32)] * 2,
        compiler_params=_cparams(("arbitrary", "arbitrary", "arbitrary")),
        name=f"attn_prompt_g{g}",
    )(proj, proj, proj, proj, proj)
    return o, lse


def _win_extract_kernel(x_ref, o_ref):
    o_ref[0, 0] = x_ref[0].T


def _win_extract_call(proj, g, keep):
    B, S, N = proj.shape
    chunk = min(keep, 512)
    first = (S - keep) // chunk
    cblk = C_AK // GROUP_W + g
    step = (C_AV - C_AK) // GROUP_W
    out = pl.pallas_call(
        _win_extract_kernel,
        grid=(B, 2, keep // chunk),
        in_specs=[pl.BlockSpec((1, chunk, GROUP_W), lambda b, kv, j: (b, first + j, cblk + step * kv))],
        out_specs=pl.BlockSpec((1, 1, GROUP_W, chunk), lambda b, kv, j: (b, kv, 0, j)),
        out_shape=jax.ShapeDtypeStruct((B, 2, GROUP_W, keep), F32),
        compiler_params=_cparams(("arbitrary", "arbitrary", "arbitrary")),
        name=f"win_extract_g{g}",
    )(proj)
    return out.reshape(B, 2, ATTN_HPG, HEAD_DIM, keep).transpose(0, 4, 1, 2, 3)


QROWS = 16


def _attn_sample_kernel(qkv_ref, c0_ref, c1_ref, c2_ref, *rest, T, n_prev):
    y_ref, n0_ref, n1_ref, n2_ref, q16, npad, ysc = rest[n_prev:]
    q16[...] = jnp.zeros_like(q16)
    npad[...] = jnp.zeros_like(npad)
    q16[0:T, :] = qkv_ref[0, :, 0:ATTN_WIDTH]
    lane = lax.broadcasted_iota(jnp.int32, (HEAD_DIM, LANE), 1)
    items = []
    for g, (cref, nref) in enumerate(((c0_ref, n0_ref), (c1_ref, n1_ref), (c2_ref, n2_ref))):
        Lb = cref.shape[-1]
        dil = ATTN_GROUPS[g][1]
        npad[g, 0:T, 0:GROUP_W] = qkv_ref[0, :, C_AK + g * GROUP_W:C_AK + (g + 1) * GROUP_W]
        npad[g, 0:T, GROUP_W:2 * GROUP_W] = qkv_ref[0, :, C_AV + g * GROUP_W:C_AV + (g + 1) * GROUP_W]
        new_t = npad[g].T
        ncols = Lb + LANE
        t = lax.broadcasted_iota(jnp.int32, (QROWS, ncols), 0)
        c = lax.broadcasted_iota(jnp.int32, (QROWS, ncols), 1)
        if dil == 1:
            valid = ((c >= t) & (c < Lb)) | ((c >= Lb) & (c - Lb <= t) & (c - Lb < T))
        else:
            valid = ((c < Lb) & ((c & (dil - 1)) == t)) | (c == Lb + t)
        for h in range(ATTN_HPG):
            rk = slice(h * HEAD_DIM, (h + 1) * HEAD_DIM)
            rv = slice(GROUP_W + h * HEAD_DIM, GROUP_W + (h + 1) * HEAD_DIM)
            kc, vc = cref[0, 0, 0, h], cref[0, 0, 1, h]
            kt = jnp.concatenate([kc, new_t[rk, :]], axis=1).astype(BF16)
            vt = jnp.concatenate([vc, new_t[rv, :]], axis=1).astype(BF16)
            qh = q16[:, g * GROUP_W + h * HEAD_DIM:g * GROUP_W + (h + 1) * HEAD_DIM].astype(BF16)
            items.append((qh, kt, vt, valid))
            for kv, blk, rn in ((0, kc, rk), (1, vc, rv)):
                sh = pltpu.roll(blk, Lb - T, 1)
                newc = pltpu.roll(new_t[rn, :], LANE - T, 1)
                if Lb > LANE:
                    nref[0, 0, kv, h, :, 0:Lb - LANE] = sh[:, 0:Lb - LANE]
                nref[0, 0, kv, h, :, Lb - LANE:Lb] = jnp.where(lane >= LANE - T, newc, sh[:, Lb - LANE:Lb])
    ss = [jnp.where(valid, _dot(qh, kt) * ATTN_SCALE, NEG) for qh, kt, vt, valid in items]
    mxs = [jnp.max(s, axis=-1, keepdims=True) for s in ss]
    es = [jnp.exp(s - mx) for s, mx in zip(ss, mxs)]
    dens = [jnp.sum(e, axis=-1, keepdims=True) for e in es]
    pvs = [_dot_nt(e.astype(BF16), it[2]) for e, it in zip(es, items)]
    outs = [pv / den for pv, den in zip(pvs, dens)]
    lses = [mx + jnp.log(den) for mx, den in zip(mxs, dens)]
    for h in range(ATTN_HPG):
        l0, l1, l2 = lses[h], lses[ATTN_HPG + h], lses[2 * ATTN_HPG + h]
        mx = jnp.maximum(jnp.maximum(l0, l1), l2)
        es = [jnp.exp(l0 - mx), jnp.exp(l1 - mx), jnp.exp(l2 - mx)]
        inv = 1.0 / (es[0] + es[1] + es[2])
        for g in range(3):
            c0 = g * GROUP_W + h * HEAD_DIM
            ysc[:, c0:c0 + HEAD_DIM] = outs[g * ATTN_HPG + h] * (es[g] * inv)
    y_ref[0] = ysc[0:T, :].astype(y_ref.dtype)


def _attn_sample_call(qkv, caches_t, layer, prev):
    B, T, _ = qkv.shape
    depth = caches_t[0].shape[0]
    for g, (win, dil) in enumerate(ATTN_GROUPS):
        assert caches_t[g].shape[-1] == win and win // dil == BAND and (g == 0 or T <= dil), \
            "sample attention assumes full window caches"
    cspecs = [pl.BlockSpec((1, 1) + c.shape[2:], lambda b: (layer, b, 0, 0, 0, 0)) for c in caches_t]
    n_prev = 0 if prev is None else 3
    prev_args = [] if prev is None else list(prev)
    prev_specs = [pl.BlockSpec(memory_space=pl.ANY)] * n_prev
    aliases = {} if prev is None else {4 + g: 1 + g for g in range(3)}
    outs = pl.pallas_call(
        functools.partial(_attn_sample_kernel, T=T, n_prev=n_prev),
        grid=(B,),
        in_specs=[pl.BlockSpec((1, T, 3 * ATTN_WIDTH), lambda b: (b, 0, 0))] + cspecs + prev_specs,
        out_specs=[pl.BlockSpec((1, T, ATTN_WIDTH), lambda b: (b, 0, 0))] + cspecs,
        out_shape=[jax.ShapeDtypeStruct((B, T, ATTN_WIDTH), BF16)]
        + [jax.ShapeDtypeStruct(c.shape, F32) for c in caches_t],
        scratch_shapes=[pltpu.VMEM((QROWS, ATTN_WIDTH), F32),
                        pltpu.VMEM((3, LANE, 2 * GROUP_W), F32),
                        pltpu.VMEM((QROWS, ATTN_WIDTH), F32)],
        input_output_aliases=aliases,
        compiler_params=_cparams(("arbitrary",)),
        name="attn_sample",
    )(qkv, *caches_t, *prev_args)
    return outs[0], outs[1:]


def _mlstm_kernel(mqk_ref, mv_ref, mi_ref, mf_ref, mo_ref, cst_ref, wc_ref, bc_ref, c0_ref, n0_ref, m0_ref,
                  g_ref, tri_ref, y_ref, c_ref, n_ref, m_ref, cso_ref,
                  xbuf, vbuf, cs, ns, ms, tpad, kwp, *, T, Lp, nb):
    c = pl.program_id(1)
    last = pl.num_programs(1) - 1
    W = MLSTM_WIDTH
    bs = range(nb)
    heads = range(MLSTM_HEADS)
    hsl = [slice(h * HEAD_DIM, (h + 1) * HEAD_DIM) for h in heads]
    pairs = [(b, h) for b in bs for h in heads]

    @pl.when(c == 0)
    def _():
        xbuf[...] = jnp.zeros_like(xbuf)
        vbuf[...] = jnp.zeros_like(vbuf)
        tpad[...] = jnp.zeros_like(tpad)
        kwp[...] = jnp.zeros_like(kwp)
        xbuf[:, 5:8, :] = cst_ref[...]
        cs[...] = c0_ref[...]
        ns[...] = n0_ref[...]
        ms[...] = m0_ref[...]

    w = wc_ref[...]
    tri = tri_ref[...]
    causal = tri > 0.5
    rowid = lax.broadcasted_iota(jnp.int32, (Lp, LANE), 0)
    real = rowid < T
    lane_row = lax.broadcasted_iota(jnp.int32, (1, LANE), 1)

    for b in bs:
        xbuf[b, 8:8 + T, :] = mqk_ref[b]
        vbuf[b, 0:T, :] = mv_ref[b]
    ys = [bc_ref[...] + xbuf[b, 8:8 + Lp, :] * w[3:4] + xbuf[b, 7:7 + Lp, :] * w[2:3]
          + xbuf[b, 6:6 + Lp, :] * w[1:2] + xbuf[b, 5:5 + Lp, :] * w[0:1] for b in bs]
    tails = [xbuf[b, 8 + T - 3:8 + T, :] for b in bs]
    for b in bs:
        xbuf[b, 5:8, :] = tails[b]

    @pl.when(c == last)
    def _():
        for b in bs:
            cso_ref[b] = tails[b]

    qks = [y * _sigmoid(y) for y in ys]
    igs, lfs = [], []
    for b in bs:
        tpad[b, 0:T, :] = mi_ref[b]
        igs.append(jnp.where(real, tpad[b, 0:Lp, :], NEG))
        tpad[b, 0:T, :] = mf_ref[b]
        lfs.append(jnp.where(real, _log_sigmoid(tpad[b, 0:Lp, :]), 0.0))
    Fs = [_dot_sel(tri, lf) for lf in lfs]
    m_rows = [ms[b] for b in bs]
    inter_all = [Fs[b] + m_rows[b] for b in bs]
    FTs, ITs = [], []
    for b in bs:
        tpad[b, 0:Lp, :] = Fs[b]
        FTs.append(tpad[b].T)
        tpad[b, 0:Lp, :] = igs[b]
        ITs.append(tpad[b].T)
    v_alls = [vbuf[b, 0:Lp, :] for b in bs]
    n_alls = [ns[b] for b in bs]

    qs = [qks[b][:, hsl[h]] for b, h in pairs]
    ks = [qks[b][:, W + h * HEAD_DIM:W + (h + 1) * HEAD_DIM] * (HEAD_DIM ** -0.5) for b, h in pairs]
    qbs = [q.astype(BF16) for q in qs]
    vbs = [v_alls[b][:, hsl[h]].astype(BF16) for b, h in pairs]
    sqk = [_dot_nt(qb, k.astype(BF16)) for qb, k in zip(qbs, ks)]
    qcs = [_dot(qbs[i], cs[b, h].astype(BF16)) for i, (b, h) in enumerate(pairs)]
    nrows = [n_alls[b][h:h + 1, :] for b, h in pairs]
    qn = [_dot_nt(qbs[i], jnp.broadcast_to(nrows[i], (16, HEAD_DIM)).astype(BF16))[:, 0:1]
          for i in range(len(pairs))]

    fcols = [Fs[b][:, h:h + 1] for b, h in pairs]
    icols = [igs[b][:, h:h + 1] for b, h in pairs]
    inters = [inter_all[b][:, h:h + 1] for b, h in pairs]
    logws = [jnp.where(causal, fcols[i] - FTs[b][h:h + 1, 0:Lp] + ITs[b][h:h + 1, 0:Lp], NEG)
             for i, (b, h) in enumerate(pairs)]
    rmax = [jnp.max(lw, axis=-1, keepdims=True) for lw in logws]
    mts = [jnp.maximum(r, it) for r, it in zip(rmax, inters)]
    As = [jnp.exp(lw - mt) * s for lw, mt, s in zip(logws, mts, sqk)]
    gqs = [jnp.exp(it - mt) for it, mt in zip(inters, mts)]
    asum = [jnp.sum(A, axis=-1, keepdims=True) for A in As]
    dens = [a + g * q for a, g, q in zip(asum, gqs, qn)]
    a_bf = [A.astype(BF16) for A in As]
    avs = [_dot(a, v) for a, v in zip(a_bf, vbs)]
    mLs = [mt[Lp - 1:Lp, :] for mt in mts]
    flast = [Fs[b][Lp - 1:Lp, h:h + 1] for b, h in pairs]
    wLs = [jnp.exp(fl - fc + ic - mL) for fl, fc, ic, mL in zip(flast, fcols, icols, mLs)]
    gls = [jnp.exp(flast[i] + m_rows[b][0:1, h:h + 1] - mLs[i]) for i, (b, h) in enumerate(pairs)]
    kws = [k * wl for k, wl in zip(ks, wLs)]
    for i, (b, h) in enumerate(pairs):
        kwp[b, 0:Lp, hsl[h]] = kws[i]
    ksum = [jnp.sum(kw, axis=0, keepdims=True) for kw in kws]
    for i, (b, h) in enumerate(pairs):
        ns[b, h:h + 1, :] = gls[i] * nrows[i] + ksum[i]
    for b in bs:
        m_new = m_rows[b]
        for h in heads:
            m_new = jnp.where(lane_row == h, mLs[b * MLSTM_HEADS + h], m_new)
        ms[b] = m_new

    kwts = [kwp[b].T for b in bs]
    upds = [_dot(kwts[b][hsl[h], 0:Lp].astype(BF16), vbs[i]) for i, (b, h) in enumerate(pairs)]

    nums = [av + g * qc for av, g, qc in zip(avs, gqs, qcs)]
    hhs = [nu / jnp.maximum(jnp.abs(de), jnp.exp(-mt)) for nu, de, mt in zip(nums, dens, mts)]
    msq = [jnp.mean(hh * hh, axis=-1, keepdims=True) for hh in hhs]
    hns = [hhs[i] * lax.rsqrt(msq[i] + EPS) * g_ref[0:1, hsl[h]] for i, (b, h) in enumerate(pairs)]
    ogs = [_sigmoid(mo_ref[b, :, hsl[h]]) for b, h in pairs]
    for i, (b, h) in enumerate(pairs):
        yo = hns[i] * ogs[i] if Lp == T else hns[i][0:T] * ogs[i]
        y_ref[b, :, hsl[h]] = yo.astype(y_ref.dtype)
    for i, (b, h) in enumerate(pairs):
        cs[b, h] = gls[i] * cs[b, h] + upds[i]

    @pl.when(c == last)
    def _():
        c_ref[...] = cs[...]
        n_ref[...] = ns[...]
        m_ref[...] = ms[...]


def _mlstm_call(proj, conv_state, w_conv, b_conv, c0, n0, m0, g_mlstm, chunk):
    B, S, N = proj.shape
    T = chunk
    nc = S // T
    Lp = max(16, T)
    H = MLSTM_HEADS
    nb = math.gcd(B, MIXER_BATCH)
    m0p = jnp.pad(m0, ((0, 0), (0, LANE - H))).reshape(B, 1, LANE)
    tri = jnp.tril(jnp.ones((Lp, Lp), F32))

    def col(width, off):
        blk = off // width
        return pl.BlockSpec((nb, T, width), lambda b, c: (b, c, blk))

    def const(shape):
        nd = len(shape)
        return pl.BlockSpec(shape, lambda b, c: (0,) * nd)

    def per_b(shape):
        nd = len(shape)
        return pl.BlockSpec((nb,) + shape, lambda b, c: (b,) + (0,) * nd)

    outs = pl.pallas_call(
        functools.partial(_mlstm_kernel, T=T, Lp=Lp, nb=nb),
        grid=(B // nb, nc),
        in_specs=[col(2 * MLSTM_WIDTH, C_MQK), col(MLSTM_WIDTH, C_MV), col(LANE, C_MI), col(LANE, C_MF),
                  col(MLSTM_WIDTH, C_MO), per_b((MLSTM_CONV - 1, 2 * MLSTM_WIDTH)),
                  const((MLSTM_CONV, 2 * MLSTM_WIDTH)), const((1, 2 * MLSTM_WIDTH)),
                  per_b((H, HEAD_DIM, HEAD_DIM)), per_b((H, HEAD_DIM)), per_b((1, LANE)),
                  const((1, MLSTM_WIDTH)), const((Lp, Lp))],
        out_specs=[pl.BlockSpec((nb, T, MLSTM_WIDTH), lambda b, c: (b, c, 0)),
                   per_b((H, HEAD_DIM, HEAD_DIM)), per_b((H, HEAD_DIM)), per_b((1, LANE)),
                   per_b((MLSTM_CONV - 1, 2 * MLSTM_WIDTH))],
        out_shape=[jax.ShapeDtypeStruct((B, S, MLSTM_WIDTH), BF16),
                   jax.ShapeDtypeStruct((B, H, HEAD_DIM, HEAD_DIM), F32),
                   jax.ShapeDtypeStruct((B, H, HEAD_DIM), F32),
                   jax.ShapeDtypeStruct((B, 1, LANE), F32),
                   jax.ShapeDtypeStruct((B, MLSTM_CONV - 1, 2 * MLSTM_WIDTH), F32)],
        scratch_shapes=[pltpu.VMEM((nb, 8 + Lp, 2 * MLSTM_WIDTH), F32),
                        pltpu.VMEM((nb, Lp, MLSTM_WIDTH), F32),
                        pltpu.VMEM((nb, H, HEAD_DIM, HEAD_DIM), F32),
                        pltpu.VMEM((nb, H, HEAD_DIM), F32),
                        pltpu.VMEM((nb, 1, LANE), F32),
                        pltpu.VMEM((nb, LANE, LANE), F32),
                        pltpu.VMEM((nb, LANE, MLSTM_WIDTH), F32)],
        compiler_params=_cparams(("arbitrary", "arbitrary")),
        name="mlstm",
    )(proj, proj, proj, proj, proj, conv_state, w_conv, b_conv.reshape(1, -1), c0, n0, m0p,
      g_mlstm.reshape(1, -1), tri)
    y, C, n, m, cso = outs
    return y, C, n, m[:, 0, :H], cso


def _gla_kernel(gqk_ref, gv_ref, gg_ref, ga_ref, wa_ref, wat_ref, s0_ref, g_ref, tri_ref, bd_ref, ee_ref,
                y_ref, s_ref, qkp, vp, gap, sbd, osc, *, T, Lp, nb):
    c = pl.program_id(1)
    last = pl.num_programs(1) - 1
    KW, VW = GLA_K_WIDTH, GLA_V_WIDTH
    bs = range(nb)
    heads = range(GLA_HEADS)

    @pl.when(c == 0)
    def _():
        qkp[...] = jnp.zeros_like(qkp)
        vp[...] = jnp.zeros_like(vp)
        gap[...] = jnp.zeros_like(gap)
        sbd[...] = jnp.zeros_like(sbd)
        osc[...] = jnp.zeros_like(osc)
        for b in bs:
            for h in heads:
                sbd[b, h * GLA_DK:(h + 1) * GLA_DK, h * GLA_DV:(h + 1) * GLA_DV] = s0_ref[b, h]

    lane = lax.broadcasted_iota(jnp.int32, (T, LANE), 1)
    for b in bs:
        qkp[b, 0:T, :] = gqk_ref[b]
        vp[b, 0:T, :] = gv_ref[b]
        gap[b, 0:T, :] = jnp.where(lane == GLA_RANK, 1.0, ga_ref[b])

    tri = tri_ref[...]
    rowid = lax.broadcasted_iota(jnp.int32, (Lp, KW), 0)
    colid = lax.broadcasted_iota(jnp.int32, (KW, LANE), 1)
    wa_hi, wa_lo, _ = _split3(wa_ref[...])
    wat_hi, wat_lo, _ = _split3(wat_ref[...])
    ga_bs = [gap[b, 0:Lp, :].astype(BF16) for b in bs]
    gat_bs = [gap[b].T.astype(BF16) for b in bs]
    las = [_dot(g, wa_hi) + _dot(g, wa_lo) for g in ga_bs]
    las = [jnp.where(rowid < T, _log_sigmoid(la) / GLA_TAU, 0.0) for la in las]
    bcs = [_dot_sel(tri[0:Lp, 0:Lp], la) for la in las]
    lats = [_dot(wat_hi, g) + _dot(wat_lo, g) for g in gat_bs]
    lats = [jnp.where(colid < T, _log_sigmoid(lat) / GLA_TAU, 0.0) for lat in lats]
    bcts = [_dot_sel_nt(lat, tri) for lat in lats]
    blcols = [bct[:, LANE - 1:LANE] for bct in bcts]
    blrows = [bc[Lp - 1:Lp, :] for bc in bcs]
    kts = [qkp[b].T[KW:2 * KW, :] for b in bs]
    bd = bd_ref[...]
    min_decay = jnp.min(blrows[0])
    for b in bs[1:]:
        min_decay = jnp.minimum(min_decay, jnp.min(blrows[b]))
    safe = min_decay >= -GLA_SAFE_DECAY

    @pl.when(safe)
    def _():
        causal = tri[0:Lp, 0:Lp] > 0.5
        qbs = [(qkp[b, 0:Lp, 0:KW] * (GLA_DK ** -0.5) * jnp.exp(bcs[b])).astype(BF16) for b in bs]
        kbs = [(qkp[b, 0:Lp, KW:2 * KW] * jnp.exp(-bcs[b])).astype(BF16) for b in bs]
        vbs = [vp[b, 0:Lp, :].astype(BF16) for b in bs]
        o_inters = [_dot(qbs[b], sbd[b].astype(BF16)) for b in bs]
        pairs = [(b, h) for b in bs for h in heads]
        scores = [_dot_nt(qbs[b][:, h * GLA_DK:(h + 1) * GLA_DK], kbs[b][:, h * GLA_DK:(h + 1) * GLA_DK])
                  for b, h in pairs]
        a_bf = [jnp.where(causal, s, 0.0).astype(BF16) for s in scores]
        ovs = [_dot(a_bf[i], vbs[b][:, h * GLA_DV:(h + 1) * GLA_DV]) for i, (b, h) in enumerate(pairs)]
        klts = [(kts[b] * jnp.exp(blcols[b] - bcts[b])).astype(BF16) for b in bs]
        upds = [_dot(klts[b], vp[b].astype(BF16)) for b in bs]
        for b in bs:
            osc[b, 0:Lp, :] = o_inters[b] + jnp.concatenate(ovs[b * GLA_HEADS:(b + 1) * GLA_HEADS], axis=1)
            sbd[b] = bd * (jnp.exp(blcols[b]) * sbd[b] + upds[b])

    @pl.when(jnp.logical_not(safe))
    def _():
        srow = lax.broadcasted_iota(jnp.int32, (LANE, LANE), 0)
        for b in bs:
            def body(t, carry, b=b):
                sel = (srow == t).astype(F32)
                lac = jnp.dot(lats[b], sel, precision=HI, preferred_element_type=F32)
                kc = jnp.dot(kts[b], sel, precision=HI, preferred_element_type=F32)
                dec = jnp.concatenate([jnp.exp(lac)] * (VW // LANE), axis=1)
                kcw = jnp.concatenate([kc] * (VW // LANE), axis=1)
                vrow = vp[b, pl.ds(t, 1), :]
                snew = bd * (dec * sbd[b] + kcw * vrow)
                sbd[b] = snew
                qrow = jnp.broadcast_to(qkp[b, pl.ds(t, 1), 0:KW] * (GLA_DK ** -0.5), (8, KW))
                orow = jnp.dot(qrow, snew, precision=HI, preferred_element_type=F32)
                osc[b, pl.ds(t, 1), :] = orow[0:1]
                return carry

            lax.fori_loop(0, T, body, 0)

    os_ = [osc[b, 0:Lp, :] for b in bs]
    mss = [_dot_sel_rhs(o * o, ee_ref[...]) for o in os_]
    ogs = [o * lax.rsqrt(ms + EPS) * g_ref[...] for o, ms in zip(os_, mss)]
    for b in bs:
        gg = gg_ref[b]
        yo = (ogs[b] if Lp == T else ogs[b][0:T]) * (gg * _sigmoid(gg))
        y_ref[b] = yo.astype(y_ref.dtype)

    @pl.when(c == last)
    def _():
        for b in bs:
            for h in heads:
                s_ref[b, h] = sbd[b, h * GLA_DK:(h + 1) * GLA_DK, h * GLA_DV:(h + 1) * GLA_DV]


def _gla_call(proj, w_a2, b_a2, s0, g_gla, chunk):
    B, S, N = proj.shape
    T = chunk
    nc = S // T
    Lp = max(16, T)
    H, KW, VW = GLA_HEADS, GLA_K_WIDTH, GLA_V_WIDTH
    wa = jnp.zeros((LANE, KW), F32).at[:GLA_RANK].set(w_a2).at[GLA_RANK].set(b_a2)
    tri = jnp.tril(jnp.ones((LANE, LANE), F32))
    hk = jnp.arange(KW) // GLA_DK
    hv = jnp.arange(VW) // GLA_DV
    bd = (hk[:, None] == hv[None, :]).astype(F32)
    ee = (hv[:, None] == hv[None, :]).astype(F32) / GLA_DV

    nb = math.gcd(B, MIXER_BATCH)

    def col(width, off):
        blk = off // width
        return pl.BlockSpec((nb, T, width), lambda b, c: (b, c, blk))

    def const(shape):
        nd = len(shape)
        return pl.BlockSpec(shape, lambda b, c: (0,) * nd)

    y, s = pl.pallas_call(
        functools.partial(_gla_kernel, T=T, Lp=Lp, nb=nb),
        grid=(B // nb, nc),
        in_specs=[col(2 * KW, C_GQK), col(VW, C_GV), col(VW, C_GG), col(LANE, C_GA),
                  const((LANE, KW)), const((KW, LANE)),
                  pl.BlockSpec((nb, H, GLA_DK, GLA_DV), lambda b, c: (b, 0, 0, 0)),
                  const((1, VW)), const((LANE, LANE)), const((KW, VW)), const((VW, VW))],
        out_specs=[pl.BlockSpec((nb, T, VW), lambda b, c: (b, c, 0)),
                   pl.BlockSpec((nb, H, GLA_DK, GLA_DV), lambda b, c: (b, 0, 0, 0))],
        out_shape=[jax.ShapeDtypeStruct((B, S, VW), BF16),
                   jax.ShapeDtypeStruct((B, H, GLA_DK, GLA_DV), F32)],
        scratch_shapes=[pltpu.VMEM((nb, LANE, 2 * KW), F32),
                        pltpu.VMEM((nb, LANE, VW), F32),
                        pltpu.VMEM((nb, LANE, LANE), F32),
                        pltpu.VMEM((nb, KW, VW), F32),
                        pltpu.VMEM((nb, LANE, VW), F32)],
        compiler_params=_cparams(("arbitrary", "arbitrary")),
        name="gla",
    )(proj, proj, proj, proj, wa, wa.T, s0, g_gla.reshape(1, -1), tri, bd, ee)
    return y, s


def _outproj_kernel(*refs, n_attn):
    attn_refs = refs[:n_attn]
    ym_ref, yg_ref, w_ref, x_ref, gt_ref, g2_ref, sc_ref, sh_ref, xo_ref, h2_ref = refs[n_attn:]
    a, b = ATTN_WIDTH, ATTN_WIDTH + MLSTM_WIDTH
    if n_attn == 1:
        acc = _dot(attn_refs[0][0], w_ref[0:a, :])
    else:
        l0, l1, l2 = attn_refs[3][0], attn_refs[4][0], attn_refs[5][0]
        mx = jnp.maximum(jnp.maximum(l0, l1), l2)
        es = [jnp.exp(l0 - mx), jnp.exp(l1 - mx), jnp.exp(l2 - mx)]
        inv = 1.0 / (es[0] + es[1] + es[2])
        acc = None
        for g in range(3):
            yg_attn = (attn_refs[g][0] * (es[g] * inv)).astype(BF16)
            part = _dot(yg_attn, w_ref[g * GROUP_W:(g + 1) * GROUP_W, :])
            acc = part if acc is None else part + acc
    acc = acc + _dot(ym_ref[0], w_ref[a:b, :])
    acc = acc + _dot(yg_ref[0], w_ref[b:, :])
    x = x_ref[0] + gt_ref[0] * acc
    xo_ref[0] = x
    y = x * lax.rsqrt(jnp.mean(x * x, axis=-1, keepdims=True) + EPS) * g2_ref[...]
    h2_ref[0] = (y * (1.0 + sc_ref[0]) + sh_ref[0]).astype(h2_ref.dtype)


def _outproj_call(ya, ym, yg, w_out, layer, x, gt, g2, sc, sh, tm):
    G, R, D = x.shape
    rr = gt.shape[1]
    mod_spec = (pl.BlockSpec((1, 1, D), lambda b, i: (b, 0, 0)) if rr == 1
                else pl.BlockSpec((1, tm, D), lambda b, i: (b, i, 0)))

    def act(width):
        return pl.BlockSpec((1, tm, width), lambda b, i: (b, i, 0))

    attn = list(ya) if isinstance(ya, (list, tuple)) else [ya]
    return pl.pallas_call(
        functools.partial(_outproj_kernel, n_attn=len(attn)),
        grid=(G, R // tm),
        in_specs=[act(a_.shape[-1]) for a_ in attn] + [act(MLSTM_WIDTH), act(GLA_V_WIDTH),
                  pl.BlockSpec((None,) + w_out.shape[1:], lambda b, i: (layer, 0, 0)),
                  act(D), mod_spec, pl.BlockSpec((1, D), lambda b, i: (0, 0)), mod_spec, mod_spec],
        out_specs=[act(D), act(D)],
        out_shape=[jax.ShapeDtypeStruct((G, R, D), F32), jax.ShapeDtypeStruct((G, R, D), BF16)],
        compiler_params=_cparams(("arbitrary", "arbitrary")),
        name="outproj",
    )(*attn, ym, yg, w_out, x, gt, g2.reshape(1, D), sc, sh)


def _ffn_kernel(h_ref, wg_in, wu_in, wc_ref, bc_ref, wo_in, x_ref, gt_ref, init_ref, gf_ref,
                xo_ref, st_ref, *rest, tm, u, R, tiles_per_seq, emit, final_norm):
    m = pl.program_id(1)
    f = pl.program_id(2)
    nf = pl.num_programs(2)
    h = h_ref[0]
    if emit:
        wg_ref, wu_ref, wo_ref, ubuf, cbuf = rest
        wg_ref[...] = wg_in[...].astype(BF16)
        wu_ref[...] = wu_in[...].astype(BF16)
        wo_ref[...] = wo_in[...].astype(BF16)
    else:
        ubuf, cbuf = rest
        wg_ref, wu_ref, wo_ref = wg_in, wu_in, wo_in

    @pl.when(m % tiles_per_seq == 0)
    def _():
        ubuf[0:R, :] = init_ref[0]

    @pl.when(m % tiles_per_seq != 0)
    def _():
        ubuf[0:R, :] = cbuf[f]

    @pl.when(f == 0)
    def _():
        xo_ref[0] = jnp.zeros_like(xo_ref[0])

    tf = wg_ref.shape[1]
    chunks = [slice(c0, min(c0 + FFN_SUB, tf)) for c0 in range(0, tf, FFN_SUB)]

    def up_matmuls(cs_):
        return _dot(h, wg_ref[:, cs_]), _dot(h, wu_ref[:, cs_])

    pending = up_matmuls(chunks[0])
    acc = xo_ref[0]
    for ci, cs_ in enumerate(chunks):
        ug, uu = pending
        if ci + 1 < len(chunks):
            pending = up_matmuls(chunks[ci + 1])
        ubuf[R:R + tm, cs_] = ug
        w = wc_ref[:, cs_]
        gate = (bc_ref[:, cs_] + ug * w[2:3] + ubuf[R - u:R - u + tm, cs_] * w[1:2]
                + ubuf[R - 2 * u:R - 2 * u + tm, cs_] * w[0:1])
        act = (gate * _sigmoid(gate) * uu).astype(BF16)
        acc = _dot(act, wo_ref[cs_, :]) + acc
    xo_ref[0] = acc
    tail = ubuf[tm:tm + R, :]
    cbuf[f] = tail
    st_ref[0, 0] = tail

    @pl.when(f == nf - 1)
    def _():
        rows = min(tm, LANE)

        def chunk(r, carry):
            rs = pl.ds(pl.multiple_of(r * rows, rows), rows)
            gt = gt_ref[0] if gt_ref.shape[1] == 1 else gt_ref[0, rs, :]
            y = x_ref[0, rs, :] + gt * xo_ref[0, rs, :]
            if final_norm:
                y = y * lax.rsqrt(jnp.mean(y * y, axis=-1, keepdims=True) + EPS) * gf_ref[...]
            xo_ref[0, rs, :] = y
            return carry

        lax.fori_loop(0, tm // rows, chunk, 0)


def _ffn_call(h2, weights, w_conv, b_conv, x, gt, init, g_final, final_norm, tm, tf, u):
    G, rows, D = x.shape
    F = w_conv.shape[-1]
    R = max(8, 2 * u)
    nm, nf = rows // tm, F // tf
    rr = gt.shape[1]
    mod_spec = (pl.BlockSpec((1, 1, D), lambda b, i, f: (b, 0, 0)) if rr == 1
                else pl.BlockSpec((1, tm, D), lambda b, i, f: (b, i, 0)))
    emit = weights[0] == 'f32'
    if emit:
        assert G == 1 and nm == 1, "weights are emitted tile by tile: every tile must be visited once"
        _, w_in_f32, w_out_f32, layer = weights
        w_args = (w_in_f32, w_in_f32, w_out_f32)
        w_specs = [pl.BlockSpec((None, D, tf), lambda b, i, f: (layer, 0, f)),
                   pl.BlockSpec((None, D, tf), lambda b, i, f: (layer, 0, nf + f)),
                   pl.BlockSpec((None, tf, D), lambda b, i, f: (layer, f, 0))]
    else:
        w_args = weights[1:]
        w_specs = [pl.BlockSpec((D, tf), lambda b, i, f: (0, f)),
                   pl.BlockSpec((D, tf), lambda b, i, f: (0, f)),
                   pl.BlockSpec((tf, D), lambda b, i, f: (f, 0))]
    w_out_specs = [pl.BlockSpec((D, tf), lambda b, i, f: (0, f)),
                   pl.BlockSpec((D, tf), lambda b, i, f: (0, f)),
                   pl.BlockSpec((tf, D), lambda b, i, f: (f, 0))] if emit else []
    w_out_shapes = [jax.ShapeDtypeStruct((D, F), BF16), jax.ShapeDtypeStruct((D, F), BF16),
                    jax.ShapeDtypeStruct((F, D), BF16)] if emit else []
    outs = pl.pallas_call(
        functools.partial(_ffn_kernel, tm=tm, u=u, R=R, tiles_per_seq=nm, emit=emit, final_norm=final_norm),
        grid=(G, nm, nf),
        in_specs=[pl.BlockSpec((1, tm, D), lambda b, i, f: (b, i, 0)),
                  w_specs[0], w_specs[1],
                  pl.BlockSpec((FFN_CONV, tf), lambda b, i, f: (0, f)),
                  pl.BlockSpec((1, tf), lambda b, i, f: (0, f)),
                  w_specs[2],
                  pl.BlockSpec((1, tm, D), lambda b, i, f: (b, i, 0)),
                  mod_spec,
                  pl.BlockSpec((1, R, tf), lambda b, i, f: (b, 0, f)),
                  pl.BlockSpec((1, D), lambda b, i, f: (0, 0))],
        out_specs=[pl.BlockSpec((1, tm, D), lambda b, i, f: (b, i, 0)),
                   pl.BlockSpec((1, 1, R, tf), lambda b, i, f: (b, i, 0, f))] + w_out_specs,
        out_shape=[jax.ShapeDtypeStruct((G, rows, D), F32),
                   jax.ShapeDtypeStruct((G, nm, R, F), F32)] + w_out_shapes,
        scratch_shapes=[pltpu.VMEM((R + tm, tf), F32), pltpu.VMEM((nf, R, tf), F32)],
        compiler_params=_cparams(("arbitrary", "arbitrary", "arbitrary")),
        name="ffn",
    )(h2, w_args[0], w_args[1], w_conv, b_conv.reshape(1, F), w_args[2], x, gt, init, g_final.reshape(1, D))
    xo, st = outs[0], outs[1]
    return xo, st[:, nm - 1], tuple(outs[2:])


N_ATTN_COLS = 3 * ATTN_WIDTH
N_GATE_COL = N_ATTN_COLS + 3 * MLSTM_WIDTH
N_TAIL_COL = N_GATE_COL + 2 * MLSTM_HEADS
PACK_BLOCKS = N_PACK // LANE


def _pack_src(j, n_in):
    blk_mi, blk_mf, blk_mqk, blk_mo = C_MI // LANE, C_MF // LANE, C_MQK // LANE, C_MO // LANE
    last = PACK_BLOCKS - 1
    last_real = n_in - (N_TAIL_COL + (last - blk_mo) * LANE)
    start = jnp.where(j < blk_mi, j * LANE,
            jnp.where(j == blk_mi, N_GATE_COL,
            jnp.where(j == blk_mf, N_GATE_COL + MLSTM_HEADS,
            jnp.where(j < blk_mo, N_ATTN_COLS + (j - blk_mqk) * LANE,
            jnp.where(j < last, N_TAIL_COL + (j - blk_mo) * LANE, n_in - LANE)))))
    valid = jnp.where((j == blk_mi) | (j == blk_mf), MLSTM_HEADS, jnp.where(j == last, last_real, LANE))
    return start, valid, last_real


def _pack_w_kernel(wt_hbm, o_ref, buf, sem, *, n_in, depth):
    j = pl.program_id(0)
    nblk = pl.num_programs(0)

    def copies(jj, slot):
        start, _, _ = _pack_src(jj, n_in)
        return [pltpu.make_async_copy(wt_hbm.at[pl.ds(start, LANE), l, :], buf.at[slot, l], sem.at[slot, l])
                for l in range(depth)]

    @pl.when(j == 0)
    def _():
        for cp in copies(j, 0):
            cp.start()

    @pl.when(j + 1 < nblk)
    def _():
        for cp in copies(j + 1, (j + 1) % 2):
            cp.start()

    slot = j % 2
    for cp in copies(j, slot):
        cp.wait()
    _, valid, last_real = _pack_src(j, n_in)
    row = lax.broadcasted_iota(jnp.int32, (LANE, 1), 0)
    for l in range(depth):
        @pl.when(j == nblk - 1)
        def _():
            tail = buf[slot, l, LANE - last_real:LANE, :]
            buf[slot, l, 0:last_real, :] = tail

        x = jnp.where(row < valid, buf[slot, l], 0.0)
        o_ref[l] = x.T.astype(BF16)


def _pack_w_call(w_in):
    depth, d, n_in = w_in.shape
    wt = w_in.transpose(2, 0, 1)
    return pl.pallas_call(
        functools.partial(_pack_w_kernel, n_in=n_in, depth=depth),
        grid=(PACK_BLOCKS,),
        in_specs=[pl.BlockSpec(memory_space=pl.ANY)],
        out_specs=pl.BlockSpec((depth, d, LANE), lambda j: (0, 0, j)),
        out_shape=jax.ShapeDtypeStruct((depth, d, N_PACK), BF16),
        scratch_shapes=[pltpu.VMEM((2, depth, LANE, d), F32), pltpu.SemaphoreType.DMA((2, depth))],
        compiler_params=_cparams(("arbitrary",)),
        name="pack_w_in",
    )(wt)


def _pack_in_proj(w_in, b_in):
    def split(a):
        out, off = [], 0
        for s in IN_SPLITS:
            out.append(a[..., off:off + s])
            off += s
        return out

    def pad(a, n):
        return jnp.pad(a, [(0, 0)] * (a.ndim - 1) + [(0, n - a.shape[-1])])

    def pack(a):
        aq, ak, av, mqk, mv, mi, mf, mo, gq, gk, gv, gg, ga = split(a)
        return jnp.concatenate([aq, ak, av, pad(mi, LANE), pad(mf, LANE), mqk, mv, mo, gq, gk, gv, gg,
                                pad(ga, LANE)], axis=-1)

    return _pack_w_call(w_in), pack(b_in)


def _rope_tables(pos):
    half = HEAD_DIM // 2
    inv_freq = jnp.power(ROPE_THETA, -jnp.arange(half, dtype=F32) / half)
    ang = pos.astype(F32)[:, None] * inv_freq[None, :]
    cos, sin = jnp.cos(ang), jnp.sin(ang)
    zero = jnp.zeros_like(sin)
    reps = LANE // HEAD_DIM
    return (jnp.tile(jnp.concatenate([cos, cos], -1), (1, reps)),
            jnp.tile(jnp.concatenate([-sin, zero], -1), (1, reps)),
            jnp.tile(jnp.concatenate([zero, sin], -1), (1, reps)))


def _pick_tile(n, pref):
    return math.gcd(n, pref)


def _tiles(seq, d_ff):
    return (_pick_tile(seq, 512), _pick_tile(seq, 1024), _pick_tile(seq, 1024), N_PACK // 4,
            _pick_tile(d_ff, 512))


def kernel(x_prompt, x_sample, c_prompt, c_sample, cache_win0_kv, cache_win1_kv, cache_win2_kv, state_mlstm_C, state_mlstm_n, state_mlstm_m, state_mlstm_conv, state_gla_S, state_ffn_conv, w_ada, b_ada, g_norm1, g_norm2, w_in, b_in, w_mconv, b_mconv, g_mlstm, w_gla_a2, b_gla_a2, g_gla, w_out, w_ff_in, w_fconv, b_fconv, w_ff_out, g_final):
    B, S, D = x_prompt.shape
    Bs, Ts, _ = x_sample.shape
    depth = w_ada.shape[0]
    d_ff = w_fconv.shape[-1]
    caches = (cache_win0_kv, cache_win1_kv, cache_win2_kv)
    Ms = Bs * Ts

    n_c = B + Bs
    rows_c = -(-n_c // 8) * 8
    c_all = jnp.pad(jnp.concatenate([c_prompt, c_sample], axis=0), ((0, rows_c - n_c), (0, 0)))
    mod = _ada_call(c_all, w_ada, b_ada).reshape(depth, rows_c, 6, D)

    w_in_p, b_in_p = _pack_in_proj(w_in, b_in)
    w_out_b = w_out.astype(BF16)

    rope_p = _rope_tables(jnp.arange(S))
    rope_s = _rope_tables(PAST_LEN + jnp.repeat(jnp.arange(Ts), Bs))

    tm_p, tm_in, tm_ff, tn, tf = _tiles(S, d_ff)

    xp = x_prompt
    xs = x_sample.transpose(1, 0, 2).reshape(1, Ms, D)
    zeros_p = {
        'mconv': jnp.zeros((B, MLSTM_CONV - 1, 2 * MLSTM_WIDTH), F32),
        'C': jnp.zeros((B, MLSTM_HEADS, HEAD_DIM, HEAD_DIM), F32),
        'n': jnp.zeros((B, MLSTM_HEADS, HEAD_DIM), F32),
        'm': jnp.zeros((B, MLSTM_HEADS), F32),
        'S': jnp.zeros((B, GLA_HEADS, GLA_DK, GLA_DV), F32),
        'fconv': jnp.zeros((B, 8, d_ff), F32),
    }
    names = ('win0', 'win1', 'win2', 'C', 'n', 'm', 'mconv', 'S', 'fconv')
    col_p = {k: [] for k in names}
    col_s = {k: [] for k in names}
    caches_t = [c.transpose(0, 1, 3, 4, 5, 2) for c in caches]
    new_caches = None

    for l in range(depth):
        mp = mod[l, :B]
        ms_ = jnp.tile(mod[l, B:B + Bs], (Ts, 1, 1))

        def mods_p(i):
            return mp[:, i:i + 1, :]

        def mods_s(i):
            return ms_[None, :, i, :]

        h = _normmod_call(xp, g_norm1[l], mods_p(1), mods_p(0), BF16, tm_p)
        proj = _inproj_call(h.reshape(B * S, D), w_in_p, l, b_in_p[l][None], *rope_p, tm_in, tn)
        proj = proj.reshape(B, S, N_PACK)
        os_, ls_ = [], []
        for g, (win, dil) in enumerate(ATTN_GROUPS):
            o, lse = _attn_prompt_call(proj, g, dil)
            os_.append(o)
            ls_.append(lse)
            col_p[f'win{g}'].append(_win_extract_call(proj, g, min(win, S)))
        ya = os_ + ls_
        ym, C, n, m, cso = _mlstm_call(proj, zeros_p['mconv'], w_mconv[l], b_mconv[l], zeros_p['C'],
                                        zeros_p['n'], zeros_p['m'], g_mlstm[l], math.gcd(S, MLSTM_CHUNK))
        yg, Sg = _gla_call(proj, w_gla_a2[l], b_gla_a2[l], zeros_p['S'], g_gla[l], math.gcd(S, GLA_CHUNK))
        xp, h2_p = _outproj_call(ya, ym, yg, w_out_b, l, xp, mods_p(2), g_norm2[l], mods_p(4), mods_p(3), tm_p)
        for k_, v_ in (('C', C), ('n', n), ('m', m), ('mconv', cso), ('S', Sg)):
            col_p[k_].append(v_)

        h = _normmod_call(xs, g_norm1[l], mods_s(1), mods_s(0), BF16, Ms)
        proj = _inproj_call(h.reshape(Ms, D), w_in_p, l, b_in_p[l][None], *rope_s, Ms, tn)
        proj_b = proj.reshape(Ts, Bs, N_PACK).transpose(1, 0, 2)
        ya, new_caches = _attn_sample_call(proj_b[:, :, :3 * ATTN_WIDTH], caches_t, l, new_caches)
        ym, C, n, m, cso = _mlstm_call(proj_b, state_mlstm_conv[l], w_mconv[l], b_mconv[l], state_mlstm_C[l],
                                        state_mlstm_n[l], state_mlstm_m[l], g_mlstm[l], Ts)
        yg, Sg = _gla_call(proj_b, w_gla_a2[l], b_gla_a2[l], state_gla_S[l], g_gla[l], Ts)

        def tmaj(a):
            return a.transpose(1, 0, 2).reshape(1, Ms, a.shape[-1])

        xs, h2 = _outproj_call(tmaj(ya), tmaj(ym), tmaj(yg), w_out_b, l, xs, mods_s(2), g_norm2[l],
                               mods_s(4), mods_s(3), Ms)
        n_st = (FFN_CONV - 1) * Bs
        r_st = max(8, n_st)
        init = state_ffn_conv[l].transpose(1, 0, 2).reshape(1, n_st, d_ff)
        init = jnp.pad(init, ((0, 0), (r_st - n_st, 0), (0, 0)))
        last = l == depth - 1
        xs, fst, w_bf16 = _ffn_call(h2, ('f32', w_ff_in, w_ff_out, l), w_fconv[l], b_fconv[l], xs, mods_s(5),
                                    init, g_final, last, Ms, tf, Bs)
        fst = fst[:, r_st - n_st:].reshape(FFN_CONV - 1, Bs, d_ff).transpose(1, 0, 2)
        for k_, v_ in (('C', C), ('n', n), ('m', m), ('mconv', cso), ('S', Sg), ('fconv', fst)):
            col_s[k_].append(v_)

        xp, fst, _ = _ffn_call(h2_p, ('bf16',) + w_bf16, w_fconv[l], b_fconv[l], xp, mods_p(5),
                               zeros_p['fconv'], g_final, last, tm_ff, tf, 1)
        col_p['fconv'].append(fst[:, 6:8])

    y_prompt = xp
    y_sample = xs.reshape(Ts, Bs, D).transpose(1, 0, 2)
    sp = {k: jnp.stack(v, axis=0) for k, v in col_p.items()}
    ss = {k: jnp.stack(v, axis=0) for k, v in col_s.items() if v}
    for g in range(3):
        ss[f'win{g}'] = new_caches[g].transpose(0, 1, 5, 2, 3, 4)
    return (y_prompt, y_sample, sp['win0'], ss['win0'], sp['win1'], ss['win1'], sp['win2'], ss['win2'],
            sp['C'], ss['C'], sp['n'], ss['n'], sp['m'], ss['m'], sp['mconv'], ss['mconv'],
            sp['S'], ss['S'], sp['fconv'], ss['fconv'])
```

```python
import functools
import math

import jax
import jax.numpy as jnp
from jax import lax
from jax.experimental import pallas as pl
from jax.experimental.pallas import tpu as pltpu

F32 = jnp.float32
BF16 = jnp.bfloat16
HI = lax.Precision.HIGHEST

HEAD_DIM = 64
ATTN_GROUPS = ((128, 1), (512, 4), (2048, 16))
ATTN_HPG = 4
ATTN_WIDTH = 3 * ATTN_HPG * HEAD_DIM
GROUP_W = ATTN_HPG * HEAD_DIM
ATTN_SCALE = HEAD_DIM ** -0.5
ROPE_THETA = 10000.0
BAND = 128
RES_UNROLL = 4
MIXER_BATCH = 4
MLSTM_HEADS = 10
MLSTM_WIDTH = MLSTM_HEADS * HEAD_DIM
MLSTM_CONV = 4
MLSTM_CHUNK = 128
GLA_HEADS = 10
GLA_DK = 32
GLA_DV = 64
GLA_K_WIDTH = GLA_HEADS * GLA_DK
GLA_V_WIDTH = GLA_HEADS * GLA_DV
GLA_RANK = 16
GLA_TAU = 16.0
GLA_CHUNK = 128
GLA_SAFE_DECAY = 60.0
FFN_CONV = 3
FFN_SUB = 512
EPS = 1e-6
PAST_LEN = 8192
NEG = -1e30

LANE = 128
VMEM_LIMIT = 61 * 1024 * 1024

IN_SPLITS = (ATTN_WIDTH, ATTN_WIDTH, ATTN_WIDTH, 2 * MLSTM_WIDTH, MLSTM_WIDTH, MLSTM_HEADS, MLSTM_HEADS,
             MLSTM_WIDTH, GLA_K_WIDTH, GLA_K_WIDTH, GLA_V_WIDTH, GLA_V_WIDTH, GLA_RANK)

C_AQ, C_AK, C_AV = 0, 768, 1536
C_MI, C_MF = 2304, 2432
C_MQK = 2560
C_MV, C_MO = 3840, 4480
C_GQK, C_GV, C_GG = 5120, 5760, 6400
C_GA = 7040
N_PACK = 7168
ROPE_COLS = 2 * ATTN_WIDTH


def _cparams(sem):
    return pltpu.CompilerParams(dimension_semantics=sem, vmem_limit_bytes=VMEM_LIMIT)


def _sigmoid(x):
    return 1.0 / (1.0 + jnp.exp(-x))


def _log_sigmoid(x):
    return jnp.minimum(x, 0.0) - jnp.log(1.0 + jnp.exp(-jnp.abs(x)))


def _dot(a, b):
    return jnp.dot(a, b, preferred_element_type=F32)


def _dot_nt(a, b):
    return lax.dot_general(a, b, (((1,), (1,)), ((), ())), preferred_element_type=F32)


def _split3(x):
    hi = x.astype(BF16)
    r1 = x - hi.astype(F32)
    mid = r1.astype(BF16)
    lo = (r1 - mid.astype(F32)).astype(BF16)
    return hi, mid, lo


def _dot_sel(sel, x, parts=3):
    sb = sel.astype(BF16)
    return sum(_dot(sb, p) for p in _split3(x)[:parts])


def _dot_sel_rhs(x, sel, parts=2):
    sb = sel.astype(BF16)
    return sum(_dot(p, sb) for p in _split3(x)[:parts])


def _dot_sel_nt(x, sel):
    sb = sel.astype(BF16)
    return sum(_dot_nt(p, sb) for p in _split3(x))


def _ada_kernel(c_ref, w_ref, b_ref, o_ref):
    c = c_ref[...]
    s = (c * _sigmoid(c)).astype(BF16)
    o_ref[0] = _dot(s, w_ref[0].astype(BF16)) + b_ref[0]


def _ada_call(c_all, w_ada, b_ada):
    depth, d, n = w_ada.shape
    rows = c_all.shape[0]
    tn = 1024
    return pl.pallas_call(
        _ada_kernel,
        grid=(depth, n // tn),
        in_specs=[pl.BlockSpec((rows, d), lambda l, j: (0, 0)),
                  pl.BlockSpec((1, d, tn), lambda l, j: (l, 0, j)),
                  pl.BlockSpec((1, 1, tn), lambda l, j: (l, 0, j))],
        out_specs=pl.BlockSpec((1, rows, tn), lambda l, j: (l, 0, j)),
        out_shape=jax.ShapeDtypeStruct((depth, rows, n), F32),
        compiler_params=_cparams(("arbitrary", "arbitrary")),
        name="ada",
    )(c_all, w_ada, b_ada.reshape(depth, 1, n))


def _normmod_kernel(x_ref, g_ref, sc_ref, sh_ref, o_ref):
    x = x_ref[0]
    y = x * lax.rsqrt(jnp.mean(x * x, axis=-1, keepdims=True) + EPS) * g_ref[...]
    o_ref[0] = (y * (1.0 + sc_ref[0]) + sh_ref[0]).astype(o_ref.dtype)


def _normmod_call(x, g, sc, sh, out_dtype, tm):
    G, R, D = x.shape
    rr = sc.shape[1]
    mod_spec = (pl.BlockSpec((1, 1, D), lambda b, i: (b, 0, 0)) if rr == 1
                else pl.BlockSpec((1, tm, D), lambda b, i: (b, i, 0)))
    return pl.pallas_call(
        _normmod_kernel,
        grid=(G, R // tm),
        in_specs=[pl.BlockSpec((1, tm, D), lambda b, i: (b, i, 0)),
                  pl.BlockSpec((1, D), lambda b, i: (0, 0)),
                  mod_spec, mod_spec],
        out_specs=pl.BlockSpec((1, tm, D), lambda b, i: (b, i, 0)),
        out_shape=jax.ShapeDtypeStruct((G, R, D), out_dtype),
        compiler_params=_cparams(("arbitrary", "arbitrary")),
        name="normmod",
    )(x, g.reshape(1, D), sc, sh)


def _rope_chunk(x, cos, sin_a, sin_b):
    return x * cos + pltpu.roll(x, 96, 1) * sin_a + pltpu.roll(x, 32, 1) * sin_b


def _inproj_kernel(h_ref, w_ref, b_ref, cos_ref, sa_ref, sb_ref, o_ref, *, tn):
    j = pl.program_id(0)
    o_ref[...] = _dot(h_ref[...], w_ref[...]) + b_ref[...]
    n_chunks = tn // LANE
    rope_tiles = -(-ROPE_COLS // tn)
    for jt in range(rope_tiles):
        chunks = min(n_chunks, (ROPE_COLS - jt * tn) // LANE)

        @pl.when(j == jt)
        def _():
            cos, sa, sb = cos_ref[...], sa_ref[...], sb_ref[...]
            for c in range(chunks):
                sl = slice(c * LANE, (c + 1) * LANE)
                o_ref[:, sl] = _rope_chunk(o_ref[:, sl], cos, sa, sb)


def _inproj_call(h, w, layer, b, cos, sin_a, sin_b, tm, tn):
    M, D = h.shape
    N = w.shape[2]
    tp = cos.shape[0] // tm
    tab = pl.BlockSpec((tm, LANE), lambda j, i: (i % tp, 0))
    return pl.pallas_call(
        functools.partial(_inproj_kernel, tn=tn),
        grid=(N // tn, M // tm),
        in_specs=[pl.BlockSpec((tm, D), lambda j, i: (i, 0)),
                  pl.BlockSpec((None, D, tn), lambda j, i: (layer, 0, j)),
                  pl.BlockSpec((1, tn), lambda j, i: (0, j)),
                  tab, tab, tab],
        out_specs=pl.BlockSpec((tm, tn), lambda j, i: (i, j)),
        out_shape=jax.ShapeDtypeStruct((M, N), F32),
        compiler_params=_cparams(("arbitrary", "arbitrary")),
        name="inproj",
    )(h, w, b, cos, sin_a, sin_b)


def _attn_prompt_kernel(q_ref, kp_ref, kc_ref, vp_ref, vc_ref, o_ref, l_ref, obuf, lbuf, *, dil, nq):
    i = pl.program_id(2)
    row = lax.broadcasted_iota(jnp.int32, (BAND, 2 * BAND), 0)
    col = lax.broadcasted_iota(jnp.int32, (BAND, 2 * BAND), 1)
    band = (col >= row) & (col <= row + BAND)
    band_first = band & ((col >= BAND) | (i > 0))
    nr = nq * BAND

    def residues(rs):
        rows_of, items = [], []
        for u, r in enumerate(rs):
            if dil == 1:
                rows, prev_rows = slice(None), slice(nr - BAND, nr)
            else:
                rows, prev_rows = pl.ds(r, nr, stride=dil), pl.ds((nr - BAND) * dil + r, BAND, stride=dil)
            rows_of.append(rows)
            q_all = q_ref[0, rows, :].astype(BF16)
            k_all = jnp.concatenate([kp_ref[0, prev_rows, :], kc_ref[0, rows, :]], axis=0).astype(BF16)
            v_all = jnp.concatenate([vp_ref[0, prev_rows, :], vc_ref[0, rows, :]], axis=0).astype(BF16)
            for sb in range(nq):
                qs = slice(sb * BAND, (sb + 1) * BAND)
                ks = slice(sb * BAND, (sb + 2) * BAND)
                for h in range(LANE // HEAD_DIM):
                    hs = slice(h * HEAD_DIM, (h + 1) * HEAD_DIM)
                    items.append((u, qs, hs, q_all[qs, hs], k_all[ks, hs], v_all[ks, hs],
                                  band_first if sb == 0 else band))
        ss = [jnp.where(it[6], _dot_nt(it[3], it[4]) * ATTN_SCALE, NEG) for it in items]
        mxs = [jnp.max(s, axis=-1, keepdims=True) for s in ss]
        es = [jnp.exp(s - mx) for s, mx in zip(ss, mxs)]
        dens = [jnp.sum(e, axis=-1, keepdims=True) for e in es]
        pvs = [_dot(e.astype(BF16), it[5]) for e, it in zip(es, items)]
        outs = [pv / den for pv, den in zip(pvs, dens)]
        lses = [jnp.broadcast_to(mx + jnp.log(den), (BAND, HEAD_DIM)) for mx, den in zip(mxs, dens)]
        for it, o, l in zip(items, outs, lses):
            obuf[it[0], it[1], it[2]] = o
            lbuf[it[0], it[1], it[2]] = l
        for u, rows in enumerate(rows_of):
            o_ref[0, rows, :] = obuf[u]
            l_ref[0, rows, :] = lbuf[u]

    if dil == 1:
        residues([0])
    elif dil <= RES_UNROLL:
        residues(list(range(dil)))
    else:
        def body(j, carry):
            residues([j * RES_UNROLL + u for u in range(RES_UNROLL)])
            return carry

        lax.fori_loop(0, dil // RES_UNROLL, body, 0)


def _attn_prompt_call(proj, g, dil):
    B, S, N = proj.shape
    nq = max(1, math.gcd(S, 1024) // (BAND * dil))
    tb = nq * BAND * dil
    nb = S // tb
    hp = GROUP_W // LANE
    cq, ck, cv = (C_AQ + g * GROUP_W) // LANE, (C_AK + g * GROUP_W) // LANE, (C_AV + g * GROUP_W) // LANE

    def spec(cblk, prev):
        if prev:
            return pl.BlockSpec((1, tb, LANE), lambda b, p, i: (b, jnp.maximum(i - 1, 0), cblk + p))
        return pl.BlockSpec((1, tb, LANE), lambda b, p, i: (b, i, cblk + p))

    ospec = pl.BlockSpec((1, tb, LANE), lambda b, p, i: (b, i, p))
    o, lse = pl.pallas_call(
        functools.partial(_attn_prompt_kernel, dil=dil, nq=nq),
        grid=(B, hp, nb),
        in_specs=[spec(cq, False), spec(ck, True), spec(ck, False), spec(cv, True), spec(cv, False)],
        out_specs=[ospec, ospec],
        out_shape=[jax.ShapeDtypeStruct((B, S, GROUP_W), F32)] * 2,
        scratch_shapes=[pltpu.VMEM((min(dil, RES_UNROLL), nq * BAND, LANE), F32)] * 2,
        compiler_params=_cparams(("arbitrary", "arbitrary", "arbitrary")),
        name=f"attn_prompt_g{g}",
    )(proj, proj, proj, proj, proj)
    return o, lse


def _win_extract_kernel(x_ref, o_ref):
    o_ref[0, 0] = x_ref[0].T


def _win_extract_call(proj, g, keep):
    B, S, N = proj.shape
    chunk = min(keep, 512)
    first = (S - keep) // chunk
    cblk = C_AK // GROUP_W + g
    step = (C_AV - C_AK) // GROUP_W
    out = pl.pallas_call(
        _win_extract_kernel,
        grid=(B, 2, keep // chunk),
        in_specs=[pl.BlockSpec((1, chunk, GROUP_W), lambda b, kv, j: (b, first + j, cblk + step * kv))],
        out_specs=pl.BlockSpec((1, 1, GROUP_W, chunk), lambda b, kv, j: (b, kv, 0, j)),
        out_shape=jax.ShapeDtypeStruct((B, 2, GROUP_W, keep), F32),
        compiler_params=_cparams(("arbitrary", "arbitrary", "arbitrary")),
        name=f"win_extract_g{g}",
    )(proj)
    return out.reshape(B, 2, ATTN_HPG, HEAD_DIM, keep).transpose(0, 4, 1, 2, 3)


QROWS = 16


def _attn_sample_kernel(qkv_ref, c0_ref, c1_ref, c2_ref, *rest, T, n_prev):
    y_ref, n0_ref, n1_ref, n2_ref, q16, npad, ysc = rest[n_prev:]
    q16[...] = jnp.zeros_like(q16)
    npad[...] = jnp.zeros_like(npad)
    q16[0:T, :] = qkv_ref[0, :, 0:ATTN_WIDTH]
    lane = lax.broadcasted_iota(jnp.int32, (HEAD_DIM, LANE), 1)
    items = []
    for g, (cref, nref) in enumerate(((c0_ref, n0_ref), (c1_ref, n1_ref), (c2_ref, n2_ref))):
        Lb = cref.shape[-1]
        dil = ATTN_GROUPS[g][1]
        npad[g, 0:T, 0:GROUP_W] = qkv_ref[0, :, C_AK + g * GROUP_W:C_AK + (g + 1) * GROUP_W]
        npad[g, 0:T, GROUP_W:2 * GROUP_W] = qkv_ref[0, :, C_AV + g * GROUP_W:C_AV + (g + 1) * GROUP_W]
        new_t = npad[g].T
        ncols = Lb + LANE
        t = lax.broadcasted_iota(jnp.int32, (QROWS, ncols), 0)
        c = lax.broadcasted_iota(jnp.int32, (QROWS, ncols), 1)
        if dil == 1:
            valid = ((c >= t) & (c < Lb)) | ((c >= Lb) & (c - Lb <= t) & (c - Lb < T))
        else:
            valid = ((c < Lb) & ((c & (dil - 1)) == t)) | (c == Lb + t)
        for h in range(ATTN_HPG):
            rk = slice(h * HEAD_DIM, (h + 1) * HEAD_DIM)
            rv = slice(GROUP_W + h * HEAD_DIM, GROUP_W + (h + 1) * HEAD_DIM)
            kc, vc = cref[0, 0, 0, h], cref[0, 0, 1, h]
            kt = jnp.concatenate([kc, new_t[rk, :]], axis=1).astype(BF16)
            vt = jnp.concatenate([vc, new_t[rv, :]], axis=1).astype(BF16)
            qh = q16[:, g * GROUP_W + h * HEAD_DIM:g * GROUP_W + (h + 1) * HEAD_DIM].astype(BF16)
            items.append((qh, kt, vt, valid))
            for kv, blk, rn in ((0, kc, rk), (1, vc, rv)):
                sh = pltpu.roll(blk, Lb - T, 1)
                newc = pltpu.roll(new_t[rn, :], LANE - T, 1)
                if Lb > LANE:
                    nref[0, 0, kv, h, :, 0:Lb - LANE] = sh[:, 0:Lb - LANE]
                nref[0, 0, kv, h, :, Lb - LANE:Lb] = jnp.where(lane >= LANE - T, newc, sh[:, Lb - LANE:Lb])
    ss = [jnp.where(valid, _dot(qh, kt) * ATTN_SCALE, NEG) for qh, kt, vt, valid in items]
    mxs = [jnp.max(s, axis=-1, keepdims=True) for s in ss]
    es = [jnp.exp(s - mx) for s, mx in zip(ss, mxs)]
    dens = [jnp.sum(e, axis=-1, keepdims=True) for e in es]
    pvs = [_dot_nt(e.astype(BF16), it[2]) for e, it in zip(es, items)]
    outs = [pv / den for pv, den in zip(pvs, dens)]
    lses = [mx + jnp.log(den) for mx, den in zip(mxs, dens)]
    for h in range(ATTN_HPG):
        l0, l1, l2 = lses[h], lses[ATTN_HPG + h], lses[2 * ATTN_HPG + h]
        mx = jnp.maximum(jnp.maximum(l0, l1), l2)
        es = [jnp.exp(l0 - mx), jnp.exp(l1 - mx), jnp.exp(l2 - mx)]
        inv = 1.0 / (es[0] + es[1] + es[2])
        for g in range(3):
            c0 = g * GROUP_W + h * HEAD_DIM
            ysc[:, c0:c0 + HEAD_DIM] = outs[g * ATTN_HPG + h] * (es[g] * inv)
    y_ref[0] = ysc[0:T, :].astype(y_ref.dtype)


def _attn_sample_call(qkv, caches_t, layer, prev):
    B, T, _ = qkv.shape
    depth = caches_t[0].shape[0]
    for g, (win, dil) in enumerate(ATTN_GROUPS):
        assert caches_t[g].shape[-1] == win and win // dil == BAND and (g == 0 or T <= dil), \
            "sample attention assumes full window caches"
    cspecs = [pl.BlockSpec((1, 1) + c.shape[2:], lambda b: (layer, b, 0, 0, 0, 0)) for c in caches_t]
    n_prev = 0 if prev is None else 3
    prev_args = [] if prev is None else list(prev)
    prev_specs = [pl.BlockSpec(memory_space=pl.ANY)] * n_prev
    aliases = {} if prev is None else {4 + g: 1 + g for g in range(3)}
    outs = pl.pallas_call(
        functools.partial(_attn_sample_kernel, T=T, n_prev=n_prev),
        grid=(B,),
        in_specs=[pl.BlockSpec((1, T, 3 * ATTN_WIDTH), lambda b: (b, 0, 0))] + cspecs + prev_specs,
        out_specs=[pl.BlockSpec((1, T, ATTN_WIDTH), lambda b: (b, 0, 0))] + cspecs,
        out_shape=[jax.ShapeDtypeStruct((B, T, ATTN_WIDTH), BF16)]
        + [jax.ShapeDtypeStruct(c.shape, F32) for c in caches_t],
        scratch_shapes=[pltpu.VMEM((QROWS, ATTN_WIDTH), F32),
                        pltpu.VMEM((3, LANE, 2 * GROUP_W), F32),
                        pltpu.VMEM((QROWS, ATTN_WIDTH), F32)],
        input_output_aliases=aliases,
        compiler_params=_cparams(("arbitrary",)),
        name="attn_sample",
    )(qkv, *caches_t, *prev_args)
    return outs[0], outs[1:]


def _mlstm_kernel(mqk_ref, mv_ref, mi_ref, mf_ref, mo_ref, cst_ref, wc_ref, bc_ref, c0_ref, n0_ref, m0_ref,
                  g_ref, tri_ref, y_ref, c_ref, n_ref, m_ref, cso_ref,
                  xbuf, vbuf, cs, ns, ms, tpad, kwp, *, T, Lp, nb):
    c = pl.program_id(1)
    last = pl.num_programs(1) - 1
    W = MLSTM_WIDTH
    bs = range(nb)
    heads = range(MLSTM_HEADS)
    hsl = [slice(h * HEAD_DIM, (h + 1) * HEAD_DIM) for h in heads]
    pairs = [(b, h) for b in bs for h in heads]

    @pl.when(c == 0)
    def _():
        xbuf[...] = jnp.zeros_like(xbuf)
        vbuf[...] = jnp.zeros_like(vbuf)
        tpad[...] = jnp.zeros_like(tpad)
        kwp[...] = jnp.zeros_like(kwp)
        xbuf[:, 5:8, :] = cst_ref[...]
        cs[...] = c0_ref[...]
        ns[...] = n0_ref[...]
        ms[...] = m0_ref[...]

    w = wc_ref[...]
    tri = tri_ref[...]
    causal = tri > 0.5
    rowid = lax.broadcasted_iota(jnp.int32, (Lp, LANE), 0)
    real = rowid < T
    lane_row = lax.broadcasted_iota(jnp.int32, (1, LANE), 1)

    for b in bs:
        xbuf[b, 8:8 + T, :] = mqk_ref[b]
        vbuf[b, 0:T, :] = mv_ref[b]
    ys = [bc_ref[...] + xbuf[b, 8:8 + Lp, :] * w[3:4] + xbuf[b, 7:7 + Lp, :] * w[2:3]
          + xbuf[b, 6:6 + Lp, :] * w[1:2] + xbuf[b, 5:5 + Lp, :] * w[0:1] for b in bs]
    tails = [xbuf[b, 8 + T - 3:8 + T, :] for b in bs]
    for b in bs:
        xbuf[b, 5:8, :] = tails[b]

    @pl.when(c == last)
    def _():
        for b in bs:
            cso_ref[b] = tails[b]

    qks = [y * _sigmoid(y) for y in ys]
    igs, lfs = [], []
    for b in bs:
        tpad[b, 0:T, :] = mi_ref[b]
        igs.append(jnp.where(real, tpad[b, 0:Lp, :], NEG))
        tpad[b, 0:T, :] = mf_ref[b]
        lfs.append(jnp.where(real, _log_sigmoid(tpad[b, 0:Lp, :]), 0.0))
    Fs = [_dot_sel(tri, lf) for lf in lfs]
    m_rows = [ms[b] for b in bs]
    inter_all = [Fs[b] + m_rows[b] for b in bs]
    FTs, ITs = [], []
    for b in bs:
        tpad[b, 0:Lp, :] = Fs[b]
        FTs.append(tpad[b].T)
        tpad[b, 0:Lp, :] = igs[b]
        ITs.append(tpad[b].T)
    v_alls = [vbuf[b, 0:Lp, :] for b in bs]
    n_alls = [ns[b] for b in bs]

    qs = [qks[b][:, hsl[h]] for b, h in pairs]
    ks = [qks[b][:, W + h * HEAD_DIM:W + (h + 1) * HEAD_DIM] * (HEAD_DIM ** -0.5) for b, h in pairs]
    qbs = [q.astype(BF16) for q in qs]
    vbs = [v_alls[b][:, hsl[h]].astype(BF16) for b, h in pairs]
    sqk = [_dot_nt(qb, k.astype(BF16)) for qb, k in zip(qbs, ks)]
    qcs = [_dot(qbs[i], cs[b, h].astype(BF16)) for i, (b, h) in enumerate(pairs)]
    nrows = [n_alls[b][h:h + 1, :] for b, h in pairs]
    qn = [_dot_nt(qbs[i], jnp.broadcast_to(nrows[i], (16, HEAD_DIM)).astype(BF16))[:, 0:1]
          for i in range(len(pairs))]

    fcols = [Fs[b][:, h:h + 1] for b, h in pairs]
    icols = [igs[b][:, h:h + 1] for b, h in pairs]
    inters = [inter_all[b][:, h:h + 1] for b, h in pairs]
    logws = [jnp.where(causal, fcols[i] - FTs[b][h:h + 1, 0:Lp] + ITs[b][h:h + 1, 0:Lp], NEG)
             for i, (b, h) in enumerate(pairs)]
    rmax = [jnp.max(lw, axis=-1, keepdims=True) for lw in logws]
    mts = [jnp.maximum(r, it) for r, it in zip(rmax, inters)]
    As = [jnp.exp(lw - mt) * s for lw, mt, s in zip(logws, mts, sqk)]
    gqs = [jnp.exp(it - mt) for it, mt in zip(inters, mts)]
    asum = [jnp.sum(A, axis=-1, keepdims=True) for A in As]
    dens = [a + g * q for a, g, q in zip(asum, gqs, qn)]
    a_bf = [A.astype(BF16) for A in As]
    avs = [_dot(a, v) for a, v in zip(a_bf, vbs)]
    mLs = [mt[Lp - 1:Lp, :] for mt in mts]
    flast = [Fs[b][Lp - 1:Lp, h:h + 1] for b, h in pairs]
    wLs = [jnp.exp(fl - fc + ic - mL) for fl, fc, ic, mL in zip(flast, fcols, icols, mLs)]
    gls = [jnp.exp(flast[i] + m_rows[b][0:1, h:h + 1] - mLs[i]) for i, (b, h) in enumerate(pairs)]
    kws = [k * wl for k, wl in zip(ks, wLs)]
    for i, (b, h) in enumerate(pairs):
        kwp[b, 0:Lp, hsl[h]] = kws[i]
    ksum = [jnp.sum(kw, axis=0, keepdims=True) for kw in kws]
    for i, (b, h) in enumerate(pairs):
        ns[b, h:h + 1, :] = gls[i] * nrows[i] + ksum[i]
    for b in bs:
        m_new = m_rows[b]
        for h in heads:
            m_new = jnp.where(lane_row == h, mLs[b * MLSTM_HEADS + h], m_new)
        ms[b] = m_new

    kwts = [kwp[b].T for b in bs]
    upds = [_dot(kwts[b][hsl[h], 0:Lp].astype(BF16), vbs[i]) for i, (b, h) in enumerate(pairs)]

    nums = [av + g * qc for av, g, qc in zip(avs, gqs, qcs)]
    hhs = [nu / jnp.maximum(jnp.abs(de), jnp.exp(-mt)) for nu, de, mt in zip(nums, dens, mts)]
    msq = [jnp.mean(hh * hh, axis=-1, keepdims=True) for hh in hhs]
    hns = [hhs[i] * lax.rsqrt(msq[i] + EPS) * g_ref[0:1, hsl[h]] for i, (b, h) in enumerate(pairs)]
    ogs = [_sigmoid(mo_ref[b, :, hsl[h]]) for b, h in pairs]
    for i, (b, h) in enumerate(pairs):
        yo = hns[i] * ogs[i] if Lp == T else hns[i][0:T] * ogs[i]
        y_ref[b, :, hsl[h]] = yo.astype(y_ref.dtype)
    for i, (b, h) in enumerate(pairs):
        cs[b, h] = gls[i] * cs[b, h] + upds[i]

    @pl.when(c == last)
    def _():
        c_ref[...] = cs[...]
        n_ref[...] = ns[...]
        m_ref[...] = ms[...]


def _mlstm_call(proj, conv_state, w_conv, b_conv, c0, n0, m0, g_mlstm, chunk):
    B, S, N = proj.shape
    T = chunk
    nc = S // T
    Lp = max(16, T)
    H = MLSTM_HEADS
    nb = math.gcd(B, MIXER_BATCH)
    m0p = jnp.pad(m0, ((0, 0), (0, LANE - H))).reshape(B, 1, LANE)
    tri = jnp.tril(jnp.ones((Lp, Lp), F32))

    def col(width, off):
        blk = off // width
        return pl.BlockSpec((nb, T, width), lambda b, c: (b, c, blk))

    def const(shape):
        nd = len(shape)
        return pl.BlockSpec(shape, lambda b, c: (0,) * nd)

    def per_b(shape):
        nd = len(shape)
        return pl.BlockSpec((nb,) + shape, lambda b, c: (b,) + (0,) * nd)

    outs = pl.pallas_call(
        functools.partial(_mlstm_kernel, T=T, Lp=Lp, nb=nb),
        grid=(B // nb, nc),
        in_specs=[col(2 * MLSTM_WIDTH, C_MQK), col(MLSTM_WIDTH, C_MV), col(LANE, C_MI), col(LANE, C_MF),
                  col(MLSTM_WIDTH, C_MO), per_b((MLSTM_CONV - 1, 2 * MLSTM_WIDTH)),
                  const((MLSTM_CONV, 2 * MLSTM_WIDTH)), const((1, 2 * MLSTM_WIDTH)),
                  per_b((H, HEAD_DIM, HEAD_DIM)), per_b((H, HEAD_DIM)), per_b((1, LANE)),
                  const((1, MLSTM_WIDTH)), const((Lp, Lp))],
        out_specs=[pl.BlockSpec((nb, T, MLSTM_WIDTH), lambda b, c: (b, c, 0)),
                   per_b((H, HEAD_DIM, HEAD_DIM)), per_b((H, HEAD_DIM)), per_b((1, LANE)),
                   per_b((MLSTM_CONV - 1, 2 * MLSTM_WIDTH))],
        out_shape=[jax.ShapeDtypeStruct((B, S, MLSTM_WIDTH), BF16),
                   jax.ShapeDtypeStruct((B, H, HEAD_DIM, HEAD_DIM), F32),
                   jax.ShapeDtypeStruct((B, H, HEAD_DIM), F32),
                   jax.ShapeDtypeStruct((B, 1, LANE), F32),
                   jax.ShapeDtypeStruct((B, MLSTM_CONV - 1, 2 * MLSTM_WIDTH), F32)],
        scratch_shapes=[pltpu.VMEM((nb, 8 + Lp, 2 * MLSTM_WIDTH), F32),
                        pltpu.VMEM((nb, Lp, MLSTM_WIDTH), F32),
                        pltpu.VMEM((nb, H, HEAD_DIM, HEAD_DIM), F32),
                        pltpu.VMEM((nb, H, HEAD_DIM), F32),
                        pltpu.VMEM((nb, 1, LANE), F32),
                        pltpu.VMEM((nb, LANE, LANE), F32),
                        pltpu.VMEM((nb, LANE, MLSTM_WIDTH), F32)],
        compiler_params=_cparams(("arbitrary", "arbitrary")),
        name="mlstm",
    )(proj, proj, proj, proj, proj, conv_state, w_conv, b_conv.reshape(1, -1), c0, n0, m0p,
      g_mlstm.reshape(1, -1), tri)
    y, C, n, m, cso = outs
    return y, C, n, m[:, 0, :H], cso


def _gla_kernel(gqk_ref, gv_ref, gg_ref, ga_ref, wa_ref, wat_ref, s0_ref, g_ref, tri_ref, bd_ref, ee_ref,
                y_ref, s_ref, qkp, vp, gap, sbd, osc, *, T, Lp, nb):
    c = pl.program_id(1)
    last = pl.num_programs(1) - 1
    KW, VW = GLA_K_WIDTH, GLA_V_WIDTH
    bs = range(nb)
    heads = range(GLA_HEADS)

    @pl.when(c == 0)
    def _():
        qkp[...] = jnp.zeros_like(qkp)
        vp[...] = jnp.zeros_like(vp)
        gap[...] = jnp.zeros_like(gap)
        sbd[...] = jnp.zeros_like(sbd)
        osc[...] = jnp.zeros_like(osc)
        for b in bs:
            for h in heads:
                sbd[b, h * GLA_DK:(h + 1) * GLA_DK, h * GLA_DV:(h + 1) * GLA_DV] = s0_ref[b, h]

    lane = lax.broadcasted_iota(jnp.int32, (T, LANE), 1)
    for b in bs:
        qkp[b, 0:T, :] = gqk_ref[b]
        vp[b, 0:T, :] = gv_ref[b]
        gap[b, 0:T, :] = jnp.where(lane == GLA_RANK, 1.0, ga_ref[b])

    tri = tri_ref[...]
    rowid = lax.broadcasted_iota(jnp.int32, (Lp, KW), 0)
    colid = lax.broadcasted_iota(jnp.int32, (KW, LANE), 1)
    wa_hi, wa_lo, _ = _split3(wa_ref[...])
    wat_hi, wat_lo, _ = _split3(wat_ref[...])
    ga_bs = [gap[b, 0:Lp, :].astype(BF16) for b in bs]
    gat_bs = [gap[b].T.astype(BF16) for b in bs]
    las = [_dot(g, wa_hi) + _dot(g, wa_lo) for g in ga_bs]
    las = [jnp.where(rowid < T, _log_sigmoid(la) / GLA_TAU, 0.0) for la in las]
    bcs = [_dot_sel(tri[0:Lp, 0:Lp], la) for la in las]
    lats = [_dot(wat_hi, g) + _dot(wat_lo, g) for g in gat_bs]
    lats = [jnp.where(colid < T, _log_sigmoid(lat) / GLA_TAU, 0.0) for lat in lats]
    bcts = [_dot_sel_nt(lat, tri) for lat in lats]
    blcols = [bct[:, LANE - 1:LANE] for bct in bcts]
    blrows = [bc[Lp - 1:Lp, :] for bc in bcs]
    kts = [qkp[b].T[KW:2 * KW, :] for b in bs]
    bd = bd_ref[...]
    min_decay = jnp.min(blrows[0])
    for b in bs[1:]:
        min_decay = jnp.minimum(min_decay, jnp.min(blrows[b]))
    safe = min_decay >= -GLA_SAFE_DECAY

    @pl.when(safe)
    def _():
        causal = tri[0:Lp, 0:Lp] > 0.5
        qbs = [(qkp[b, 0:Lp, 0:KW] * (GLA_DK ** -0.5) * jnp.exp(bcs[b])).astype(BF16) for b in bs]
        kbs = [(qkp[b, 0:Lp, KW:2 * KW] * jnp.exp(-bcs[b])).astype(BF16) for b in bs]
        vbs = [vp[b, 0:Lp, :].astype(BF16) for b in bs]
        o_inters = [_dot(qbs[b], sbd[b].astype(BF16)) for b in bs]
        pairs = [(b, h) for b in bs for h in heads]
        scores = [_dot_nt(qbs[b][:, h * GLA_DK:(h + 1) * GLA_DK], kbs[b][:, h * GLA_DK:(h + 1) * GLA_DK])
                  for b, h in pairs]
        a_bf = [jnp.where(causal, s, 0.0).astype(BF16) for s in scores]
        ovs = [_dot(a_bf[i], vbs[b][:, h * GLA_DV:(h + 1) * GLA_DV]) for i, (b, h) in enumerate(pairs)]
        klts = [(kts[b] * jnp.exp(blcols[b] - bcts[b])).astype(BF16) for b in bs]
        upds = [_dot(klts[b], vp[b].astype(BF16)) for b in bs]
        for b in bs:
            osc[b, 0:Lp, :] = o_inters[b] + jnp.concatenate(ovs[b * GLA_HEADS:(b + 1) * GLA_HEADS], axis=1)
            sbd[b] = bd * (jnp.exp(blcols[b]) * sbd[b] + upds[b])

    @pl.when(jnp.logical_not(safe))
    def _():
        srow = lax.broadcasted_iota(jnp.int32, (LANE, LANE), 0)
        for b in bs:
            def body(t, carry, b=b):
                sel = (srow == t).astype(F32)
                lac = jnp.dot(lats[b], sel, precision=HI, preferred_element_type=F32)
                kc = jnp.dot(kts[b], sel, precision=HI, preferred_element_type=F32)
                dec = jnp.concatenate([jnp.exp(lac)] * (VW // LANE), axis=1)
                kcw = jnp.concatenate([kc] * (VW // LANE), axis=1)
                vrow = vp[b, pl.ds(t, 1), :]
                snew = bd * (dec * sbd[b] + kcw * vrow)
                sbd[b] = snew
                qrow = jnp.broadcast_to(qkp[b, pl.ds(t, 1), 0:KW] * (GLA_DK ** -0.5), (8, KW))
                orow = jnp.dot(qrow, snew, precision=HI, preferred_element_type=F32)
                osc[b, pl.ds(t, 1), :] = orow[0:1]
                return carry

            lax.fori_loop(0, T, body, 0)

    os_ = [osc[b, 0:Lp, :] for b in bs]
    mss = [_dot_sel_rhs(o * o, ee_ref[...]) for o in os_]
    ogs = [o * lax.rsqrt(ms + EPS) * g_ref[...] for o, ms in zip(os_, mss)]
    for b in bs:
        gg = gg_ref[b]
        yo = (ogs[b] if Lp == T else ogs[b][0:T]) * (gg * _sigmoid(gg))
        y_ref[b] = yo.astype(y_ref.dtype)

    @pl.when(c == last)
    def _():
        for b in bs:
            for h in heads:
                s_ref[b, h] = sbd[b, h * GLA_DK:(h + 1) * GLA_DK, h * GLA_DV:(h + 1) * GLA_DV]


def _gla_call(proj, w_a2, b_a2, s0, g_gla, chunk):
    B, S, N = proj.shape
    T = chunk
    nc = S // T
    Lp = max(16, T)
    H, KW, VW = GLA_HEADS, GLA_K_WIDTH, GLA_V_WIDTH
    wa = jnp.zeros((LANE, KW), F32).at[:GLA_RANK].set(w_a2).at[GLA_RANK].set(b_a2)
    tri = jnp.tril(jnp.ones((LANE, LANE), F32))
    hk = jnp.arange(KW) // GLA_DK
    hv = jnp.arange(VW) // GLA_DV
    bd = (hk[:, None] == hv[None, :]).astype(F32)
    ee = (hv[:, None] == hv[None, :]).astype(F32) / GLA_DV

    nb = math.gcd(B, MIXER_BATCH)

    def col(width, off):
        blk = off // width
        return pl.BlockSpec((nb, T, width), lambda b, c: (b, c, blk))

    def const(shape):
        nd = len(shape)
        return pl.BlockSpec(shape, lambda b, c: (0,) * nd)

    y, s = pl.pallas_call(
        functools.partial(_gla_kernel, T=T, Lp=Lp, nb=nb),
        grid=(B // nb, nc),
        in_specs=[col(2 * KW, C_GQK), col(VW, C_GV), col(VW, C_GG), col(LANE, C_GA),
                  const((LANE, KW)), const((KW, LANE)),
                  pl.BlockSpec((nb, H, GLA_DK, GLA_DV), lambda b, c: (b, 0, 0, 0)),
                  const((1, VW)), const((LANE, LANE)), const((KW, VW)), const((VW, VW))],
        out_specs=[pl.BlockSpec((nb, T, VW), lambda b, c: (b, c, 0)),
                   pl.BlockSpec((nb, H, GLA_DK, GLA_DV), lambda b, c: (b, 0, 0, 0))],
        out_shape=[jax.ShapeDtypeStruct((B, S, VW), BF16),
                   jax.ShapeDtypeStruct((B, H, GLA_DK, GLA_DV), F32)],
        scratch_shapes=[pltpu.VMEM((nb, LANE, 2 * KW), F32),
                        pltpu.VMEM((nb, LANE, VW), F32),
                        pltpu.VMEM((nb, LANE, LANE), F32),
                        pltpu.VMEM((nb, KW, VW), F32),
                        pltpu.VMEM((nb, LANE, VW), F32)],
        compiler_params=_cparams(("arbitrary", "arbitrary")),
        name="gla",
    )(proj, proj, proj, proj, wa, wa.T, s0, g_gla.reshape(1, -1), tri, bd, ee)
    return y, s


def _outproj_kernel(*refs, n_attn):
    attn_refs = refs[:n_attn]
    ym_ref, yg_ref, w_ref, x_ref, gt_ref, g2_ref, sc_ref, sh_ref, xo_ref, h2_ref = refs[n_attn:]
    a, b = ATTN_WIDTH, ATTN_WIDTH + MLSTM_WIDTH
    if n_attn == 1:
        acc = _dot(attn_refs[0][0], w_ref[0:a, :])
    else:
        l0, l1, l2 = attn_refs[3][0], attn_refs[4][0], attn_refs[5][0]
        mx = jnp.maximum(jnp.maximum(l0, l1), l2)
        es = [jnp.exp(l0 - mx), jnp.exp(l1 - mx), jnp.exp(l2 - mx)]
        inv = 1.0 / (es[0] + es[1] + es[2])
        acc = None
        for g in range(3):
            yg_attn = (attn_refs[g][0] * (es[g] * inv)).astype(BF16)
            part = _dot(yg_attn, w_ref[g * GROUP_W:(g + 1) * GROUP_W, :])
            acc = part if acc is None else part + acc
    acc = acc + _dot(ym_ref[0], w_ref[a:b, :])
    acc = acc + _dot(yg_ref[0], w_ref[b:, :])
    x = x_ref[0] + gt_ref[0] * acc
    xo_ref[0] = x
    y = x * lax.rsqrt(jnp.mean(x * x, axis=-1, keepdims=True) + EPS) * g2_ref[...]
    h2_ref[0] = (y * (1.0 + sc_ref[0]) + sh_ref[0]).astype(h2_ref.dtype)


def _outproj_call(ya, ym, yg, w_out, layer, x, gt, g2, sc, sh, tm):
    G, R, D = x.shape
    rr = gt.shape[1]
    mod_spec = (pl.BlockSpec((1, 1, D), lambda b, i: (b, 0, 0)) if rr == 1
                else pl.BlockSpec((1, tm, D), lambda b, i: (b, i, 0)))

    def act(width):
        return pl.BlockSpec((1, tm, width), lambda b, i: (b, i, 0))

    attn = list(ya) if isinstance(ya, (list, tuple)) else [ya]
    return pl.pallas_call(
        functools.partial(_outproj_kernel, n_attn=len(attn)),
        grid=(G, R // tm),
        in_specs=[act(a_.shape[-1]) for a_ in attn] + [act(MLSTM_WIDTH), act(GLA_V_WIDTH),
                  pl.BlockSpec((None,) + w_out.shape[1:], lambda b, i: (layer, 0, 0)),
                  act(D), mod_spec, pl.BlockSpec((1, D), lambda b, i: (0, 0)), mod_spec, mod_spec],
        out_specs=[act(D), act(D)],
        out_shape=[jax.ShapeDtypeStruct((G, R, D), F32), jax.ShapeDtypeStruct((G, R, D), BF16)],
        compiler_params=_cparams(("arbitrary", "arbitrary")),
        name="outproj",
    )(*attn, ym, yg, w_out, x, gt, g2.reshape(1, D), sc, sh)


def _ffn_kernel(h_ref, wg_in, wu_in, wc_ref, bc_ref, wo_in, x_ref, gt_ref, init_ref, gf_ref,
                xo_ref, st_ref, *rest, tm, u, R, tiles_per_seq, emit, final_norm):
    m = pl.program_id(1)
    f = pl.program_id(2)
    nf = pl.num_programs(2)
    h = h_ref[0]
    if emit:
        wg_ref, wu_ref, wo_ref, ubuf, cbuf = rest
        wg_ref[...] = wg_in[...].astype(BF16)
        wu_ref[...] = wu_in[...].astype(BF16)
        wo_ref[...] = wo_in[...].astype(BF16)
    else:
        ubuf, cbuf = rest
        wg_ref, wu_ref, wo_ref = wg_in, wu_in, wo_in

    @pl.when(m % tiles_per_seq == 0)
    def _():
        ubuf[0:R, :] = init_ref[0]

    @pl.when(m % tiles_per_seq != 0)
    def _():
        ubuf[0:R, :] = cbuf[f]

    @pl.when(f == 0)
    def _():
        xo_ref[0] = jnp.zeros_like(xo_ref[0])

    tf = wg_ref.shape[1]
    chunks = [slice(c0, min(c0 + FFN_SUB, tf)) for c0 in range(0, tf, FFN_SUB)]

    def up_matmuls(cs_):
        return _dot(h, wg_ref[:, cs_]), _dot(h, wu_ref[:, cs_])

    pending = up_matmuls(chunks[0])
    acc = xo_ref[0]
    for ci, cs_ in enumerate(chunks):
        ug, uu = pending
        if ci + 1 < len(chunks):
            pending = up_matmuls(chunks[ci + 1])
        ubuf[R:R + tm, cs_] = ug
        w = wc_ref[:, cs_]
        gate = (bc_ref[:, cs_] + ug * w[2:3] + ubuf[R - u:R - u + tm, cs_] * w[1:2]
                + ubuf[R - 2 * u:R - 2 * u + tm, cs_] * w[0:1])
        act = (gate * _sigmoid(gate) * uu).astype(BF16)
        acc = _dot(act, wo_ref[cs_, :]) + acc
    xo_ref[0] = acc
    tail = ubuf[tm:tm + R, :]
    cbuf[f] = tail
    st_ref[0, 0] = tail

    @pl.when(f == nf - 1)
    def _():
        rows = min(tm, LANE)

        def chunk(r, carry):
            rs = pl.ds(pl.multiple_of(r * rows, rows), rows)
            gt = gt_ref[0] if gt_ref.shape[1] == 1 else gt_ref[0, rs, :]
            y = x_ref[0, rs, :] + gt * xo_ref[0, rs, :]
            if final_norm:
                y = y * lax.rsqrt(jnp.mean(y * y, axis=-1, keepdims=True) + EPS) * gf_ref[...]
            xo_ref[0, rs, :] = y
            return carry

        lax.fori_loop(0, tm // rows, chunk, 0)


def _ffn_call(h2, weights, w_conv, b_conv, x, gt, init, g_final, final_norm, tm, tf, u):
    G, rows, D = x.shape
    F = w_conv.shape[-1]
    R = max(8, 2 * u)
    nm, nf = rows // tm, F // tf
    rr = gt.shape[1]
    mod_spec = (pl.BlockSpec((1, 1, D), lambda b, i, f: (b, 0, 0)) if rr == 1
                else pl.BlockSpec((1, tm, D), lambda b, i, f: (b, i, 0)))
    emit = weights[0] == 'f32'
    if emit:
        assert G == 1 and nm == 1, "weights are emitted tile by tile: every tile must be visited once"
        _, w_in_f32, w_out_f32, layer = weights
        w_args = (w_in_f32, w_in_f32, w_out_f32)
        w_specs = [pl.BlockSpec((None, D, tf), lambda b, i, f: (layer, 0, f)),
                   pl.BlockSpec((None, D, tf), lambda b, i, f: (layer, 0, nf + f)),
                   pl.BlockSpec((None, tf, D), lambda b, i, f: (layer, f, 0))]
    else:
        w_args = weights[1:]
        w_specs = [pl.BlockSpec((D, tf), lambda b, i, f: (0, f)),
                   pl.BlockSpec((D, tf), lambda b, i, f: (0, f)),
                   pl.BlockSpec((tf, D), lambda b, i, f: (f, 0))]
    w_out_specs = [pl.BlockSpec((D, tf), lambda b, i, f: (0, f)),
                   pl.BlockSpec((D, tf), lambda b, i, f: (0, f)),
                   pl.BlockSpec((tf, D), lambda b, i, f: (f, 0))] if emit else []
    w_out_shapes = [jax.ShapeDtypeStruct((D, F), BF16), jax.ShapeDtypeStruct((D, F), BF16),
                    jax.ShapeDtypeStruct((F, D), BF16)] if emit else []
    outs = pl.pallas_call(
        functools.partial(_ffn_kernel, tm=tm, u=u, R=R, tiles_per_seq=nm, emit=emit, final_norm=final_norm),
        grid=(G, nm, nf),
        in_specs=[pl.BlockSpec((1, tm, D), lambda b, i, f: (b, i, 0)),
                  w_specs[0], w_specs[1],
                  pl.BlockSpec((FFN_CONV, tf), lambda b, i, f: (0, f)),
                  pl.BlockSpec((1, tf), lambda b, i, f: (0, f)),
                  w_specs[2],
                  pl.BlockSpec((1, tm, D), lambda b, i, f: (b, i, 0)),
                  mod_spec,
                  pl.BlockSpec((1, R, tf), lambda b, i, f: (b, 0, f)),
                  pl.BlockSpec((1, D), lambda b, i, f: (0, 0))],
        out_specs=[pl.BlockSpec((1, tm, D), lambda b, i, f: (b, i, 0)),
                   pl.BlockSpec((1, 1, R, tf), lambda b, i, f: (b, i, 0, f))] + w_out_specs,
        out_shape=[jax.ShapeDtypeStruct((G, rows, D), F32),
                   jax.ShapeDtypeStruct((G, nm, R, F), F32)] + w_out_shapes,
        scratch_shapes=[pltpu.VMEM((R + tm, tf), F32), pltpu.VMEM((nf, R, tf), F32)],
        compiler_params=_cparams(("arbitrary", "arbitrary", "arbitrary")),
        name="ffn",
    )(h2, w_args[0], w_args[1], w_conv, b_conv.reshape(1, F), w_args[2], x, gt, init, g_final.reshape(1, D))
    xo, st = outs[0], outs[1]
    return xo, st[:, nm - 1], tuple(outs[2:])


N_ATTN_COLS = 3 * ATTN_WIDTH
N_GATE_COL = N_ATTN_COLS + 3 * MLSTM_WIDTH
N_TAIL_COL = N_GATE_COL + 2 * MLSTM_HEADS
PACK_BLOCKS = N_PACK // LANE


def _pack_src(j, n_in):
    blk_mi, blk_mf, blk_mqk, blk_mo = C_MI // LANE, C_MF // LANE, C_MQK // LANE, C_MO // LANE
    last = PACK_BLOCKS - 1
    last_real = n_in - (N_TAIL_COL + (last - blk_mo) * LANE)
    start = jnp.where(j < blk_mi, j * LANE,
            jnp.where(j == blk_mi, N_GATE_COL,
            jnp.where(j == blk_mf, N_GATE_COL + MLSTM_HEADS,
            jnp.where(j < blk_mo, N_ATTN_COLS + (j - blk_mqk) * LANE,
            jnp.where(j < last, N_TAIL_COL + (j - blk_mo) * LANE, n_in - LANE)))))
    valid = jnp.where((j == blk_mi) | (j == blk_mf), MLSTM_HEADS, jnp.where(j == last, last_real, LANE))
    return start, valid, last_real


def _pack_w_kernel(wt_hbm, o_ref, buf, sem, *, n_in, depth):
    j = pl.program_id(0)
    nblk = pl.num_programs(0)

    def copies(jj, slot):
        start, _, _ = _pack_src(jj, n_in)
        return [pltpu.make_async_copy(wt_hbm.at[pl.ds(start, LANE), l, :], buf.at[slot, l], sem.at[slot, l])
                for l in range(depth)]

    @pl.when(j == 0)
    def _():
        for cp in copies(j, 0):
            cp.start()

    @pl.when(j + 1 < nblk)
    def _():
        for cp in copies(j + 1, (j + 1) % 2):
            cp.start()

    slot = j % 2
    for cp in copies(j, slot):
        cp.wait()
    _, valid, last_real = _pack_src(j, n_in)
    row = lax.broadcasted_iota(jnp.int32, (LANE, 1), 0)
    for l in range(depth):
        @pl.when(j == nblk - 1)
        def _():
            tail = buf[slot, l, LANE - last_real:LANE, :]
            buf[slot, l, 0:last_real, :] = tail

        x = jnp.where(row < valid, buf[slot, l], 0.0)
        o_ref[l] = x.T.astype(BF16)


def _pack_w_call(w_in):
    depth, d, n_in = w_in.shape
    wt = w_in.transpose(2, 0, 1)
    return pl.pallas_call(
        functools.partial(_pack_w_kernel, n_in=n_in, depth=depth),
        grid=(PACK_BLOCKS,),
        in_specs=[pl.BlockSpec(memory_space=pl.ANY)],
        out_specs=pl.BlockSpec((depth, d, LANE), lambda j: (0, 0, j)),
        out_shape=jax.ShapeDtypeStruct((depth, d, N_PACK), BF16),
        scratch_shapes=[pltpu.VMEM((2, depth, LANE, d), F32), pltpu.SemaphoreType.DMA((2, depth))],
        compiler_params=_cparams(("arbitrary",)),
        name="pack_w_in",
    )(wt)


def _pack_in_proj(w_in, b_in):
    def split(a):
        out, off = [], 0
        for s in IN_SPLITS:
            out.append(a[..., off:off + s])
            off += s
        return out

    def pad(a, n):
        return jnp.pad(a, [(0, 0)] * (a.ndim - 1) + [(0, n - a.shape[-1])])

    def pack(a):
        aq, ak, av, mqk, mv, mi, mf, mo, gq, gk, gv, gg, ga = split(a)
        return jnp.concatenate([aq, ak, av, pad(mi, LANE), pad(mf, LANE), mqk, mv, mo, gq, gk, gv, gg,
                                pad(ga, LANE)], axis=-1)

    return _pack_w_call(w_in), pack(b_in)


def _rope_tables(pos):
    half = HEAD_DIM // 2
    inv_freq = jnp.power(ROPE_THETA, -jnp.arange(half, dtype=F32) / half)
    ang = pos.astype(F32)[:, None] * inv_freq[None, :]
    cos, sin = jnp.cos(ang), jnp.sin(ang)
    zero = jnp.zeros_like(sin)
    reps = LANE // HEAD_DIM
    return (jnp.tile(jnp.concatenate([cos, cos], -1), (1, reps)),
            jnp.tile(jnp.concatenate([-sin, zero], -1), (1, reps)),
            jnp.tile(jnp.concatenate([zero, sin], -1), (1, reps)))


def _pick_tile(n, pref):
    return math.gcd(n, pref)


def _tiles(seq, d_ff):
    return (_pick_tile(seq, 512), _pick_tile(seq, 1024), _pick_tile(seq, 1024), N_PACK // 4,
            _pick_tile(d_ff, 512))


def kernel(x_prompt, x_sample, c_prompt, c_sample, cache_win0_kv, cache_win1_kv, cache_win2_kv, state_mlstm_C, state_mlstm_n, state_mlstm_m, state_mlstm_conv, state_gla_S, state_ffn_conv, w_ada, b_ada, g_norm1, g_norm2, w_in, b_in, w_mconv, b_mconv, g_mlstm, w_gla_a2, b_gla_a2, g_gla, w_out, w_ff_in, w_fconv, b_fconv, w_ff_out, g_final):
    B, S, D = x_prompt.shape
    Bs, Ts, _ = x_sample.shape
    depth = w_ada.shape[0]
    d_ff = w_fconv.shape[-1]
    caches = (cache_win0_kv, cache_win1_kv, cache_win2_kv)
    Ms = Bs * Ts

    n_c = B + Bs
    rows_c = -(-n_c // 8) * 8
    c_all = jnp.pad(jnp.concatenate([c_prompt, c_sample], axis=0), ((0, rows_c - n_c), (0, 0)))
    mod = _ada_call(c_all, w_ada, b_ada).reshape(depth, rows_c, 6, D)

    w_in_p, b_in_p = _pack_in_proj(w_in, b_in)
    w_out_b = w_out.astype(BF16)

    rope_p = _rope_tables(jnp.arange(S))
    rope_s = _rope_tables(PAST_LEN + jnp.repeat(jnp.arange(Ts), Bs))

    tm_p, tm_in, tm_ff, tn, tf = _tiles(S, d_ff)

    xp = x_prompt
    xs = x_sample.transpose(1, 0, 2).reshape(1, Ms, D)
    zeros_p = {
        'mconv': jnp.zeros((B, MLSTM_CONV - 1, 2 * MLSTM_WIDTH), F32),
        'C': jnp.zeros((B, MLSTM_HEADS, HEAD_DIM, HEAD_DIM), F32),
        'n': jnp.zeros((B, MLSTM_HEADS, HEAD_DIM), F32),
        'm': jnp.zeros((B, MLSTM_HEADS), F32),
        'S': jnp.zeros((B, GLA_HEADS, GLA_DK, GLA_DV), F32),
        'fconv': jnp.zeros((B, 8, d_ff), F32),
    }
    names = ('win0', 'win1', 'win2', 'C', 'n', 'm', 'mconv', 'S', 'fconv')
    col_p = {k: [] for k in names}
    col_s = {k: [] for k in names}
    caches_t = [c.transpose(0, 1, 3, 4, 5, 2) for c in caches]
    new_caches = None

    for l in range(depth):
        mp = mod[l, :B]
        ms_ = jnp.tile(mod[l, B:B + Bs], (Ts, 1, 1))

        def mods_p(i):
            return mp[:, i:i + 1, :]

        def mods_s(i):
            return ms_[None, :, i, :]

        h = _normmod_call(xp, g_norm1[l], mods_p(1), mods_p(0), BF16, tm_p)
        proj = _inproj_call(h.reshape(B * S, D), w_in_p, l, b_in_p[l][None], *rope_p, tm_in, tn)
        proj = proj.reshape(B, S, N_PACK)
        os_, ls_ = [], []
        for g, (win, dil) in enumerate(ATTN_GROUPS):
            o, lse = _attn_prompt_call(proj, g, dil)
            os_.append(o)
            ls_.append(lse)
            col_p[f'win{g}'].append(_win_extract_call(proj, g, min(win, S)))
        ya = os_ + ls_
        ym, C, n, m, cso = _mlstm_call(proj, zeros_p['mconv'], w_mconv[l], b_mconv[l], zeros_p['C'],
                                        zeros_p['n'], zeros_p['m'], g_mlstm[l], math.gcd(S, MLSTM_CHUNK))
        yg, Sg = _gla_call(proj, w_gla_a2[l], b_gla_a2[l], zeros_p['S'], g_gla[l], math.gcd(S, GLA_CHUNK))
        xp, h2_p = _outproj_call(ya, ym, yg, w_out_b, l, xp, mods_p(2), g_norm2[l], mods_p(4), mods_p(3), tm_p)
        for k_, v_ in (('C', C), ('n', n), ('m', m), ('mconv', cso), ('S', Sg)):
            col_p[k_].append(v_)

        h = _normmod_call(xs, g_norm1[l], mods_s(1), mods_s(0), BF16, Ms)
        proj = _inproj_call(h.reshape(Ms, D), w_in_p, l, b_in_p[l][None], *rope_s, Ms, tn)
        proj_b = proj.reshape(Ts, Bs, N_PACK).transpose(1, 0, 2)
        ya, new_caches = _attn_sample_call(proj_b[:, :, :3 * ATTN_WIDTH], caches_t, l, new_caches)
        ym, C, n, m, cso = _mlstm_call(proj_b, state_mlstm_conv[l], w_mconv[l], b_mconv[l], state_mlstm_C[l],
                                        state_mlstm_n[l], state_mlstm_m[l], g_mlstm[l], Ts)
        yg, Sg = _gla_call(proj_b, w_gla_a2[l], b_gla_a2[l], state_gla_S[l], g_gla[l], Ts)

        def tmaj(a):
            return a.transpose(1, 0, 2).reshape(1, Ms, a.shape[-1])

        xs, h2 = _outproj_call(tmaj(ya), tmaj(ym), tmaj(yg), w_out_b, l, xs, mods_s(2), g_norm2[l],
                               mods_s(4), mods_s(3), Ms)
        n_st = (FFN_CONV - 1) * Bs
        r_st = max(8, n_st)
        init = state_ffn_conv[l].transpose(1, 0, 2).reshape(1, n_st, d_ff)
        init = jnp.pad(init, ((0, 0), (r_st - n_st, 0), (0, 0)))
        last = l == depth - 1
        xs, fst, w_bf16 = _ffn_call(h2, ('f32', w_ff_in, w_ff_out, l), w_fconv[l], b_fconv[l], xs, mods_s(5),
                                    init, g_final, last, Ms, tf, Bs)
        fst = fst[:, r_st - n_st:].reshape(FFN_CONV - 1, Bs, d_ff).transpose(1, 0, 2)
        for k_, v_ in (('C', C), ('n', n), ('m', m), ('mconv', cso), ('S', Sg), ('fconv', fst)):
            col_s[k_].append(v_)

        xp, fst, _ = _ffn_call(h2_p, ('bf16',) + w_bf16, w_fconv[l], b_fconv[l], xp, mods_p(5),
                               zeros_p['fconv'], g_final, last, tm_ff, tf, 1)
        col_p['fconv'].append(fst[:, 6:8])

    y_prompt = xp
    y_sample = xs.reshape(Ts, Bs, D).transpose(1, 0, 2)
    sp = {k: jnp.stack(v, axis=0) for k, v in col_p.items()}
    ss = {k: jnp.stack(v, axis=0) for k, v in col_s.items() if v}
    for g in range(3):
        ss[f'win{g}'] = new_caches[g].transpose(0, 1, 5, 2, 3, 4)
    return (y_prompt, y_sample, sp['win0'], ss['win0'], sp['win1'], ss['win1'], sp['win2'], ss['win2'],
            sp['C'], ss['C'], sp['n'], ss['n'], sp['m'], ss['m'], sp['mconv'], ss['mconv'],
            sp['S'], ss['S'], sp['fconv'], ss['fconv'])
```
